```python
import math
import jax
import jax.numpy as jnp
from jax import lax
import numpy as np

D_MODEL = 2048
BATCH = 16
SEQ = 2048
DEPTH = 2

MEM_LEN = 256
GRID_W = 64
Q_BLOCK = 128
NORM_EPS = 1e-6
F32 = jnp.float32

HY_WIDTH = D_MODEL // 2
HY_EMB = 33
HY_BANDS = (HY_EMB - 1) // 2
HY_FILT = 64
HY_SHORT = 3
HY_TARGET = 1e-2
HY_FAST_PCT = 0.3
HY_SLOW_PCT = 1.5
DF_WIDTH = D_MODEL - HY_WIDTH
DF_HEADS = 8
DF_HEAD_DIM = DF_WIDTH // (2 * DF_HEADS)
DF_V_DIM = 2 * DF_HEAD_DIM
DF_QK_WIDTH = DF_HEADS * 2 * DF_HEAD_DIM
EVEN_IN = 3 * HY_WIDTH + 2 * DF_QK_WIDTH + DF_HEADS * DF_V_DIM

HG_WIDTH = D_MODEL // 2
HG_HEADS = 8
HG_DK = HG_WIDTH // HG_HEADS
HG_DV = HG_WIDTH // HG_HEADS
HG_CHUNK = 64
GQ_WIDTH = D_MODEL - HG_WIDTH
GQ_HEADS = 8
GQ_KV_HEADS = 2
GQ_HEAD_DIM = GQ_WIDTH // GQ_HEADS
GQ_GROUP = GQ_HEADS // GQ_KV_HEADS
GQ_KV_WIDTH = GQ_KV_HEADS * GQ_HEAD_DIM
ROPE_THETA = 10000.0
ODD_IN = 5 * HG_WIDTH + GQ_WIDTH + 2 * GQ_KV_WIDTH

CA_HEADS = 4
CA_HEAD_DIM = 128
CA_WIDTH = CA_HEADS * CA_HEAD_DIM
FFN_HIDDEN = 4 * D_MODEL

kernel_name = 'hybrid_hyena_diffattn_hgrn2_axialgqa'


def rms_norm(x, g):
    xf = x.astype(F32)
    y = xf * lax.rsqrt(jnp.mean(xf * xf, axis=-1, keepdims=True) + NORM_EPS)
    return (y * g.astype(F32)).astype(x.dtype)


def alibi_slopes(n_heads):
    return jnp.asarray(np.array([2.0 ** (-8.0 * (h + 1) / n_heads) for h in range(n_heads)], dtype=np.float32))


def sweep_query_blocks(block_fn, q):
    *lead, L, d = q.shape
    nb = L // Q_BLOCK
    qb = jnp.moveaxis(q.reshape(tuple(lead) + (nb, Q_BLOCK, d)), -3, 0)
    starts = jnp.arange(nb, dtype=jnp.int32) * Q_BLOCK
    out = lax.map(lambda a: block_fn(a[0], a[1]), (qb, starts))
    out = jnp.moveaxis(out, 0, -3)
    return out.reshape(out.shape[:-3] + (L, out.shape[-1]))


def short_conv(u, w, b):
    L = u.shape[1]
    p = HY_SHORT // 2
    up = jnp.pad(u, ((0, 0), (p, p), (0, 0)))
    y = sum(up[:, j:j + L] * w[j] for j in range(HY_SHORT))
    return y + b


def hyena_filters(L, w1, b1, w2, b2, w3, b3, w4, sin_freq):
    t = jnp.linspace(0.0, 1.0, L, dtype=F32)[:, None]
    w = (2.0 * math.pi / L) * jnp.arange(L, dtype=F32)[:, None]
    f = jnp.linspace(1e-4, HY_BANDS - 1, HY_BANDS, dtype=F32)[None, :]
    z = jnp.concatenate([t, jnp.cos(f * w), -jnp.sin(f * w)], axis=-1)
    freq = sin_freq.astype(F32)
    h = jnp.sin(freq * (z @ w1.astype(F32) + b1.astype(F32)))
    h = jnp.sin(freq * (h @ w2.astype(F32) + b2.astype(F32)))
    h = jnp.sin(freq * (h @ w3.astype(F32) + b3.astype(F32)))
    h = h @ w4.astype(F32)
    max_decay = math.log(HY_TARGET) / HY_FAST_PCT
    min_decay = math.log(HY_TARGET) / HY_SLOW_PCT
    deltas = jnp.linspace(min_decay, max_decay, HY_WIDTH, dtype=F32)
    window = jnp.exp(-t * jnp.abs(deltas)[None, :])
    h_fwd = h[:, :HY_WIDTH] * window
    h_bwd = h[:, HY_WIDTH:] * window
    return jnp.concatenate([h_fwd, jnp.zeros((1, HY_WIDTH), F32), h_bwd[:0:-1]], axis=0)


def two_sided_fft_conv(v, k_circ, bias):
    L = v.shape[1]
    V = jnp.fft.rfft(v, n=2 * L, axis=1)
    K = jnp.fft.rfft(k_circ, n=2 * L, axis=0)
    y = jnp.fft.irfft(V * K[None], n=2 * L, axis=1)[:, :L]
    return y + v * bias.astype(F32)


def hyena_mixer(u, conv_w, conv_b, w1, b1, w2, b2, w3, b3, w4, sin_freq, bias):
    L = u.shape[1]
    uc = short_conv(u, conv_w, conv_b)
    x1, x2, v = jnp.split(uc, 3, axis=-1)
    k_circ = hyena_filters(L, w1, b1, w2, b2, w3, b3, w4, sin_freq)
    z = two_sided_fft_conv((v * x2).astype(F32), k_circ, bias)
    return (z * x1.astype(F32)).astype(u.dtype)


def diff_attention(q, k, v, lam, slopes):
    L = q.shape[-2]
    scale = DF_HEAD_DIM ** -0.5
    pos_k = jnp.arange(L, dtype=jnp.int32)

    def block(qb, q0):
        s = jnp.einsum('bhjqd,bhjkd->bhjqk', qb, k).astype(F32) * scale
        pos_q = q0 + jnp.arange(Q_BLOCK, dtype=jnp.int32)
        dist = jnp.abs(pos_q[:, None] - pos_k[None, :]).astype(F32)
        p = jax.nn.softmax(s - slopes[:, None, None, None] * dist, axis=-1)
        a = p[:, :, 0] - lam * p[:, :, 1]
        return jnp.einsum('bhqk,bhkv->bhqv', a.astype(v.dtype), v)

    return sweep_query_blocks(block, q)


def even_mixer(h, layer, w_in, w_out, conv_w, conv_b, fw1, fb1, fw2, fb2, fw3, fb3, fw4, sin_freq,
               hy_bias, q_g, k_g, lam_q1, lam_k1, lam_q2, lam_k2, subln_g):
    B, L, _ = h.shape
    proj = h @ w_in
    u_a, q_b, k_b, v_b = jnp.split(
        proj, [3 * HY_WIDTH, 3 * HY_WIDTH + DF_QK_WIDTH, 3 * HY_WIDTH + 2 * DF_QK_WIDTH], axis=-1)
    o_a = hyena_mixer(u_a, conv_w, conv_b, fw1, fb1, fw2, fb2, fw3, fb3, fw4, sin_freq, hy_bias)
    q = rms_norm(q_b.reshape(B, L, DF_HEADS, 2, DF_HEAD_DIM), q_g).transpose(0, 2, 3, 1, 4)
    k = rms_norm(k_b.reshape(B, L, DF_HEADS, 2, DF_HEAD_DIM), k_g).transpose(0, 2, 3, 1, 4)
    v = v_b.reshape(B, L, DF_HEADS, DF_V_DIM).transpose(0, 2, 1, 3)
    lam_init = 0.8 - 0.6 * math.exp(-0.3 * layer)
    lam = (jnp.exp(jnp.sum(lam_q1.astype(F32) * lam_k1.astype(F32)))
           - jnp.exp(jnp.sum(lam_q2.astype(F32) * lam_k2.astype(F32))) + lam_init)
    o_b = diff_attention(q, k, v, lam, alibi_slopes(DF_HEADS))
    o_b = rms_norm(o_b, subln_g) * (1.0 - lam_init)
    o_b = o_b.transpose(0, 2, 1, 3).reshape(B, L, DF_WIDTH)
    return jnp.concatenate([o_a, o_b.astype(o_a.dtype)], axis=-1) @ w_out


def chunk_gla(q, k, v, logf):
    B, H, L, dk = q.shape
    dv = v.shape[-1]
    n = L // HG_CHUNK

    def chunks(a):
        return jnp.moveaxis(a.reshape(B, H, n, HG_CHUNK, a.shape[-1]), 2, 0)

    lower = jnp.tril(jnp.ones((HG_CHUNK, HG_CHUNK), dtype=bool))[:, :, None]

    def step(state, inp):
        qc, kc, vc, gc = inp
        b = jnp.cumsum(gc, axis=2)
        b_last = b[:, :, -1:, :]
        o_inter = jnp.einsum('bhtk,bhkv->bhtv', qc * jnp.exp(b), state)
        rel = b[:, :, :, None, :] - b[:, :, None, :, :]
        decay = jnp.exp(jnp.where(lower, rel, -jnp.inf))
        att = jnp.einsum('bhtk,bhsk,bhtsk->bhts', qc, kc, decay)
        o_intra = jnp.einsum('bhts,bhsv->bhtv', att, vc)
        state = (jnp.exp(b_last[:, :, 0])[..., None] * state
                 + jnp.einsum('bhsk,bhsv->bhkv', kc * jnp.exp(b_last - b), vc))
        return state, o_inter + o_intra

    s0 = jnp.zeros((B, H, dk, dv), F32)
    _, o = lax.scan(step, s0, (chunks(q), chunks(k), chunks(v), chunks(logf)))
    return jnp.moveaxis(o, 0, 2).reshape(B, H, L, dv)


def hgrn2_mixer(q, fz_f, fz_b, i, g, lb_f, lb_b, norm_g):
    B, L, _ = q.shape

    def heads(a):
        return a.reshape(B, L, HG_HEADS, -1).transpose(0, 2, 1, 3).astype(F32)

    qh = heads(q) * HG_DK ** -0.5
    vh = heads(i)

    def direction(fz, lb, reverse):
        f = lb.astype(F32) + (1.0 - lb.astype(F32)) * jax.nn.sigmoid(fz.astype(F32))
        args = (qh, heads(1.0 - f), vh, heads(jnp.log(f)))
        if reverse:
            args = tuple(jnp.flip(a, axis=2) for a in args)
        o = chunk_gla(*args)
        return jnp.flip(o, axis=2) if reverse else o

    o = direction(fz_f, lb_f, False) + direction(fz_b, lb_b, True)
    o = rms_norm(o, norm_g).transpose(0, 2, 1, 3).reshape(B, L, HG_WIDTH)
    return (o * jax.nn.silu(g.astype(F32))).astype(q.dtype)


def axial_rope_tables(L):
    rows = L // GRID_W
    r, c = jnp.meshgrid(jnp.arange(rows, dtype=F32), jnp.arange(GRID_W, dtype=F32), indexing='ij')
    r = r.reshape(-1)
    c = c.reshape(-1)
    half = GQ_HEAD_DIM // 2
    inv = ROPE_THETA ** (-jnp.arange(0, half, 2, dtype=F32) / half)
    ang_r = r[:, None] * inv[None, :]
    ang_c = c[:, None] * inv[None, :]
    ang = jnp.concatenate([ang_r, ang_r, ang_c, ang_c], axis=-1)
    return jnp.cos(ang), jnp.sin(ang)


def rotate_axial(x, cos, sin):
    x1, x2, x3, x4 = jnp.split(x, 4, axis=-1)
    rot = jnp.concatenate([-x2, x1, -x4, x3], axis=-1)
    return (x.astype(F32) * cos + rot.astype(F32) * sin).astype(x.dtype)


def gqa_attention(q, k, v):
    scale = GQ_HEAD_DIM ** -0.5

    def block(qb, q0):
        s = jnp.einsum('bgrqd,bgkd->bgrqk', qb, k).astype(F32) * scale
        p = jax.nn.softmax(s, axis=-1)
        return jnp.einsum('bgrqk,bgkd->bgrqd', p.astype(v.dtype), v)

    return sweep_query_blocks(block, q)


def odd_mixer(h, lb_f, lb_b, w_in, w_out, hg_norm_g, q_g, k_g):
    B, L, _ = h.shape
    proj = h @ w_in
    bounds = [HG_WIDTH, 2 * HG_WIDTH, 3 * HG_WIDTH, 4 * HG_WIDTH, 5 * HG_WIDTH,
              5 * HG_WIDTH + GQ_WIDTH, 5 * HG_WIDTH + GQ_WIDTH + GQ_KV_WIDTH]
    q_c, fz_f, fz_b, i_c, g_c, q_d, k_d, v_d = jnp.split(proj, bounds, axis=-1)
    o_c = hgrn2_mixer(q_c, fz_f, fz_b, i_c, g_c, lb_f, lb_b, hg_norm_g)
    cos, sin = axial_rope_tables(L)
    cos = cos[:, None, :]
    sin = sin[:, None, :]
    q = rotate_axial(rms_norm(q_d.reshape(B, L, GQ_HEADS, GQ_HEAD_DIM), q_g), cos, sin)
    k = rotate_axial(rms_norm(k_d.reshape(B, L, GQ_KV_HEADS, GQ_HEAD_DIM), k_g), cos, sin)
    q = q.reshape(B, L, GQ_KV_HEADS, GQ_GROUP, GQ_HEAD_DIM).transpose(0, 2, 3, 1, 4)
    k = k.transpose(0, 2, 1, 3)
    v = v_d.reshape(B, L, GQ_KV_HEADS, GQ_HEAD_DIM).transpose(0, 2, 1, 3)
    o_d = gqa_attention(q, k, v).transpose(0, 3, 1, 2, 4).reshape(B, L, GQ_WIDTH)
    return jnp.concatenate([o_c, o_d.astype(o_c.dtype)], axis=-1) @ w_out


def memory_cross_attention(h, mem_n, w_q, w_kv, w_out, q_g, k_g):
    B, L, _ = h.shape
    M = mem_n.shape[1]
    q = rms_norm((h @ w_q).reshape(B, L, CA_HEADS, CA_HEAD_DIM), q_g)
    k, v = jnp.split(mem_n @ w_kv, 2, axis=-1)
    k = rms_norm(k.reshape(B, M, CA_HEADS, CA_HEAD_DIM), k_g)
    v = v.reshape(B, M, CA_HEADS, CA_HEAD_DIM)
    s = jnp.einsum('bqhd,bkhd->bhqk', q, k).astype(F32) * CA_HEAD_DIM ** -0.5
    p = jax.nn.softmax(s, axis=-1)
    o = jnp.einsum('bhqk,bkhd->bqhd', p.astype(v.dtype), v).reshape(B, L, CA_WIDTH)
    return o @ w_out


def setup_inputs(seed: int = 0) -> dict:
    key = jax.random.key(seed)
    ks = jax.random.split(key, 64)
    cnt = [0]

    def nrm(shape, scale):
        k = ks[cnt[0]]
        cnt[0] += 1
        return jax.random.normal(k, shape, jnp.float32) * scale

    def gain(shape):
        return 1.0 + nrm(shape, 0.05)

    ne = (DEPTH + 1) // 2
    no = DEPTH // 2
    D = D_MODEL
    return {
        'x': nrm((BATCH, SEQ, D), 1.0),
        'mem': nrm((BATCH, MEM_LEN, D), 1.0),
        'norm_mix_g': gain((DEPTH, D)),
        'norm_mem_q_g': gain((DEPTH, D)),
        'norm_mem_kv_g': gain((DEPTH, D)),
        'norm_ffn_g': gain((DEPTH, D)),
        'ev_w_in': nrm((ne, D, EVEN_IN), D ** -0.5),
        'ev_w_out': nrm((ne, HY_WIDTH + DF_WIDTH, D), (HY_WIDTH + DF_WIDTH) ** -0.5),
        'hy_conv_w': nrm((ne, HY_SHORT, 3 * HY_WIDTH), HY_SHORT ** -0.5),
        'hy_conv_b': nrm((ne, 3 * HY_WIDTH), 0.02),
        'hy_fw1': nrm((ne, HY_EMB, HY_FILT), HY_EMB ** -0.5),
        'hy_fb1': nrm((ne, HY_FILT), 0.02),
        'hy_fw2': nrm((ne, HY_FILT, HY_FILT), HY_FILT ** -0.5),
        'hy_fb2': nrm((ne, HY_FILT), 0.02),
        'hy_fw3': nrm((ne, HY_FILT, HY_FILT), HY_FILT ** -0.5),
        'hy_fb3': nrm((ne, HY_FILT), 0.02),
        'hy_fw4': nrm((ne, HY_FILT, 2 * HY_WIDTH), 0.1 * HY_FILT ** -0.5),
        'hy_sin_freq': gain((ne, HY_FILT)),
        'hy_bias': nrm((ne, HY_WIDTH), 0.5),
        'df_q_g': gain((ne, DF_HEAD_DIM)),
        'df_k_g': gain((ne, DF_HEAD_DIM)),
        'df_lam_q1': nrm((ne, DF_HEAD_DIM), 0.1),
        'df_lam_k1': nrm((ne, DF_HEAD_DIM), 0.1),
        'df_lam_q2': nrm((ne, DF_HEAD_DIM), 0.1),
        'df_lam_k2': nrm((ne, DF_HEAD_DIM), 0.1),
        'df_subln_g': gain((ne, DF_V_DIM)),
        'od_w_in': nrm((no, D, ODD_IN), D ** -0.5),
        'od_w_out': nrm((no, HG_WIDTH + GQ_WIDTH, D), (HG_WIDTH + GQ_WIDTH) ** -0.5),
        'hg_lb_logits': nrm((2, DEPTH, HG_WIDTH), 0.1),
        'hg_norm_g': gain((no, HG_DV)),
        'gq_q_g': gain((no, GQ_HEAD_DIM)),
        'gq_k_g': gain((no, GQ_HEAD_DIM)),
        'ca_w_q': nrm((DEPTH, D, CA_WIDTH), D ** -0.5),
        'ca_w_kv': nrm((DEPTH, D, 2 * CA_WIDTH), D ** -0.5),
        'ca_w_out': nrm((DEPTH, CA_WIDTH, D), CA_WIDTH ** -0.5),
        'ca_q_g': gain((DEPTH, CA_HEAD_DIM)),
        'ca_k_g': gain((DEPTH, CA_HEAD_DIM)),
        'mlp_w1': nrm((DEPTH, D, FFN_HIDDEN), D ** -0.5),
        'mlp_w2': nrm((DEPTH, FFN_HIDDEN, D), FFN_HIDDEN ** -0.5),
    }


def reference(x, mem, norm_mix_g, norm_mem_q_g, norm_mem_kv_g, norm_ffn_g,
              ev_w_in, ev_w_out, hy_conv_w, hy_conv_b, hy_fw1, hy_fb1, hy_fw2, hy_fb2,
              hy_fw3, hy_fb3, hy_fw4, hy_sin_freq, hy_bias,
              df_q_g, df_k_g, df_lam_q1, df_lam_k1, df_lam_q2, df_lam_k2, df_subln_g,
              od_w_in, od_w_out, hg_lb_logits, hg_norm_g, gq_q_g, gq_k_g,
              ca_w_q, ca_w_kv, ca_w_out, ca_q_g, ca_k_g, mlp_w1, mlp_w2):
    lb_soft = jax.nn.softmax(hg_lb_logits.astype(F32), axis=1)
    lbs = jnp.cumsum(lb_soft, axis=1)
    lbs = lbs - lbs[:, :1]
    for layer in range(DEPTH):
        j = layer // 2
        h = rms_norm(x, norm_mix_g[layer])
        if layer % 2 == 0:
            mix = even_mixer(h, layer, ev_w_in[j], ev_w_out[j], hy_conv_w[j], hy_conv_b[j],
                             hy_fw1[j], hy_fb1[j], hy_fw2[j], hy_fb2[j], hy_fw3[j], hy_fb3[j],
                             hy_fw4[j], hy_sin_freq[j], hy_bias[j], df_q_g[j], df_k_g[j],
                             df_lam_q1[j], df_lam_k1[j], df_lam_q2[j], df_lam_k2[j], df_subln_g[j])
        else:
            mix = odd_mixer(h, lbs[0, layer], lbs[1, layer], od_w_in[j], od_w_out[j],
                            hg_norm_g[j], gq_q_g[j], gq_k_g[j])
        x = x + mix.astype(x.dtype)
        x = x + memory_cross_attention(rms_norm(x, norm_mem_q_g[layer]), rms_norm(mem, norm_mem_kv_g[layer]),
                                       ca_w_q[layer], ca_w_kv[layer], ca_w_out[layer],
                                       ca_q_g[layer], ca_k_g[layer]).astype(x.dtype)
        h = rms_norm(x, norm_ffn_g[layer])
        x = x + (jnp.square(jax.nn.relu(h @ mlp_w1[layer])) @ mlp_w2[layer]).astype(x.dtype)
    return x
```

```python
import functools
import math

import numpy as np
import jax
import jax.numpy as jnp
from jax import lax
from jax.experimental import pallas as pl
from jax.experimental.pallas import tpu as pltpu

F32 = jnp.float32
BF16 = jnp.bfloat16
HIGHEST = lax.Precision.HIGHEST

D_MODEL = 2048
NORM_EPS = 1e-6
GRID_W = 64
HY_WIDTH = D_MODEL // 2
HY_EMB = 33
HY_BANDS = (HY_EMB - 1) // 2
HY_FILT = 64
HY_SHORT = 3
HY_TARGET = 1e-2
HY_FAST_PCT = 0.3
HY_SLOW_PCT = 1.5
DF_HEADS = 8
DF_HEAD_DIM = 64
DF_V_DIM = 128
DF_QK_WIDTH = DF_HEADS * 2 * DF_HEAD_DIM
HG_WIDTH = D_MODEL // 2
HG_HEADS = 8
HG_DK = HG_WIDTH // HG_HEADS
GQ_WIDTH = D_MODEL - HG_WIDTH
GQ_HEADS = 8
GQ_KV_HEADS = 2
GQ_HEAD_DIM = GQ_WIDTH // GQ_HEADS
GQ_GROUP = GQ_HEADS // GQ_KV_HEADS
ROPE_THETA = 10000.0
CA_HEADS = 4
CA_HEAD_DIM = 128
CA_WIDTH = CA_HEADS * CA_HEAD_DIM

LANES = 128
SUBLANES = 8
VMEM_LIMIT_BYTES = 56 * 1024 * 1024

HG_CHUNK = 64
HG_BLOCK = SUBLANES


def _params(*sem):
    return pltpu.CompilerParams(dimension_semantics=sem, vmem_limit_bytes=VMEM_LIMIT_BYTES)


def _rms(x, g):
    return x * lax.rsqrt(jnp.mean(x * x, axis=-1, keepdims=True) + NORM_EPS) * g


def _dot(a, b):
    return jnp.dot(a, b, preferred_element_type=F32)


def _dot_nt(a, b):
    return lax.dot_general(a, b, (((1,), (1,)), ((), ())), preferred_element_type=F32)


def _dot_tn(a, b):
    return lax.dot_general(a, b, (((0,), (0,)), ((), ())), preferred_element_type=F32)


def _norm_matmul_kernel(x_ref, g_ref, w_ref, o_ref, hn_ref):
    @pl.when(pl.program_id(1) == 0)
    def _():
        hn_ref[...] = _rms(x_ref[...], g_ref[...]).astype(BF16)

    o_ref[...] = _dot(hn_ref[...], w_ref[...]).astype(o_ref.dtype)


def norm_matmul(x, g, w, *, tm, tn, out_dtype=F32):
    M, D = x.shape
    N = w.shape[1]
    tm = min(tm, M)
    tn = min(tn, N)
    return pl.pallas_call(
        _norm_matmul_kernel,
        grid=(M // tm, N // tn),
        in_specs=[pl.BlockSpec((tm, D), lambda i, j: (i, 0)),
                  pl.BlockSpec((1, D), lambda i, j: (0, 0)),
                  pl.BlockSpec((D, tn), lambda i, j: (0, j))],
        out_specs=pl.BlockSpec((tm, tn), lambda i, j: (i, j)),
        out_shape=jax.ShapeDtypeStruct((M, N), out_dtype),
        scratch_shapes=[pltpu.VMEM((tm, D), BF16)],
        compiler_params=_params("parallel", "arbitrary"),
        name="norm_matmul",
    )(x, g.reshape(1, D), w)


def _proj2_res_kernel(a1_ref, a2_ref, w1_ref, w2_ref, r_ref, o_ref):
    acc = _dot(a1_ref[...], w1_ref[...]) + _dot(a2_ref[...], w2_ref[...])
    o_ref[...] = r_ref[...] + acc


def proj2_residual(a1, a2, w, res, *, tm, tn):
    M, K1 = a1.shape
    K2 = a2.shape[1]
    assert K1 == K2
    N = w.shape[1]
    tm = min(tm, M)
    tn = min(tn, N)
    return pl.pallas_call(
        _proj2_res_kernel,
        grid=(M // tm, N // tn),
        in_specs=[pl.BlockSpec((tm, K1), lambda i, j: (i, 0)),
                  pl.BlockSpec((tm, K2), lambda i, j: (i, 0)),
                  pl.BlockSpec((K1, tn), lambda i, j: (0, j)),
                  pl.BlockSpec((K2, tn), lambda i, j: (1, j)),
                  pl.BlockSpec((tm, tn), lambda i, j: (i, j))],
        out_specs=pl.BlockSpec((tm, tn), lambda i, j: (i, j)),
        out_shape=jax.ShapeDtypeStruct((M, N), F32),
        compiler_params=_params("parallel", "arbitrary"),
        name="proj2_residual",
    )(a1, a2, w, w, res)


def _mlp_kernel(x_ref, g_ref, w1_ref, w2_ref, o_ref, hn_ref, acc_ref):
    j = pl.program_id(1)

    @pl.when(j == 0)
    def _():
        hn_ref[...] = _rms(x_ref[...], g_ref[...]).astype(BF16)
        acc_ref[...] = jnp.zeros_like(acc_ref)

    h1 = jnp.maximum(_dot(hn_ref[...], w1_ref[...]), 0.0)
    acc_ref[...] += _dot((h1 * h1).astype(BF16), w2_ref[...])

    @pl.when(j == pl.num_programs(1) - 1)
    def _():
        o_ref[...] = x_ref[...] + acc_ref[...]


def mlp_residual(x, g, w1, w2, *, tm, th):
    M, D = x.shape
    H = w1.shape[1]
    tm = min(tm, M)
    th = min(th, H)
    return pl.pallas_call(
        _mlp_kernel,
        grid=(M // tm, H // th),
        in_specs=[pl.BlockSpec((tm, D), lambda i, j: (i, 0)),
                  pl.BlockSpec((1, D), lambda i, j: (0, 0)),
                  pl.BlockSpec((D, th), lambda i, j: (0, j)),
                  pl.BlockSpec((th, D), lambda i, j: (j, 0))],
        out_specs=pl.BlockSpec((tm, D), lambda i, j: (i, 0)),
        out_shape=jax.ShapeDtypeStruct((M, D), F32),
        scratch_shapes=[pltpu.VMEM((tm, D), BF16), pltpu.VMEM((tm, D), F32)],
        compiler_params=_params("parallel", "arbitrary"),
        name="mlp_residual",
    )(x, g.reshape(1, D), w1, w2)


def _cross_attn_kernel(x_ref, g_ref, wq_ref, kv_ref, qg_ref, kg_ref, wo_ref, o_ref):
    x = x_ref[0]
    hn = _rms(x, g_ref[...]).astype(BF16)
    q = _dot(hn, wq_ref[...])
    kv = kv_ref[0]
    scale = CA_HEAD_DIM ** -0.5
    outs = []
    for h in range(CA_HEADS):
        sl = slice(h * CA_HEAD_DIM, (h + 1) * CA_HEAD_DIM)
        qh = (_rms(q[:, sl], qg_ref[...]) * scale).astype(BF16)
        kh = _rms(kv[:, sl], kg_ref[...]).astype(BF16)
        vh = kv[:, CA_WIDTH + h * CA_HEAD_DIM:CA_WIDTH + (h + 1) * CA_HEAD_DIM].astype(BF16)
        s = _dot_nt(qh, kh)
        e = jnp.exp(s - jnp.max(s, axis=-1, keepdims=True))
        p = e / jnp.sum(e, axis=-1, keepdims=True)
        outs.append(_dot(p.astype(BF16), vh))
    o = jnp.concatenate(outs, axis=-1).astype(BF16)
    o_ref[0] = x + _dot(o, wo_ref[...])


def cross_attention(x, kv, g, wq, qg, kg, wo, *, tq):
    B, L, D = x.shape
    M = kv.shape[1]
    tq = min(tq, L)
    return pl.pallas_call(
        _cross_attn_kernel,
        grid=(B, L // tq),
        in_specs=[pl.BlockSpec((1, tq, D), lambda b, i: (b, i, 0)),
                  pl.BlockSpec((1, D), lambda b, i: (0, 0)),
                  pl.BlockSpec((D, CA_WIDTH), lambda b, i: (0, 0)),
                  pl.BlockSpec((1, M, 2 * CA_WIDTH), lambda b, i: (b, 0, 0)),
                  pl.BlockSpec((1, CA_HEAD_DIM), lambda b, i: (0, 0)),
                  pl.BlockSpec((1, CA_HEAD_DIM), lambda b, i: (0, 0)),
                  pl.BlockSpec((CA_WIDTH, D), lambda b, i: (0, 0))],
        out_specs=pl.BlockSpec((1, tq, D), lambda b, i: (b, i, 0)),
        out_shape=jax.ShapeDtypeStruct((B, L, D), F32),
        compiler_params=_params("parallel", "arbitrary"),
        name="cross_attention",
    )(x, g.reshape(1, D), wq, kv, qg.reshape(1, -1), kg.reshape(1, -1), wo)


def _half_rms(x, g2, lo_mask):
    sq = x * x
    ms_lo = jnp.sum(jnp.where(lo_mask, sq, 0.0), axis=-1, keepdims=True)
    ms_hi = jnp.sum(jnp.where(lo_mask, 0.0, sq), axis=-1, keepdims=True)
    inv = jnp.where(lo_mask, lax.rsqrt(ms_lo * (1.0 / DF_HEAD_DIM) + NORM_EPS),
                    lax.rsqrt(ms_hi * (1.0 / DF_HEAD_DIM) + NORM_EPS))
    return x * inv * g2


def _diff_attn_kernel(sc_ref, q_ref, k_ref, v_ref, qg_ref, kg_ref, sg_ref, o_ref, kn_ref, vb_ref, *,
                      out_scale):
    h = pl.program_id(1)
    qi = pl.program_id(2)
    tq = q_ref.shape[1]
    L = k_ref.shape[1]
    lane = lax.broadcasted_iota(jnp.int32, (1, LANES), 1)
    lo_mask = lane < DF_HEAD_DIM

    @pl.when(qi == 0)
    def _():
        kn_ref[...] = _half_rms(k_ref[0], kg_ref[...], lo_mask).astype(BF16)
        vb_ref[...] = v_ref[0].astype(BF16)

    lam = sc_ref[0]
    slope = sc_ref[1 + h]
    qn = _half_rms(q_ref[0], qg_ref[...], lo_mask) * (DF_HEAD_DIM ** -0.5)
    q1 = jnp.where(lo_mask, qn, 0.0).astype(BF16)
    q2 = jnp.where(lo_mask, 0.0, qn).astype(BF16)
    kn = kn_ref[...]
    row = lax.broadcasted_iota(jnp.int32, (tq, L), 0) + qi * tq
    col = lax.broadcasted_iota(jnp.int32, (tq, L), 1)
    bias = jnp.abs(row - col).astype(F32) * slope

    def probs(qq):
        s = _dot_nt(qq, kn) - bias
        e = jnp.exp(s - jnp.max(s, axis=-1, keepdims=True))
        return e, jnp.sum(e, axis=-1, keepdims=True)

    e1, z1 = probs(q1)
    e2, z2 = probs(q2)
    a = e1 * (1.0 / z1) - e2 * (lam / z2)
    o = _dot(a.astype(BF16), vb_ref[...])
    o_ref[0] = (_rms(o, sg_ref[...]) * out_scale).astype(o_ref.dtype)


def diff_attention(proj, scal, q_g, k_g, subln_g, *, q_blk0, k_blk0, v_blk0, out_scale, tq):
    B, L, _ = proj.shape
    tq = min(tq, L)
    qg2 = jnp.concatenate([q_g, q_g]).reshape(1, LANES)
    kg2 = jnp.concatenate([k_g, k_g]).reshape(1, LANES)
    return pl.pallas_call(
        functools.partial(_diff_attn_kernel, out_scale=out_scale),
        grid=(B, DF_HEADS, L // tq),
        in_specs=[pl.BlockSpec(memory_space=pltpu.SMEM),
                  pl.BlockSpec((1, tq, LANES), lambda b, h, i: (b, i, q_blk0 + h)),
                  pl.BlockSpec((1, L, LANES), lambda b, h, i: (b, 0, k_blk0 + h)),
                  pl.BlockSpec((1, L, LANES), lambda b, h, i: (b, 0, v_blk0 + h)),
                  pl.BlockSpec((1, LANES), lambda b, h, i: (0, 0)),
                  pl.BlockSpec((1, LANES), lambda b, h, i: (0, 0)),
                  pl.BlockSpec((1, LANES), lambda b, h, i: (0, 0))],
        out_specs=pl.BlockSpec((1, tq, LANES), lambda b, h, i: (b, i, h)),
        out_shape=jax.ShapeDtypeStruct((B, L, DF_HEADS * DF_V_DIM), BF16),
        scratch_shapes=[pltpu.VMEM((L, LANES), BF16), pltpu.VMEM((L, LANES), BF16)],
        compiler_params=_params("parallel", "parallel", "arbitrary"),
        name="diff_attention",
    )(scal, proj, proj, proj, qg2, kg2, subln_g.reshape(1, LANES))


def _rope(x, cos, sin, first_mask):
    q = GQ_HEAD_DIM // 4
    rot = jnp.where(first_mask, -pltpu.roll(x, LANES - q, axis=1), pltpu.roll(x, q, axis=1))
    return x * cos + rot * sin


def _gqa_kernel(q_ref, k_ref, v_ref, cq_ref, sq_ref, ck_ref, sk_ref, qg_ref, kg_ref, o_ref, kn_ref, vb_ref):
    qi = pl.program_id(2)
    lane = lax.broadcasted_iota(jnp.int32, (1, LANES), 1)
    first_mask = (lane % (GQ_HEAD_DIM // 2)) < (GQ_HEAD_DIM // 4)

    @pl.when(qi == 0)
    def _():
        kn = _rms(k_ref[0], kg_ref[...])
        kn_ref[...] = _rope(kn, ck_ref[...], sk_ref[...], first_mask).astype(BF16)
        vb_ref[...] = v_ref[0].astype(BF16)

    kn = kn_ref[...]
    vb = vb_ref[...]
    cos = cq_ref[...]
    sin = sq_ref[...]
    scale = GQ_HEAD_DIM ** -0.5
    for r in range(GQ_GROUP):
        sl = slice(r * GQ_HEAD_DIM, (r + 1) * GQ_HEAD_DIM)
        qn = _rope(_rms(q_ref[0, :, sl], qg_ref[...]), cos, sin, first_mask) * scale
        s = _dot_nt(qn.astype(BF16), kn)
        e = jnp.exp(s - jnp.max(s, axis=-1, keepdims=True))
        p = e * (1.0 / jnp.sum(e, axis=-1, keepdims=True))
        o_ref[0, :, sl] = _dot(p.astype(BF16), vb).astype(o_ref.dtype)


def gqa_attention(proj, cos, sin, q_g, k_g, *, q_blk0, k_blk0, v_blk0, tq):
    B, L, _ = proj.shape
    tq = min(tq, L)
    gw = GQ_GROUP * GQ_HEAD_DIM
    return pl.pallas_call(
        _gqa_kernel,
        grid=(B, GQ_KV_HEADS, L // tq),
        in_specs=[pl.BlockSpec((1, tq, gw), lambda b, g, i: (b, i, q_blk0 + g)),
                  pl.BlockSpec((1, L, LANES), lambda b, g, i: (b, 0, k_blk0 + g)),
                  pl.BlockSpec((1, L, LANES), lambda b, g, i: (b, 0, v_blk0 + g)),
                  pl.BlockSpec((tq, LANES), lambda b, g, i: (i, 0)),
                  pl.BlockSpec((tq, LANES), lambda b, g, i: (i, 0)),
                  pl.BlockSpec((L, LANES), lambda b, g, i: (0, 0)),
                  pl.BlockSpec((L, LANES), lambda b, g, i: (0, 0)),
                  pl.BlockSpec((1, LANES), lambda b, g, i: (0, 0)),
                  pl.BlockSpec((1, LANES), lambda b, g, i: (0, 0))],
        out_specs=pl.BlockSpec((1, tq, gw), lambda b, g, i: (b, i, g)),
        out_shape=jax.ShapeDtypeStruct((B, L, GQ_WIDTH), BF16),
        scratch_shapes=[pltpu.VMEM((L, LANES), BF16), pltpu.VMEM((L, LANES), BF16)],
        compiler_params=_params("parallel", "parallel", "arbitrary"),
        name="gqa_attention",
    )(proj, proj, proj, cos, sin, cos, sin, q_g.reshape(1, LANES), k_g.reshape(1, LANES))


def _hgrn2_chunk(q, fz, v, lb, st_ref, *, reverse):
    C = HG_CHUNK
    nb = C // HG_BLOCK
    f = lb + (1.0 - lb) * jax.nn.sigmoid(fz)
    kk = 1.0 - f
    lg = jnp.log(f)
    r_i = lax.broadcasted_iota(jnp.int32, (C, C), 0)
    c_i = lax.broadcasted_iota(jnp.int32, (C, C), 1)
    tri = jnp.where((r_i <= c_i) if reverse else (r_i >= c_i), 1.0, 0.0).astype(F32)
    b = jnp.dot(tri, lg, precision=HIGHEST, preferred_element_type=F32)
    b_end = b[0:1, :] if reverse else b[C - 1:C, :]

    st = st_ref[...]
    o = _dot_nt((q * jnp.exp(b)).astype(BF16), st.astype(BF16))

    ones = jnp.ones((LANES, LANES), BF16)
    pos = lax.broadcasted_iota(jnp.int32, (nb, HG_BLOCK, LANES), 1)
    b3 = b.reshape(nb, HG_BLOCK, LANES)
    q3 = q.reshape(nb, HG_BLOCK, LANES)
    k3 = kk.reshape(nb, HG_BLOCK, LANES)
    v3 = v.reshape(nb, HG_BLOCK, LANES)
    o3 = jnp.zeros((nb, HG_BLOCK, LANES), F32)
    for s in range(HG_BLOCK):
        valid = (pos <= s) if reverse else (pos >= s)
        d = jnp.where(valid, b3 - b3[:, s:s + 1, :], -jnp.inf)
        p = q3 * k3[:, s:s + 1, :] * jnp.exp(d)
        rs = _dot(p.reshape(C, LANES).astype(BF16), ones).reshape(nb, HG_BLOCK, LANES)
        o3 = o3 + rs * v3[:, s:s + 1, :]
    o = o + o3.reshape(C, LANES)

    rowc = lax.broadcasted_iota(jnp.int32, (C, LANES), 0)
    q_segs = []
    k_segs = []
    for i in range(1, nb):
        if reverse:
            blk = nb - 1 - i
            edge = (blk + 1) * HG_BLOCK
            ref = b[edge:edge + 1, :]
            in_blk = (rowc >= blk * HG_BLOCK) & (rowc < edge)
            is_key = rowc >= edge
        else:
            blk = i
            edge = blk * HG_BLOCK
            ref = b[edge - 1:edge, :]
            in_blk = (rowc >= edge) & (rowc < edge + HG_BLOCK)
            is_key = rowc < edge
        q_segs.append(jnp.where(in_blk, q * jnp.exp(jnp.where(in_blk, b - ref, 0.0)), 0.0).astype(BF16))
        k_segs.append(jnp.where(is_key, kk * jnp.exp(jnp.where(is_key, ref - b, 0.0)), 0.0).astype(BF16))
    att = _dot_nt(jnp.concatenate(q_segs, axis=1), jnp.concatenate(k_segs, axis=1))
    o = o + _dot(att.astype(BF16), v.astype(BF16))

    kd = (kk * jnp.exp(b_end - b)).astype(BF16)
    st_ref[...] = st * jnp.exp(b_end) + _dot_tn(v.astype(BF16), kd)
    return o


def _hgrn2_kernel(q_ref, ff_ref, fb_ref, i_ref, g_ref, lbf_ref, lbb_ref, ng_ref, o_ref, of_ref, ob_ref,
                  sf_ref, sb_ref):
    L = q_ref.shape[1]
    C = HG_CHUNK
    n = L // C
    sf_ref[...] = jnp.zeros_like(sf_ref)
    sb_ref[...] = jnp.zeros_like(sb_ref)
    qscale = HG_DK ** -0.5

    def body(c, carry):
        rf = pl.ds(pl.multiple_of(c * C, C), C)
        rb = pl.ds(pl.multiple_of((n - 1 - c) * C, C), C)
        of_ref[rf, :] = _hgrn2_chunk(q_ref[0, rf, :] * qscale, ff_ref[0, rf, :], i_ref[0, rf, :],
                                     lbf_ref[0], sf_ref, reverse=False)
        ob_ref[rb, :] = _hgrn2_chunk(q_ref[0, rb, :] * qscale, fb_ref[0, rb, :], i_ref[0, rb, :],
                                     lbb_ref[0], sb_ref, reverse=True)
        return carry

    lax.fori_loop(0, n, body, 0)
    o = _rms(of_ref[...] + ob_ref[...], ng_ref[...])
    g = g_ref[0]
    o_ref[0] = (o * (g * jax.nn.sigmoid(g))).astype(o_ref.dtype)


def hgrn2_mixer(proj, lb_f, lb_b, norm_g):
    B, L, _ = proj.shape
    H = HG_HEADS

    def col(k):
        return pl.BlockSpec((1, L, LANES), lambda b, h: (b, 0, k * H + h))

    lbspec = pl.BlockSpec((1, 1, LANES), lambda b, h: (h, 0, 0))
    return pl.pallas_call(
        _hgrn2_kernel,
        grid=(B, H),
        in_specs=[col(0), col(1), col(2), col(3), col(4), lbspec, lbspec,
                  pl.BlockSpec((1, LANES), lambda b, h: (0, 0))],
        out_specs=pl.BlockSpec((1, L, LANES), lambda b, h: (b, 0, h)),
        out_shape=jax.ShapeDtypeStruct((B, L, HG_WIDTH), BF16),
        scratch_shapes=[pltpu.VMEM((L, LANES), F32), pltpu.VMEM((L, LANES), F32),
                        pltpu.VMEM((LANES, LANES), F32), pltpu.VMEM((LANES, LANES), F32)],
        compiler_params=_params("parallel", "arbitrary"),
        name="hgrn2",
    )(proj, proj, proj, proj, proj, lb_f.reshape(H, 1, LANES), lb_b.reshape(H, 1, LANES),
      norm_g.reshape(1, LANES))


def _hyena_filter_kernel(z_ref, t_ref, w1_ref, b1_ref, w2_ref, b2_ref, w3_ref, b3_ref, w4f_ref, w4b_ref,
                         fr_ref, ad_ref, gs_ref, gd_ref, kn_ref):
    def hdot(a, b):
        return jnp.dot(a, b, precision=HIGHEST, preferred_element_type=F32)

    fr = fr_ref[...]
    h = jnp.sin(fr * (hdot(z_ref[...], w1_ref[...]) + b1_ref[...]))
    h = jnp.sin(fr * (hdot(h, w2_ref[...]) + b2_ref[...]))
    h = jnp.sin(fr * (hdot(h, w3_ref[...]) + b3_ref[...]))
    window = jnp.exp(-t_ref[...] * ad_ref[...])
    hf = hdot(h, w4f_ref[...]) * window
    hb = hdot(h, w4b_ref[...]) * window
    row = lax.broadcasted_iota(jnp.int32, hb.shape, 0)
    hb = jnp.where(row == 0, 0.0, hb)
    gs_ref[...] = hf + hb
    gd_ref[...] = hb - hf
    sgn = jnp.where(row % 2 == 0, 1.0, -1.0)
    kn_ref[...] = jnp.sum((hf + hb) * sgn, axis=0, keepdims=True) * (0.5 / hb.shape[0])


def _hyena_spectrum_kernel(a_ref, b_ref, gs_ref, gd_ref, kre_ref, kim_ref, *, n_fft):
    tf = a_ref.shape[0]
    f0 = pl.program_id(1) * tf
    row = lax.broadcasted_iota(jnp.int32, (tf, 1), 0) + f0
    wf = jnp.where(row == 0, 1.0 / n_fft, 2.0 / n_fft)
    kre_ref[...] = wf * jnp.dot(a_ref[...], gs_ref[...], precision=HIGHEST, preferred_element_type=F32)
    kim_ref[...] = wf * jnp.dot(b_ref[...], gd_ref[...], precision=HIGHEST, preferred_element_type=F32)


def hyena_filter_spectrum(L, cos_m, sin_m, w1, b1, w2, b2, w3, b3, w4, sin_freq):
    C = HY_WIDTH
    t = jnp.linspace(0.0, 1.0, L, dtype=F32)[:, None]
    w = (2.0 * math.pi / L) * jnp.arange(L, dtype=F32)[:, None]
    f = jnp.linspace(1e-4, HY_BANDS - 1, HY_BANDS, dtype=F32)[None, :]
    z = jnp.concatenate([t, jnp.cos(f * w), -jnp.sin(f * w)], axis=-1)
    zp = jnp.pad(z, ((0, 0), (0, LANES - HY_EMB)))
    w1p = jnp.pad(w1, ((0, LANES - HY_EMB), (0, 0)))
    max_decay = math.log(HY_TARGET) / HY_FAST_PCT
    min_decay = math.log(HY_TARGET) / HY_SLOW_PCT
    absd = jnp.abs(jnp.linspace(min_decay, max_decay, C, dtype=F32))[None, :]
    tc = min(512, C)
    full = lambda shape: pl.BlockSpec(shape, lambda j: (0, 0))
    gs, gd, knyq = pl.pallas_call(
        _hyena_filter_kernel,
        grid=(C // tc,),
        in_specs=[full((L, LANES)), full((L, 1)), full((LANES, HY_FILT)), full((1, HY_FILT)),
                  full((HY_FILT, HY_FILT)), full((1, HY_FILT)), full((HY_FILT, HY_FILT)), full((1, HY_FILT)),
                  pl.BlockSpec((HY_FILT, tc), lambda j: (0, j)),
                  pl.BlockSpec((HY_FILT, tc), lambda j: (0, C // tc + j)),
                  full((1, HY_FILT)),
                  pl.BlockSpec((1, tc), lambda j: (0, j))],
        out_specs=[pl.BlockSpec((L, tc), lambda j: (0, j)), pl.BlockSpec((L, tc), lambda j: (0, j)),
                   pl.BlockSpec((1, tc), lambda j: (0, j))],
        out_shape=[jax.ShapeDtypeStruct((L, C), F32), jax.ShapeDtypeStruct((L, C), F32),
                   jax.ShapeDtypeStruct((1, C), F32)],
        compiler_params=_params("arbitrary"),
        name="hyena_filter",
    )(zp, t, w1p, b1.reshape(1, -1), w2, b2.reshape(1, -1), w3, b3.reshape(1, -1), w4, w4,
      sin_freq.reshape(1, -1), absd)

    tf = min(256, L)
    kre, kim = pl.pallas_call(
        functools.partial(_hyena_spectrum_kernel, n_fft=2 * L),
        grid=(C // tc, L // tf),
        in_specs=[pl.BlockSpec((tf, L), lambda j, i: (i, 0)),
                  pl.BlockSpec((tf, L), lambda j, i: (i, 0)),
                  pl.BlockSpec((L, tc), lambda j, i: (0, j)),
                  pl.BlockSpec((L, tc), lambda j, i: (0, j))],
        out_specs=[pl.BlockSpec((tf, tc), lambda j, i: (i, j)), pl.BlockSpec((tf, tc), lambda j, i: (i, j))],
        out_shape=[jax.ShapeDtypeStruct((L, C), F32), jax.ShapeDtypeStruct((L, C), F32)],
        compiler_params=_params("parallel", "arbitrary"),
        name="hyena_spectrum",
    )(cos_m, sin_m, gs, gd)
    return kre, kim, knyq


def _short_conv(u, w_ref, b_ref):
    L = u.shape[0]
    row = lax.broadcasted_iota(jnp.int32, (L, 1), 0)
    prev = jnp.where(row == 0, 0.0, pltpu.roll(u, 1, axis=0))
    nxt = jnp.where(row == L - 1, 0.0, pltpu.roll(u, L - 1, axis=0))
    return prev * w_ref[0:1, :] + u * w_ref[1:2, :] + nxt * w_ref[2:3, :] + b_ref[...]


def _hyena_conv_kernel(x1_ref, x2_ref, v_ref, cw1_ref, cw2_ref, cw3_ref, cb1_ref, cb2_ref, cb3_ref,
                       a_ref, b_ref, kre_ref, kim_ref, knyq_ref, bias_ref, o_ref, y_ref, *, tf):
    L = v_ref.shape[1]
    x2 = _short_conv(x2_ref[0], cw2_ref, cb2_ref)
    vv = _short_conv(v_ref[0], cw3_ref, cb3_ref)
    vx = vv * x2
    vxb = vx.astype(BF16)
    sgn = jnp.where(lax.broadcasted_iota(jnp.int32, (L, 1), 0) % 2 == 0, 1.0, -1.0)
    v_nyq = jnp.sum(vx * sgn, axis=0, keepdims=True)
    y_ref[...] = vx * bias_ref[...] + sgn * (v_nyq * knyq_ref[...])
    for fb in range(L // tf):
        fs = slice(fb * tf, (fb + 1) * tf)
        p = _dot(a_ref[fs, :], vxb)
        q = _dot(b_ref[fs, :], vxb)
        kre = kre_ref[fs, :]
        kim = kim_ref[fs, :]
        r1 = (p * kre + q * kim).astype(BF16)
        r2 = (q * kre - p * kim).astype(BF16)
        y_ref[...] += _dot(a_ref[:, fs], r1) + _dot(b_ref[:, fs], r2)
    x1 = _short_conv(x1_ref[0], cw1_ref, cb1_ref)
    o_ref[0] = (y_ref[...] * x1).astype(o_ref.dtype)


def hyena_conv(proj, conv_w, conv_b, cos_b, sin_b, kre, kim, knyq, bias, *, tc):
    B, L, _ = proj.shape
    C = HY_WIDTH
    tc = min(tc, C)
    nc = C // tc
    tf = min(512, L)

    def grp(k):
        return pl.BlockSpec((1, L, tc), lambda c, b: (b, 0, k * nc + c))

    def cw(k):
        return pl.BlockSpec((HY_SHORT, tc), lambda c, b: (0, k * nc + c))

    def cb(k):
        return pl.BlockSpec((1, tc), lambda c, b: (0, k * nc + c))

    tbl = pl.BlockSpec((L, L), lambda c, b: (0, 0), pipeline_mode=pl.Buffered(1))
    chan = pl.BlockSpec((L, tc), lambda c, b: (0, c), pipeline_mode=pl.Buffered(1))
    vec = pl.BlockSpec((1, tc), lambda c, b: (0, c))
    return pl.pallas_call(
        functools.partial(_hyena_conv_kernel, tf=tf),
        grid=(nc, B),
        in_specs=[grp(0), grp(1), grp(2), cw(0), cw(1), cw(2), cb(0), cb(1), cb(2),
                  tbl, tbl, chan, chan, vec, vec],
        out_specs=pl.BlockSpec((1, L, tc), lambda c, b: (b, 0, c)),
        out_shape=jax.ShapeDtypeStruct((B, L, C), BF16),
        scratch_shapes=[pltpu.VMEM((L, tc), F32)],
        compiler_params=_params("parallel", "arbitrary"),
        name="hyena_conv",
    )(proj, proj, proj, conv_w, conv_w, conv_w, conv_b.reshape(1, -1), conv_b.reshape(1, -1),
      conv_b.reshape(1, -1), cos_b, sin_b, kre, kim, knyq, bias.reshape(1, -1))


def _dft_tables(L):
    n = 2 * L
    idx = jnp.arange(L, dtype=jnp.int32)
    m = (idx[:, None] * idx[None, :]) % n
    ang = m.astype(F32) * (2.0 * math.pi / n)
    return jnp.cos(ang), jnp.sin(ang)


def _axial_rope_tables(L):
    rows = L // GRID_W
    r, c = jnp.meshgrid(jnp.arange(rows, dtype=F32), jnp.arange(GRID_W, dtype=F32), indexing='ij')
    r = r.reshape(-1)
    c = c.reshape(-1)
    half = GQ_HEAD_DIM // 2
    inv = ROPE_THETA ** (-jnp.arange(0, half, 2, dtype=F32) / half)
    ang_r = r[:, None] * inv[None, :]
    ang_c = c[:, None] * inv[None, :]
    ang = jnp.concatenate([ang_r, ang_r, ang_c, ang_c], axis=-1)
    return jnp.cos(ang), jnp.sin(ang)


def _even_mixer(x, layer, norm_g, w_in, w_out, conv_w, conv_b, fw1, fb1, fw2, fb2, fw3, fb3, fw4, sin_freq,
                hy_bias, q_g, k_g, lam_q1, lam_k1, lam_q2, lam_k2, subln_g):
    B, L, D = x.shape
    x2 = x.reshape(B * L, D)
    proj = norm_matmul(x2, norm_g, w_in.astype(BF16), tm=1024, tn=512).reshape(B, L, -1)
    cos_m, sin_m = _dft_tables(L)
    kre, kim, knyq = hyena_filter_spectrum(L, cos_m, sin_m, fw1, fb1, fw2, fb2, fw3, fb3, fw4, sin_freq)
    o_a = hyena_conv(proj, conv_w, conv_b, cos_m.astype(BF16), sin_m.astype(BF16), kre, kim, knyq, hy_bias,
                     tc=256)
    lam_init = 0.8 - 0.6 * math.exp(-0.3 * layer)
    lam = (jnp.exp(jnp.sum(lam_q1.astype(F32) * lam_k1.astype(F32)))
           - jnp.exp(jnp.sum(lam_q2.astype(F32) * lam_k2.astype(F32))) + lam_init)
    slopes = jnp.asarray(np.array([2.0 ** (-8.0 * (h + 1) / DF_HEADS) for h in range(DF_HEADS)],
                                  dtype=np.float32))
    scal = jnp.concatenate([lam.reshape(1), slopes]).astype(F32)
    nb_hy = 3 * HY_WIDTH // LANES
    nb_qk = DF_QK_WIDTH // LANES
    o_b = diff_attention(proj, scal, q_g, k_g, subln_g, q_blk0=nb_hy, k_blk0=nb_hy + nb_qk,
                         v_blk0=nb_hy + 2 * nb_qk, out_scale=1.0 - lam_init, tq=256)
    out = proj2_residual(o_a.reshape(B * L, -1), o_b.reshape(B * L, -1), w_out.astype(BF16), x2,
                         tm=1024, tn=1024)
    return out.reshape(B, L, D)


def _odd_mixer(x, norm_g, lb_f, lb_b, w_in, w_out, hg_norm_g, q_g, k_g):
    B, L, D = x.shape
    x2 = x.reshape(B * L, D)
    proj = norm_matmul(x2, norm_g, w_in.astype(BF16), tm=1024, tn=512).reshape(B, L, -1)
    o_c = hgrn2_mixer(proj, lb_f, lb_b, hg_norm_g)
    cos, sin = _axial_rope_tables(L)
    gw = GQ_GROUP * GQ_HEAD_DIM
    q0 = 5 * HG_WIDTH
    k0 = q0 + GQ_WIDTH
    v0 = k0 + GQ_KV_HEADS * GQ_HEAD_DIM
    o_d = gqa_attention(proj, cos, sin, q_g, k_g, q_blk0=q0 // gw, k_blk0=k0 // LANES, v_blk0=v0 // LANES,
                        tq=256)
    out = proj2_residual(o_c.reshape(B * L, -1), o_d.reshape(B * L, -1), w_out.astype(BF16), x2,
                         tm=1024, tn=1024)
    return out.reshape(B, L, D)


def kernel(x, mem, norm_mix_g, norm_mem_q_g, norm_mem_kv_g, norm_ffn_g, ev_w_in, ev_w_out, hy_conv_w, hy_conv_b, hy_fw1, hy_fb1, hy_fw2, hy_fb2, hy_fw3, hy_fb3, hy_fw4, hy_sin_freq, hy_bias, df_q_g, df_k_g, df_lam_q1, df_lam_k1, df_lam_q2, df_lam_k2, df_subln_g, od_w_in, od_w_out, hg_lb_logits, hg_norm_g, gq_q_g, gq_k_g, ca_w_q, ca_w_kv, ca_w_out, ca_q_g, ca_k_g, mlp_w1, mlp_w2):
    B, L, D = x.shape
    M = mem.shape[1]
    depth = norm_mix_g.shape[0]
    lb_soft = jax.nn.softmax(hg_lb_logits.astype(F32), axis=1)
    lbs = jnp.cumsum(lb_soft, axis=1)
    lbs = lbs - lbs[:, :1]
    mem2 = mem.reshape(B * M, D)
    for layer in range(depth):
        j = layer // 2
        if layer % 2 == 0:
            x = _even_mixer(x, layer, norm_mix_g[layer], ev_w_in[j], ev_w_out[j], hy_conv_w[j], hy_conv_b[j],
                            hy_fw1[j], hy_fb1[j], hy_fw2[j], hy_fb2[j], hy_fw3[j], hy_fb3[j], hy_fw4[j],
                            hy_sin_freq[j], hy_bias[j], df_q_g[j], df_k_g[j], df_lam_q1[j], df_lam_k1[j],
                            df_lam_q2[j], df_lam_k2[j], df_subln_g[j])
        else:
            x = _odd_mixer(x, norm_mix_g[layer], lbs[0, layer], lbs[1, layer], od_w_in[j], od_w_out[j],
                           hg_norm_g[j], gq_q_g[j], gq_k_g[j])
        kv = norm_matmul(mem2, norm_mem_kv_g[layer], ca_w_kv[layer].astype(BF16), tm=1024, tn=512)
        x = cross_attention(x, kv.reshape(B, M, -1), norm_mem_q_g[layer], ca_w_q[layer].astype(BF16),
                            ca_q_g[layer], ca_k_g[layer], ca_w_out[layer].astype(BF16), tq=512)
        x = mlp_residual(x.reshape(B * L, D), norm_ffn_g[layer], mlp_w1[layer].astype(BF16),
                         mlp_w2[layer].astype(BF16), tm=512, th=1024).reshape(B, L, D)
    return x
```

```python
import functools
import math

import numpy as np
import jax
import jax.numpy as jnp
from jax import lax
from jax.experimental import pallas as pl
from jax.experimental.pallas import tpu as pltpu

F32 = jnp.float32
BF16 = jnp.bfloat16
HIGHEST = lax.Precision.HIGHEST

D_MODEL = 2048
NORM_EPS = 1e-6
GRID_W = 64
HY_WIDTH = D_MODEL // 2
HY_EMB = 33
HY_BANDS = (HY_EMB - 1) // 2
HY_FILT = 64
HY_SHORT = 3
HY_TARGET = 1e-2
HY_FAST_PCT = 0.3
HY_SLOW_PCT = 1.5
DF_HEADS = 8
DF_HEAD_DIM = 64
DF_V_DIM = 128
DF_QK_WIDTH = DF_HEADS * 2 * DF_HEAD_DIM
HG_WIDTH = D_MODEL // 2
HG_HEADS = 8
HG_DK = HG_WIDTH // HG_HEADS
GQ_WIDTH = D_MODEL - HG_WIDTH
GQ_HEADS = 8
GQ_KV_HEADS = 2
GQ_HEAD_DIM = GQ_WIDTH // GQ_HEADS
GQ_GROUP = GQ_HEADS // GQ_KV_HEADS
ROPE_THETA = 10000.0
CA_HEADS = 4
CA_HEAD_DIM = 128
CA_WIDTH = CA_HEADS * CA_HEAD_DIM

LANES = 128
SUBLANES = 8
VMEM_LIMIT_BYTES = 56 * 1024 * 1024

HG_CHUNK = 64
HG_BLOCK = SUBLANES
HG_INTRA_CHUNKS = 4
LOG2E = 1.4426950408889634


def _params(*sem):
    return pltpu.CompilerParams(dimension_semantics=sem, vmem_limit_bytes=VMEM_LIMIT_BYTES)


def _rms(x, g):
    return x * lax.rsqrt(jnp.mean(x * x, axis=-1, keepdims=True) + NORM_EPS) * g


def _dot(a, b):
    return jnp.dot(a, b, preferred_element_type=F32)


def _dot_nt(a, b):
    return lax.dot_general(a, b, (((1,), (1,)), ((), ())), preferred_element_type=F32)


def _dot_tn(a, b):
    return lax.dot_general(a, b, (((0,), (0,)), ((), ())), preferred_element_type=F32)


def _norm_matmul_kernel(x_ref, g_ref, w_ref, o_ref, hn_ref):
    @pl.when(pl.program_id(1) == 0)
    def _():
        hn_ref[...] = _rms(x_ref[...], g_ref[...]).astype(BF16)

    o_ref[...] = _dot(hn_ref[...], w_ref[...]).astype(o_ref.dtype)


def norm_matmul(x, g, w, *, tm, tn, out_dtype=F32):
    M, D = x.shape
    N = w.shape[1]
    tm = min(tm, M)
    tn = min(tn, N)
    return pl.pallas_call(
        _norm_matmul_kernel,
        grid=(M // tm, N // tn),
        in_specs=[pl.BlockSpec((tm, D), lambda i, j: (i, 0)),
                  pl.BlockSpec((1, D), lambda i, j: (0, 0)),
                  pl.BlockSpec((D, tn), lambda i, j: (0, j))],
        out_specs=pl.BlockSpec((tm, tn), lambda i, j: (i, j)),
        out_shape=jax.ShapeDtypeStruct((M, N), out_dtype),
        scratch_shapes=[pltpu.VMEM((tm, D), BF16)],
        compiler_params=_params("parallel", "arbitrary"),
        name="norm_matmul",
    )(x, g.reshape(1, D), w)


def _proj2_res_kernel(a1_ref, a2_ref, w1_ref, w2_ref, r_ref, o_ref):
    acc = _dot(a1_ref[...], w1_ref[...]) + _dot(a2_ref[...], w2_ref[...])
    o_ref[...] = r_ref[...] + acc


def proj2_residual(a1, a2, w, res, *, tm, tn):
    M, K1 = a1.shape
    K2 = a2.shape[1]
    assert K1 == K2
    N = w.shape[1]
    tm = min(tm, M)
    tn = min(tn, N)
    return pl.pallas_call(
        _proj2_res_kernel,
        grid=(M // tm, N // tn),
        in_specs=[pl.BlockSpec((tm, K1), lambda i, j: (i, 0)),
                  pl.BlockSpec((tm, K2), lambda i, j: (i, 0)),
                  pl.BlockSpec((K1, tn), lambda i, j: (0, j)),
                  pl.BlockSpec((K2, tn), lambda i, j: (1, j)),
                  pl.BlockSpec((tm, tn), lambda i, j: (i, j))],
        out_specs=pl.BlockSpec((tm, tn), lambda i, j: (i, j)),
        out_shape=jax.ShapeDtypeStruct((M, N), F32),
        compiler_params=_params("parallel", "arbitrary"),
        name="proj2_residual",
    )(a1, a2, w, w, res)


def _mlp_kernel(x_ref, g_ref, w1_ref, w2_ref, o_ref, hn_ref, acc_ref):
    j = pl.program_id(1)

    @pl.when(j == 0)
    def _():
        hn_ref[...] = _rms(x_ref[...], g_ref[...]).astype(BF16)
        acc_ref[...] = jnp.zeros_like(acc_ref)

    h1 = jnp.maximum(_dot(hn_ref[...], w1_ref[...]), 0.0)
    acc_ref[...] += _dot((h1 * h1).astype(BF16), w2_ref[...])

    @pl.when(j == pl.num_programs(1) - 1)
    def _():
        o_ref[...] = x_ref[...] + acc_ref[...]


def mlp_residual(x, g, w1, w2, *, tm, th):
    M, D = x.shape
    H = w1.shape[1]
    tm = min(tm, M)
    th = min(th, H)
    return pl.pallas_call(
        _mlp_kernel,
        grid=(M // tm, H // th),
        in_specs=[pl.BlockSpec((tm, D), lambda i, j: (i, 0)),
                  pl.BlockSpec((1, D), lambda i, j: (0, 0)),
                  pl.BlockSpec((D, th), lambda i, j: (0, j)),
                  pl.BlockSpec((th, D), lambda i, j: (j, 0))],
        out_specs=pl.BlockSpec((tm, D), lambda i, j: (i, 0)),
        out_shape=jax.ShapeDtypeStruct((M, D), F32),
        scratch_shapes=[pltpu.VMEM((tm, D), BF16), pltpu.VMEM((tm, D), F32)],
        compiler_params=_params("parallel", "arbitrary"),
        name="mlp_residual",
    )(x, g.reshape(1, D), w1, w2)


def _cross_attn_kernel(x_ref, g_ref, wq_ref, kv_ref, qg_ref, kg_ref, wo_ref, o_ref):
    x = x_ref[0]
    hn = _rms(x, g_ref[...]).astype(BF16)
    q = _dot(hn, wq_ref[...])
    kv = kv_ref[0]
    scale = CA_HEAD_DIM ** -0.5
    outs = []
    for h in range(CA_HEADS):
        sl = slice(h * CA_HEAD_DIM, (h + 1) * CA_HEAD_DIM)
        qh = (_rms(q[:, sl], qg_ref[...]) * scale).astype(BF16)
        kh = _rms(kv[:, sl], kg_ref[...]).astype(BF16)
        vh = kv[:, CA_WIDTH + h * CA_HEAD_DIM:CA_WIDTH + (h + 1) * CA_HEAD_DIM].astype(BF16)
        s = _dot_nt(qh, kh)
        e = jnp.exp(s - jnp.max(s, axis=-1, keepdims=True))
        p = e / jnp.sum(e, axis=-1, keepdims=True)
        outs.append(_dot(p.astype(BF16), vh))
    o = jnp.concatenate(outs, axis=-1).astype(BF16)
    o_ref[0] = x + _dot(o, wo_ref[...])


def cross_attention(x, kv, g, wq, qg, kg, wo, *, tq):
    B, L, D = x.shape
    M = kv.shape[1]
    tq = min(tq, L)
    return pl.pallas_call(
        _cross_attn_kernel,
        grid=(B, L // tq),
        in_specs=[pl.BlockSpec((1, tq, D), lambda b, i: (b, i, 0)),
                  pl.BlockSpec((1, D), lambda b, i: (0, 0)),
                  pl.BlockSpec((D, CA_WIDTH), lambda b, i: (0, 0)),
                  pl.BlockSpec((1, M, 2 * CA_WIDTH), lambda b, i: (b, 0, 0)),
                  pl.BlockSpec((1, CA_HEAD_DIM), lambda b, i: (0, 0)),
                  pl.BlockSpec((1, CA_HEAD_DIM), lambda b, i: (0, 0)),
                  pl.BlockSpec((CA_WIDTH, D), lambda b, i: (0, 0))],
        out_specs=pl.BlockSpec((1, tq, D), lambda b, i: (b, i, 0)),
        out_shape=jax.ShapeDtypeStruct((B, L, D), F32),
        compiler_params=_params("parallel", "arbitrary"),
        name="cross_attention",
    )(x, g.reshape(1, D), wq, kv, qg.reshape(1, -1), kg.reshape(1, -1), wo)


def _half_rms(x, g2, lo_mask):
    sq = x * x
    ms_lo = jnp.sum(jnp.where(lo_mask, sq, 0.0), axis=-1, keepdims=True)
    ms_hi = jnp.sum(jnp.where(lo_mask, 0.0, sq), axis=-1, keepdims=True)
    inv = jnp.where(lo_mask, lax.rsqrt(ms_lo * (1.0 / DF_HEAD_DIM) + NORM_EPS),
                    lax.rsqrt(ms_hi * (1.0 / DF_HEAD_DIM) + NORM_EPS))
    return x * inv * g2


def _diff_attn_kernel(sc_ref, q_ref, k_ref, v_ref, qg_ref, kg_ref, sg_ref, o_ref, kn_ref, vb_ref, *,
                      out_scale):
    h = pl.program_id(1)
    qi = pl.program_id(2)
    tq = q_ref.shape[1]
    L = k_ref.shape[1]
    lane = lax.broadcasted_iota(jnp.int32, (1, LANES), 1)
    lo_mask = lane < DF_HEAD_DIM

    @pl.when(qi == 0)
    def _():
        kn_ref[...] = _half_rms(k_ref[0], kg_ref[...], lo_mask).astype(BF16)
        vb_ref[...] = v_ref[0].astype(BF16)

    lam = sc_ref[0]
    slope = sc_ref[1 + h]
    qn = _half_rms(q_ref[0], qg_ref[...], lo_mask) * (DF_HEAD_DIM ** -0.5)
    q1 = jnp.where(lo_mask, qn, 0.0).astype(BF16)
    q2 = jnp.where(lo_mask, 0.0, qn).astype(BF16)
    kn = kn_ref[...]
    row = lax.broadcasted_iota(jnp.int32, (tq, L), 0) + qi * tq
    col = lax.broadcasted_iota(jnp.int32, (tq, L), 1)
    bias = jnp.abs(row - col).astype(F32) * slope

    def probs(qq):
        s = _dot_nt(qq, kn) - bias
        e = jnp.exp(s - jnp.max(s, axis=-1, keepdims=True))
        return e, jnp.sum(e, axis=-1, keepdims=True)

    e1, z1 = probs(q1)
    e2, z2 = probs(q2)
    a = e1 * (1.0 / z1) - e2 * (lam / z2)
    o = _dot(a.astype(BF16), vb_ref[...])
    o_ref[0] = (_rms(o, sg_ref[...]) * out_scale).astype(o_ref.dtype)


def diff_attention(proj, scal, q_g, k_g, subln_g, *, q_blk0, k_blk0, v_blk0, out_scale, tq):
    B, L, _ = proj.shape
    tq = min(tq, L)
    qg2 = jnp.concatenate([q_g, q_g]).reshape(1, LANES)
    kg2 = jnp.concatenate([k_g, k_g]).reshape(1, LANES)
    return pl.pallas_call(
        functools.partial(_diff_attn_kernel, out_scale=out_scale),
        grid=(B, DF_HEADS, L // tq),
        in_specs=[pl.BlockSpec(memory_space=pltpu.SMEM),
                  pl.BlockSpec((1, tq, LANES), lambda b, h, i: (b, i, q_blk0 + h)),
                  pl.BlockSpec((1, L, LANES), lambda b, h, i: (b, 0, k_blk0 + h)),
                  pl.BlockSpec((1, L, LANES), lambda b, h, i: (b, 0, v_blk0 + h)),
                  pl.BlockSpec((1, LANES), lambda b, h, i: (0, 0)),
                  pl.BlockSpec((1, LANES), lambda b, h, i: (0, 0)),
                  pl.BlockSpec((1, LANES), lambda b, h, i: (0, 0))],
        out_specs=pl.BlockSpec((1, tq, LANES), lambda b, h, i: (b, i, h)),
        out_shape=jax.ShapeDtypeStruct((B, L, DF_HEADS * DF_V_DIM), BF16),
        scratch_shapes=[pltpu.VMEM((L, LANES), BF16), pltpu.VMEM((L, LANES), BF16)],
        compiler_params=_params("parallel", "parallel", "arbitrary"),
        name="diff_attention",
    )(scal, proj, proj, proj, qg2, kg2, subln_g.reshape(1, LANES))


def _rope(x, cos, sin, first_mask):
    q = GQ_HEAD_DIM // 4
    rot = jnp.where(first_mask, -pltpu.roll(x, LANES - q, axis=1), pltpu.roll(x, q, axis=1))
    return x * cos + rot * sin


def _gqa_kernel(q_ref, k_ref, v_ref, cq_ref, sq_ref, ck_ref, sk_ref, qg_ref, kg_ref, o_ref, kn_ref, vb_ref):
    qi = pl.program_id(2)
    lane = lax.broadcasted_iota(jnp.int32, (1, LANES), 1)
    first_mask = (lane % (GQ_HEAD_DIM // 2)) < (GQ_HEAD_DIM // 4)

    @pl.when(qi == 0)
    def _():
        kn = _rms(k_ref[0], kg_ref[...])
        kn_ref[...] = _rope(kn, ck_ref[...], sk_ref[...], first_mask).astype(BF16)
        vb_ref[...] = v_ref[0].astype(BF16)

    kn = kn_ref[...]
    vb = vb_ref[...]
    cos = cq_ref[...]
    sin = sq_ref[...]
    scale = GQ_HEAD_DIM ** -0.5
    for r in range(GQ_GROUP):
        sl = slice(r * GQ_HEAD_DIM, (r + 1) * GQ_HEAD_DIM)
        qn = _rope(_rms(q_ref[0, :, sl], qg_ref[...]), cos, sin, first_mask) * scale
        s = _dot_nt(qn.astype(BF16), kn)
        e = jnp.exp(s - jnp.max(s, axis=-1, keepdims=True))
        p = e * (1.0 / jnp.sum(e, axis=-1, keepdims=True))
        o_ref[0, :, sl] = _dot(p.astype(BF16), vb).astype(o_ref.dtype)


def gqa_attention(proj, cos, sin, q_g, k_g, *, q_blk0, k_blk0, v_blk0, tq):
    B, L, _ = proj.shape
    tq = min(tq, L)
    gw = GQ_GROUP * GQ_HEAD_DIM
    return pl.pallas_call(
        _gqa_kernel,
        grid=(B, GQ_KV_HEADS, L // tq),
        in_specs=[pl.BlockSpec((1, tq, gw), lambda b, g, i: (b, i, q_blk0 + g)),
                  pl.BlockSpec((1, L, LANES), lambda b, g, i: (b, 0, k_blk0 + g)),
                  pl.BlockSpec((1, L, LANES), lambda b, g, i: (b, 0, v_blk0 + g)),
                  pl.BlockSpec((tq, LANES), lambda b, g, i: (i, 0)),
                  pl.BlockSpec((tq, LANES), lambda b, g, i: (i, 0)),
                  pl.BlockSpec((L, LANES), lambda b, g, i: (0, 0)),
                  pl.BlockSpec((L, LANES), lambda b, g, i: (0, 0)),
                  pl.BlockSpec((1, LANES), lambda b, g, i: (0, 0)),
                  pl.BlockSpec((1, LANES), lambda b, g, i: (0, 0))],
        out_specs=pl.BlockSpec((1, tq, gw), lambda b, g, i: (b, i, g)),
        out_shape=jax.ShapeDtypeStruct((B, L, GQ_WIDTH), BF16),
        scratch_shapes=[pltpu.VMEM((L, LANES), BF16), pltpu.VMEM((L, LANES), BF16)],
        compiler_params=_params("parallel", "parallel", "arbitrary"),
        name="gqa_attention",
    )(proj, proj, proj, cos, sin, cos, sin, q_g.reshape(1, LANES), k_g.reshape(1, LANES))


def _hgrn2_intra(q, fz, v, vb, lb, *, reverse):
    return _hgrn2_intra_streams([(q, fz, v, vb, lb, reverse)])[0]


def _hgrn2_intra_streams(streams):
    C = HG_CHUNK
    nb = C // HG_BLOCK
    ns = range(len(streams))
    rev = [st[5] for st in streams]
    r_i = lax.broadcasted_iota(jnp.int32, (C, C), 0)
    c_i = lax.broadcasted_iota(jnp.int32, (C, C), 1)
    tri_f = jnp.where(r_i >= c_i, 1.0, 0.0).astype(BF16)
    tri_b = jnp.where(r_i <= c_i, 1.0, 0.0).astype(BF16)

    kk, lg = [], []
    for q, fz, v, vb, lb, _ in streams:
        f = lb + (1.0 - lb) * jax.nn.sigmoid(fz)
        kk.append(1.0 - f)
        lg.append(jnp.log(f) * LOG2E)

    b = []
    for i in ns:
        hi = lg[i].astype(BF16)
        r1 = lg[i] - hi.astype(F32)
        mid = r1.astype(BF16)
        lo = (r1 - mid.astype(F32)).astype(BF16)
        tri = tri_b if rev[i] else tri_f
        b.append(_dot(tri, hi) + _dot(tri, mid) + _dot(tri, lo))
    b_end = [b[i][0:1, :] if rev[i] else b[i][C - 1:C, :] for i in ns]
    qe = [(streams[i][0] * jnp.exp2(b[i])).astype(BF16) for i in ns]
    u = [_dot_tn(streams[i][3], (kk[i] * jnp.exp2(b_end[i] - b[i])).astype(BF16)) for i in ns]

    pos = lax.broadcasted_iota(jnp.int32, (nb, HG_BLOCK, LANES), 1)
    b3 = [b[i].reshape(nb, HG_BLOCK, LANES) for i in ns]
    q3 = [streams[i][0].reshape(nb, HG_BLOCK, LANES) for i in ns]
    seg_pos = lax.broadcasted_iota(jnp.int32, (C, HG_BLOCK * LANES), 1) // LANES
    row_pos = lax.broadcasted_iota(jnp.int32, (C, HG_BLOCK * LANES), 0) % HG_BLOCK
    kcat = []
    for i in ns:
        kb = kk[i].astype(BF16)
        kcat.append(jnp.where(seg_pos == row_pos, jnp.concatenate([kb] * HG_BLOCK, axis=1),
                              jnp.zeros((), BF16)))
    ps = [[] for i in ns]
    for s in range(HG_BLOCK):
        for i in ns:
            valid = (pos <= s) if rev[i] else (pos >= s)
            d = jnp.where(valid, b3[i] - b3[i][:, s:s + 1, :], -jnp.inf)
            ps[i].append((q3[i] * jnp.exp2(d)).reshape(C, LANES).astype(BF16))
    same_blk = (r_i // HG_BLOCK) == (c_i // HG_BLOCK)
    adiag = [_dot_nt(jnp.concatenate(ps[i], axis=1), kcat[i]) for i in ns]

    zrow = jnp.zeros((1, 1, LANES), F32)
    qt = []
    for i in ns:
        if rev[i]:
            rblk3 = jnp.concatenate([b3[i][1:, 0:1, :], zrow], axis=0)
        else:
            rblk3 = jnp.concatenate([zrow, b3[i][:nb - 1, HG_BLOCK - 1:HG_BLOCK, :]], axis=0)
        qt.append((q3[i] * jnp.exp2(b3[i] - rblk3)).reshape(C, LANES))
    rows = [[] for i in ns]
    for blk in range(nb):
        for i in ns:
            if rev[i]:
                edge = (blk + 1) * HG_BLOCK
                if edge == C:
                    rows[i].append(jnp.zeros((HG_BLOCK, C), F32))
                    continue
                kt = kk[i][edge:, :] * jnp.exp2(b[i][edge:edge + 1, :] - b[i][edge:, :])
                kfull = jnp.concatenate([jnp.zeros((edge, LANES), F32), kt], axis=0)
            else:
                edge = blk * HG_BLOCK
                if edge == 0:
                    rows[i].append(jnp.zeros((HG_BLOCK, C), F32))
                    continue
                kt = kk[i][:edge, :] * jnp.exp2(b[i][edge - 1:edge, :] - b[i][:edge, :])
                kfull = jnp.concatenate([kt, jnp.zeros((C - edge, LANES), F32)], axis=0)
            rows[i].append(_dot_nt(qt[i][blk * HG_BLOCK:(blk + 1) * HG_BLOCK, :].astype(BF16),
                                   kfull.astype(BF16)))
    outs = []
    for i in ns:
        att = jnp.where(same_blk, adiag[i], jnp.concatenate(rows[i], axis=0))
        o = _dot(att.astype(BF16), streams[i][3])
        outs.append((o, qe[i], u[i], jnp.exp2(b_end[i])))
    return outs


def _hgrn2_kernel(q_ref, ff_ref, fb_ref, i_ref, g_ref, lbf_ref, lbb_ref, ng_ref, o_ref, of_ref, ob_ref,
                  qf_ref, qb_ref, uf_ref, ub_ref, ef_ref, eb_ref, sf_ref, sb_ref):
    L = q_ref.shape[1]
    C = HG_CHUNK
    n = L // C
    qscale = HG_DK ** -0.5

    def rows(c):
        return pl.ds(pl.multiple_of(c * C, C), C)

    def intra(it, carry):
        streams = []
        dests = []
        for k in range(HG_INTRA_CHUNKS):
            c = it * HG_INTRA_CHUNKS + k
            r = rows(c)
            q = q_ref[0, r, :] * qscale
            v = i_ref[0, r, :]
            vb = v.astype(BF16)
            streams.append((q, ff_ref[0, r, :], v, vb, lbf_ref[0], False))
            dests.append((of_ref, qf_ref, uf_ref, ef_ref, r, c))
            streams.append((q, fb_ref[0, r, :], v, vb, lbb_ref[0], True))
            dests.append((ob_ref, qb_ref, ub_ref, eb_ref, r, c))
        for (o, qe, u, e), (o_s, q_s, u_s, e_s, r, c) in zip(_hgrn2_intra_streams(streams), dests):
            o_s[r, :] = o
            q_s[r, :] = qe
            u_s[c] = u
            e_s[c] = e
        return carry

    lax.fori_loop(0, n // HG_INTRA_CHUNKS, intra, 0)

    sf_ref[...] = jnp.zeros_like(sf_ref)
    sb_ref[...] = jnp.zeros_like(sb_ref)

    def inter(c, carry):
        for o_s, q_s, u_s, e_s, s_s, cc in ((of_ref, qf_ref, uf_ref, ef_ref, sf_ref, c),
                                            (ob_ref, qb_ref, ub_ref, eb_ref, sb_ref, n - 1 - c)):
            r = rows(cc)
            st = s_s[...]
            o_s[r, :] += _dot_nt(q_s[r, :], st.astype(BF16))
            s_s[...] = st * e_s[cc] + u_s[cc]
        return carry

    lax.fori_loop(0, n, inter, 0, unroll=2)
    o = _rms(of_ref[...] + ob_ref[...], ng_ref[...])
    g = g_ref[0]
    o_ref[0] = (o * (g * jax.nn.sigmoid(g))).astype(o_ref.dtype)


def hgrn2_mixer(proj, lb_f, lb_b, norm_g):
    B, L, _ = proj.shape
    H = HG_HEADS

    def col(k):
        return pl.BlockSpec((1, L, LANES), lambda b, h: (b, 0, k * H + h))

    lbspec = pl.BlockSpec((1, 1, LANES), lambda b, h: (h, 0, 0))
    return pl.pallas_call(
        _hgrn2_kernel,
        grid=(B, H),
        in_specs=[col(0), col(1), col(2), col(3), col(4), lbspec, lbspec,
                  pl.BlockSpec((1, LANES), lambda b, h: (0, 0))],
        out_specs=pl.BlockSpec((1, L, LANES), lambda b, h: (b, 0, h)),
        out_shape=jax.ShapeDtypeStruct((B, L, HG_WIDTH), BF16),
        scratch_shapes=[pltpu.VMEM((L, LANES), F32), pltpu.VMEM((L, LANES), F32),
                        pltpu.VMEM((L, LANES), BF16), pltpu.VMEM((L, LANES), BF16),
                        pltpu.VMEM((L // HG_CHUNK, LANES, LANES), F32),
                        pltpu.VMEM((L // HG_CHUNK, LANES, LANES), F32),
                        pltpu.VMEM((L // HG_CHUNK, 1, LANES), F32), pltpu.VMEM((L // HG_CHUNK, 1, LANES), F32),
                        pltpu.VMEM((LANES, LANES), F32), pltpu.VMEM((LANES, LANES), F32)],
        compiler_params=_params("parallel", "arbitrary"),
        name="hgrn2",
    )(proj, proj, proj, proj, proj, lb_f.reshape(H, 1, LANES), lb_b.reshape(H, 1, LANES),
      norm_g.reshape(1, LANES))


def _hyena_filter_kernel(z_ref, t_ref, w1_ref, b1_ref, w2_ref, b2_ref, w3_ref, b3_ref, w4f_ref, w4b_ref,
                         fr_ref, ad_ref, gs_ref, gd_ref, kn_ref):
    def hdot(a, b):
        return jnp.dot(a, b, precision=HIGHEST, preferred_element_type=F32)

    fr = fr_ref[...]
    h = jnp.sin(fr * (hdot(z_ref[...], w1_ref[...]) + b1_ref[...]))
    h = jnp.sin(fr * (hdot(h, w2_ref[...]) + b2_ref[...]))
    h = jnp.sin(fr * (hdot(h, w3_ref[...]) + b3_ref[...]))
    window = jnp.exp(-t_ref[...] * ad_ref[...])
    hf = hdot(h, w4f_ref[...]) * window
    hb = hdot(h, w4b_ref[...]) * window
    row = lax.broadcasted_iota(jnp.int32, hb.shape, 0)
    hb = jnp.where(row == 0, 0.0, hb)
    gs_ref[...] = hf + hb
    gd_ref[...] = hb - hf
    sgn = jnp.where(row % 2 == 0, 1.0, -1.0)
    kn_ref[...] = jnp.sum((hf + hb) * sgn, axis=0, keepdims=True) * (0.5 / hb.shape[0])


def _hyena_spectrum_kernel(a_ref, b_ref, gs_ref, gd_ref, kre_ref, kim_ref, *, n_fft):
    tf = a_ref.shape[0]
    f0 = pl.program_id(1) * tf
    row = lax.broadcasted_iota(jnp.int32, (tf, 1), 0) + f0
    wf = jnp.where(row == 0, 1.0 / n_fft, 2.0 / n_fft)
    kre_ref[...] = wf * jnp.dot(a_ref[...], gs_ref[...], precision=HIGHEST, preferred_element_type=F32)
    kim_ref[...] = wf * jnp.dot(b_ref[...], gd_ref[...], precision=HIGHEST, preferred_element_type=F32)


def hyena_filter_spectrum(L, cos_m, sin_m, w1, b1, w2, b2, w3, b3, w4, sin_freq):
    C = HY_WIDTH
    t = jnp.linspace(0.0, 1.0, L, dtype=F32)[:, None]
    w = (2.0 * math.pi / L) * jnp.arange(L, dtype=F32)[:, None]
    f = jnp.linspace(1e-4, HY_BANDS - 1, HY_BANDS, dtype=F32)[None, :]
    z = jnp.concatenate([t, jnp.cos(f * w), -jnp.sin(f * w)], axis=-1)
    zp = jnp.pad(z, ((0, 0), (0, LANES - HY_EMB)))
    w1p = jnp.pad(w1, ((0, LANES - HY_EMB), (0, 0)))
    max_decay = math.log(HY_TARGET) / HY_FAST_PCT
    min_decay = math.log(HY_TARGET) / HY_SLOW_PCT
    absd = jnp.abs(jnp.linspace(min_decay, max_decay, C, dtype=F32))[None, :]
    tc = min(512, C)
    full = lambda shape: pl.BlockSpec(shape, lambda j: (0, 0))
    gs, gd, knyq = pl.pallas_call(
        _hyena_filter_kernel,
        grid=(C // tc,),
        in_specs=[full((L, LANES)), full((L, 1)), full((LANES, HY_FILT)), full((1, HY_FILT)),
                  full((HY_FILT, HY_FILT)), full((1, HY_FILT)), full((HY_FILT, HY_FILT)), full((1, HY_FILT)),
                  pl.BlockSpec((HY_FILT, tc), lambda j: (0, j)),
                  pl.BlockSpec((HY_FILT, tc), lambda j: (0, C // tc + j)),
                  full((1, HY_FILT)),
                  pl.BlockSpec((1, tc), lambda j: (0, j))],
        out_specs=[pl.BlockSpec((L, tc), lambda j: (0, j)), pl.BlockSpec((L, tc), lambda j: (0, j)),
                   pl.BlockSpec((1, tc), lambda j: (0, j))],
        out_shape=[jax.ShapeDtypeStruct((L, C), F32), jax.ShapeDtypeStruct((L, C), F32),
                   jax.ShapeDtypeStruct((1, C), F32)],
        compiler_params=_params("arbitrary"),
        name="hyena_filter",
    )(zp, t, w1p, b1.reshape(1, -1), w2, b2.reshape(1, -1), w3, b3.reshape(1, -1), w4, w4,
      sin_freq.reshape(1, -1), absd)

    tf = min(256, L)
    kre, kim = pl.pallas_call(
        functools.partial(_hyena_spectrum_kernel, n_fft=2 * L),
        grid=(C // tc, L // tf),
        in_specs=[pl.BlockSpec((tf, L), lambda j, i: (i, 0)),
                  pl.BlockSpec((tf, L), lambda j, i: (i, 0)),
                  pl.BlockSpec((L, tc), lambda j, i: (0, j)),
                  pl.BlockSpec((L, tc), lambda j, i: (0, j))],
        out_specs=[pl.BlockSpec((tf, tc), lambda j, i: (i, j)), pl.BlockSpec((tf, tc), lambda j, i: (i, j))],
        out_shape=[jax.ShapeDtypeStruct((L, C), F32), jax.ShapeDtypeStruct((L, C), F32)],
        compiler_params=_params("parallel", "arbitrary"),
        name="hyena_spectrum",
    )(cos_m, sin_m, gs, gd)
    return kre, kim, knyq


def _short_conv(u, w_ref, b_ref):
    L = u.shape[0]
    row = lax.broadcasted_iota(jnp.int32, (L, 1), 0)
    prev = jnp.where(row == 0, 0.0, pltpu.roll(u, 1, axis=0))
    nxt = jnp.where(row == L - 1, 0.0, pltpu.roll(u, L - 1, axis=0))
    return prev * w_ref[0:1, :] + u * w_ref[1:2, :] + nxt * w_ref[2:3, :] + b_ref[...]


def _hyena_conv_kernel(x1_ref, x2_ref, v_ref, cw1_ref, cw2_ref, cw3_ref, cb1_ref, cb2_ref, cb3_ref,
                       a_ref, b_ref, kre_ref, kim_ref, knyq_ref, bias_ref, o_ref, y_ref, *, tf):
    L = v_ref.shape[1]
    x2 = _short_conv(x2_ref[0], cw2_ref, cb2_ref)
    vv = _short_conv(v_ref[0], cw3_ref, cb3_ref)
    vx = vv * x2
    vxb = vx.astype(BF16)
    sgn = jnp.where(lax.broadcasted_iota(jnp.int32, (L, 1), 0) % 2 == 0, 1.0, -1.0)
    v_nyq = jnp.sum(vx * sgn, axis=0, keepdims=True)
    y_ref[...] = vx * bias_ref[...] + sgn * (v_nyq * knyq_ref[...])
    for fb in range(L // tf):
        fs = slice(fb * tf, (fb + 1) * tf)
        p = _dot(a_ref[fs, :], vxb)
        q = _dot(b_ref[fs, :], vxb)
        kre = kre_ref[fs, :]
        kim = kim_ref[fs, :]
        r1 = (p * kre + q * kim).astype(BF16)
        r2 = (q * kre - p * kim).astype(BF16)
        y_ref[...] += _dot(a_ref[:, fs], r1) + _dot(b_ref[:, fs], r2)
    x1 = _short_conv(x1_ref[0], cw1_ref, cb1_ref)
    o_ref[0] = (y_ref[...] * x1).astype(o_ref.dtype)


def hyena_conv(proj, conv_w, conv_b, cos_b, sin_b, kre, kim, knyq, bias, *, tc):
    B, L, _ = proj.shape
    C = HY_WIDTH
    tc = min(tc, C)
    nc = C // tc
    tf = min(512, L)

    def grp(k):
        return pl.BlockSpec((1, L, tc), lambda c, b: (b, 0, k * nc + c))

    def cw(k):
        return pl.BlockSpec((HY_SHORT, tc), lambda c, b: (0, k * nc + c))

    def cb(k):
        return pl.BlockSpec((1, tc), lambda c, b: (0, k * nc + c))

    tbl = pl.BlockSpec((L, L), lambda c, b: (0, 0), pipeline_mode=pl.Buffered(1))
    chan = pl.BlockSpec((L, tc), lambda c, b: (0, c), pipeline_mode=pl.Buffered(1))
    vec = pl.BlockSpec((1, tc), lambda c, b: (0, c))
    return pl.pallas_call(
        functools.partial(_hyena_conv_kernel, tf=tf),
        grid=(nc, B),
        in_specs=[grp(0), grp(1), grp(2), cw(0), cw(1), cw(2), cb(0), cb(1), cb(2),
                  tbl, tbl, chan, chan, vec, vec],
        out_specs=pl.BlockSpec((1, L, tc), lambda c, b: (b, 0, c)),
        out_shape=jax.ShapeDtypeStruct((B, L, C), BF16),
        scratch_shapes=[pltpu.VMEM((L, tc), F32)],
        compiler_params=_params("parallel", "arbitrary"),
        name="hyena_conv",
    )(proj, proj, proj, conv_w, conv_w, conv_w, conv_b.reshape(1, -1), conv_b.reshape(1, -1),
      conv_b.reshape(1, -1), cos_b, sin_b, kre, kim, knyq, bias.reshape(1, -1))


def _dft_tables(L):
    n = 2 * L
    idx = jnp.arange(L, dtype=jnp.int32)
    m = (idx[:, None] * idx[None, :]) % n
    ang = m.astype(F32) * (2.0 * math.pi / n)
    return jnp.cos(ang), jnp.sin(ang)


def _axial_rope_tables(L):
    rows = L // GRID_W
    r, c = jnp.meshgrid(jnp.arange(rows, dtype=F32), jnp.arange(GRID_W, dtype=F32), indexing='ij')
    r = r.reshape(-1)
    c = c.reshape(-1)
    half = GQ_HEAD_DIM // 2
    inv = ROPE_THETA ** (-jnp.arange(0, half, 2, dtype=F32) / half)
    ang_r = r[:, None] * inv[None, :]
    ang_c = c[:, None] * inv[None, :]
    ang = jnp.concatenate([ang_r, ang_r, ang_c, ang_c], axis=-1)
    return jnp.cos(ang), jnp.sin(ang)


def _even_mixer(x, layer, norm_g, w_in, w_out, conv_w, conv_b, fw1, fb1, fw2, fb2, fw3, fb3, fw4, sin_freq,
                hy_bias, q_g, k_g, lam_q1, lam_k1, lam_q2, lam_k2, subln_g):
    B, L, D = x.shape
    x2 = x.reshape(B * L, D)
    proj = norm_matmul(x2, norm_g, w_in.astype(BF16), tm=1024, tn=512).reshape(B, L, -1)
    cos_m, sin_m = _dft_tables(L)
    kre, kim, knyq = hyena_filter_spectrum(L, cos_m, sin_m, fw1, fb1, fw2, fb2, fw3, fb3, fw4, sin_freq)
    o_a = hyena_conv(proj, conv_w, conv_b, cos_m.astype(BF16), sin_m.astype(BF16), kre, kim, knyq, hy_bias,
                     tc=256)
    lam_init = 0.8 - 0.6 * math.exp(-0.3 * layer)
    lam = (jnp.exp(jnp.sum(lam_q1.astype(F32) * lam_k1.astype(F32)))
           - jnp.exp(jnp.sum(lam_q2.astype(F32) * lam_k2.astype(F32))) + lam_init)
    slopes = jnp.asarray(np.array([2.0 ** (-8.0 * (h + 1) / DF_HEADS) for h in range(DF_HEADS)],
                                  dtype=np.float32))
    scal = jnp.concatenate([lam.reshape(1), slopes]).astype(F32)
    nb_hy = 3 * HY_WIDTH // LANES
    nb_qk = DF_QK_WIDTH // LANES
    o_b = diff_attention(proj, scal, q_g, k_g, subln_g, q_blk0=nb_hy, k_blk0=nb_hy + nb_qk,
                         v_blk0=nb_hy + 2 * nb_qk, out_scale=1.0 - lam_init, tq=256)
    out = proj2_residual(o_a.reshape(B * L, -1), o_b.reshape(B * L, -1), w_out.astype(BF16), x2,
                         tm=1024, tn=1024)
    return out.reshape(B, L, D)


def _odd_mixer(x, norm_g, lb_f, lb_b, w_in, w_out, hg_norm_g, q_g, k_g):
    B, L, D = x.shape
    x2 = x.reshape(B * L, D)
    proj = norm_matmul(x2, norm_g, w_in.astype(BF16), tm=1024, tn=512).reshape(B, L, -1)
    o_c = hgrn2_mixer(proj, lb_f, lb_b, hg_norm_g)
    cos, sin = _axial_rope_tables(L)
    gw = GQ_GROUP * GQ_HEAD_DIM
    q0 = 5 * HG_WIDTH
    k0 = q0 + GQ_WIDTH
    v0 = k0 + GQ_KV_HEADS * GQ_HEAD_DIM
    o_d = gqa_attention(proj, cos, sin, q_g, k_g, q_blk0=q0 // gw, k_blk0=k0 // LANES, v_blk0=v0 // LANES,
                        tq=256)
    out = proj2_residual(o_c.reshape(B * L, -1), o_d.reshape(B * L, -1), w_out.astype(BF16), x2,
                         tm=1024, tn=1024)
    return out.reshape(B, L, D)


def kernel(x, mem, norm_mix_g, norm_mem_q_g, norm_mem_kv_g, norm_ffn_g, ev_w_in, ev_w_out, hy_conv_w, hy_conv_b, hy_fw1, hy_fb1, hy_fw2, hy_fb2, hy_fw3, hy_fb3, hy_fw4, hy_sin_freq, hy_bias, df_q_g, df_k_g, df_lam_q1, df_lam_k1, df_lam_q2, df_lam_k2, df_subln_g, od_w_in, od_w_out, hg_lb_logits, hg_norm_g, gq_q_g, gq_k_g, ca_w_q, ca_w_kv, ca_w_out, ca_q_g, ca_k_g, mlp_w1, mlp_w2):
    B, L, D = x.shape
    M = mem.shape[1]
    depth = norm_mix_g.shape[0]
    lb_soft = jax.nn.softmax(hg_lb_logits.astype(F32), axis=1)
    lbs = jnp.cumsum(lb_soft, axis=1)
    lbs = lbs - lbs[:, :1]
    mem2 = mem.reshape(B * M, D)
    for layer in range(depth):
        j = layer // 2
        if layer % 2 == 0:
            x = _even_mixer(x, layer, norm_mix_g[layer], ev_w_in[j], ev_w_out[j], hy_conv_w[j], hy_conv_b[j],
                            hy_fw1[j], hy_fb1[j], hy_fw2[j], hy_fb2[j], hy_fw3[j], hy_fb3[j], hy_fw4[j],
                            hy_sin_freq[j], hy_bias[j], df_q_g[j], df_k_g[j], df_lam_q1[j], df_lam_k1[j],
                            df_lam_q2[j], df_lam_k2[j], df_subln_g[j])
        else:
            x = _odd_mixer(x, norm_mix_g[layer], lbs[0, layer], lbs[1, layer], od_w_in[j], od_w_out[j],
                           hg_norm_g[j], gq_q_g[j], gq_k_g[j])
        kv = norm_matmul(mem2, norm_mem_kv_g[layer], ca_w_kv[layer].astype(BF16), tm=1024, tn=512)
        x = cross_attention(x, kv.reshape(B, M, -1), norm_mem_q_g[layer], ca_w_q[layer].astype(BF16),
                            ca_q_g[layer], ca_k_g[layer], ca_w_out[layer].astype(BF16), tq=512)
        x = mlp_residual(x.reshape(B * L, D), norm_ffn_g[layer], mlp_w1[layer].astype(BF16),
                         mlp_w2[layer].astype(BF16), tm=512, th=1024).reshape(B, L, D)
    return x
```

```python
import functools
import math

import numpy as np
import jax
import jax.numpy as jnp
from jax import lax
from jax.experimental import pallas as pl
from jax.experimental.pallas import tpu as pltpu

F32 = jnp.float32
BF16 = jnp.bfloat16
HIGHEST = lax.Precision.HIGHEST

D_MODEL = 2048
NORM_EPS = 1e-6
GRID_W = 64
HY_WIDTH = D_MODEL // 2
HY_EMB = 33
HY_BANDS = (HY_EMB - 1) // 2
HY_FILT = 64
HY_SHORT = 3
HY_TARGET = 1e-2
HY_FAST_PCT = 0.3
HY_SLOW_PCT = 1.5
DF_HEADS = 8
DF_HEAD_DIM = 64
DF_V_DIM = 128
DF_QK_WIDTH = DF_HEADS * 2 * DF_HEAD_DIM
HG_WIDTH = D_MODEL // 2
HG_HEADS = 8
HG_DK = HG_WIDTH // HG_HEADS
GQ_WIDTH = D_MODEL - HG_WIDTH
GQ_HEADS = 8
GQ_KV_HEADS = 2
GQ_HEAD_DIM = GQ_WIDTH // GQ_HEADS
GQ_GROUP = GQ_HEADS // GQ_KV_HEADS
ROPE_THETA = 10000.0
CA_HEADS = 4
CA_HEAD_DIM = 128
CA_WIDTH = CA_HEADS * CA_HEAD_DIM

LANES = 128
SUBLANES = 8
VMEM_LIMIT_BYTES = 56 * 1024 * 1024

HG_CHUNK = 64
HG_BLOCK = SUBLANES
ATTN_STRIP = 256
HG_INTRA_CHUNKS = 4
LOG2E = 1.4426950408889634


def _params(*sem):
    return pltpu.CompilerParams(dimension_semantics=sem, vmem_limit_bytes=VMEM_LIMIT_BYTES)


def _rms(x, g):
    return x * lax.rsqrt(jnp.mean(x * x, axis=-1, keepdims=True) + NORM_EPS) * g


def _dot(a, b):
    return jnp.dot(a, b, preferred_element_type=F32)


def _dot_nt(a, b):
    return lax.dot_general(a, b, (((1,), (1,)), ((), ())), preferred_element_type=F32)


def _dot_tn(a, b):
    return lax.dot_general(a, b, (((0,), (0,)), ((), ())), preferred_element_type=F32)


def _pipeline3(n, stage_a, stage_b, stage_c):
    a = {}
    b = {}
    for t in range(n + 2):
        if t < n:
            a[t] = stage_a(t)
        if 0 <= t - 1 < n:
            b[t - 1] = stage_b(t - 1, a.pop(t - 1))
        if 0 <= t - 2 < n:
            stage_c(t - 2, b.pop(t - 2))


def _norm_matmul_kernel(x_ref, g_ref, w_ref, o_ref, hn_ref):
    @pl.when(pl.program_id(1) == 0)
    def _():
        hn_ref[...] = _rms(x_ref[...], g_ref[...]).astype(BF16)

    o_ref[...] = _dot(hn_ref[...], w_ref[...]).astype(o_ref.dtype)


def norm_matmul(x, g, w, *, tm, tn, out_dtype=F32):
    M, D = x.shape
    N = w.shape[1]
    tm = min(tm, M)
    tn = min(tn, N)
    return pl.pallas_call(
        _norm_matmul_kernel,
        grid=(M // tm, N // tn),
        in_specs=[pl.BlockSpec((tm, D), lambda i, j: (i, 0)),
                  pl.BlockSpec((1, D), lambda i, j: (0, 0)),
                  pl.BlockSpec((D, tn), lambda i, j: (0, j))],
        out_specs=pl.BlockSpec((tm, tn), lambda i, j: (i, j)),
        out_shape=jax.ShapeDtypeStruct((M, N), out_dtype),
        scratch_shapes=[pltpu.VMEM((tm, D), BF16)],
        compiler_params=_params("parallel", "arbitrary"),
        name="norm_matmul",
    )(x, g.reshape(1, D), w)


def _proj2_res_kernel(a1_ref, a2_ref, w1_ref, w2_ref, r_ref, o_ref):
    acc = _dot(a1_ref[...], w1_ref[...]) + _dot(a2_ref[...], w2_ref[...])
    o_ref[...] = r_ref[...] + acc


def proj2_residual(a1, a2, w, res, *, tm, tn):
    M, K1 = a1.shape
    K2 = a2.shape[1]
    assert K1 == K2
    N = w.shape[1]
    tm = min(tm, M)
    tn = min(tn, N)
    return pl.pallas_call(
        _proj2_res_kernel,
        grid=(M // tm, N // tn),
        in_specs=[pl.BlockSpec((tm, K1), lambda i, j: (i, 0)),
                  pl.BlockSpec((tm, K2), lambda i, j: (i, 0)),
                  pl.BlockSpec((K1, tn), lambda i, j: (0, j)),
                  pl.BlockSpec((K2, tn), lambda i, j: (1, j)),
                  pl.BlockSpec((tm, tn), lambda i, j: (i, j))],
        out_specs=pl.BlockSpec((tm, tn), lambda i, j: (i, j)),
        out_shape=jax.ShapeDtypeStruct((M, N), F32),
        compiler_params=_params("parallel", "arbitrary"),
        name="proj2_residual",
    )(a1, a2, w, w, res)


def _mlp_kernel(x_ref, g_ref, w1_ref, w2_ref, o_ref, hn_ref, acc_ref):
    j = pl.program_id(1)

    @pl.when(j == 0)
    def _():
        hn_ref[...] = _rms(x_ref[...], g_ref[...]).astype(BF16)
        acc_ref[...] = jnp.zeros_like(acc_ref)

    h1 = jnp.maximum(_dot(hn_ref[...], w1_ref[...]), 0.0)
    acc_ref[...] += _dot((h1 * h1).astype(BF16), w2_ref[...])

    @pl.when(j == pl.num_programs(1) - 1)
    def _():
        o_ref[...] = x_ref[...] + acc_ref[...]


def mlp_residual(x, g, w1, w2, *, tm, th):
    M, D = x.shape
    H = w1.shape[1]
    tm = min(tm, M)
    th = min(th, H)
    return pl.pallas_call(
        _mlp_kernel,
        grid=(M // tm, H // th),
        in_specs=[pl.BlockSpec((tm, D), lambda i, j: (i, 0)),
                  pl.BlockSpec((1, D), lambda i, j: (0, 0)),
                  pl.BlockSpec((D, th), lambda i, j: (0, j)),
                  pl.BlockSpec((th, D), lambda i, j: (j, 0))],
        out_specs=pl.BlockSpec((tm, D), lambda i, j: (i, 0)),
        out_shape=jax.ShapeDtypeStruct((M, D), F32),
        scratch_shapes=[pltpu.VMEM((tm, D), BF16), pltpu.VMEM((tm, D), F32)],
        compiler_params=_params("parallel", "arbitrary"),
        name="mlp_residual",
    )(x, g.reshape(1, D), w1, w2)


def _cross_attn_kernel(x_ref, g_ref, wq_ref, kv_ref, qg_ref, kg_ref, wo_ref, o_ref):
    x = x_ref[0]
    hn = _rms(x, g_ref[...]).astype(BF16)
    q = _dot(hn, wq_ref[...])
    kv = kv_ref[0]
    scale = CA_HEAD_DIM ** -0.5
    outs = []
    for h in range(CA_HEADS):
        sl = slice(h * CA_HEAD_DIM, (h + 1) * CA_HEAD_DIM)
        qh = (_rms(q[:, sl], qg_ref[...]) * scale).astype(BF16)
        kh = _rms(kv[:, sl], kg_ref[...]).astype(BF16)
        vh = kv[:, CA_WIDTH + h * CA_HEAD_DIM:CA_WIDTH + (h + 1) * CA_HEAD_DIM].astype(BF16)
        s = _dot_nt(qh, kh)
        e = jnp.exp(s - jnp.max(s, axis=-1, keepdims=True))
        p = e / jnp.sum(e, axis=-1, keepdims=True)
        outs.append(_dot(p.astype(BF16), vh))
    o = jnp.concatenate(outs, axis=-1).astype(BF16)
    o_ref[0] = x + _dot(o, wo_ref[...])


def cross_attention(x, kv, g, wq, qg, kg, wo, *, tq):
    B, L, D = x.shape
    M = kv.shape[1]
    tq = min(tq, L)
    return pl.pallas_call(
        _cross_attn_kernel,
        grid=(B, L // tq),
        in_specs=[pl.BlockSpec((1, tq, D), lambda b, i: (b, i, 0)),
                  pl.BlockSpec((1, D), lambda b, i: (0, 0)),
                  pl.BlockSpec((D, CA_WIDTH), lambda b, i: (0, 0)),
                  pl.BlockSpec((1, M, 2 * CA_WIDTH), lambda b, i: (b, 0, 0)),
                  pl.BlockSpec((1, CA_HEAD_DIM), lambda b, i: (0, 0)),
                  pl.BlockSpec((1, CA_HEAD_DIM), lambda b, i: (0, 0)),
                  pl.BlockSpec((CA_WIDTH, D), lambda b, i: (0, 0))],
        out_specs=pl.BlockSpec((1, tq, D), lambda b, i: (b, i, 0)),
        out_shape=jax.ShapeDtypeStruct((B, L, D), F32),
        compiler_params=_params("parallel", "arbitrary"),
        name="cross_attention",
    )(x, g.reshape(1, D), wq, kv, qg.reshape(1, -1), kg.reshape(1, -1), wo)


def _half_rms(x, g2, lo_mask):
    sq = x * x
    ms_lo = jnp.sum(jnp.where(lo_mask, sq, 0.0), axis=-1, keepdims=True)
    ms_hi = jnp.sum(jnp.where(lo_mask, 0.0, sq), axis=-1, keepdims=True)
    inv = jnp.where(lo_mask, lax.rsqrt(ms_lo * (1.0 / DF_HEAD_DIM) + NORM_EPS),
                    lax.rsqrt(ms_hi * (1.0 / DF_HEAD_DIM) + NORM_EPS))
    return x * inv * g2


def _diff_attn_kernel(sc_ref, q_ref, k_ref, v_ref, qg_ref, kg_ref, sg_ref, o_ref, kn_ref, vb_ref, *,
                      out_scale):
    h = pl.program_id(1)
    qi = pl.program_id(2)
    tq = q_ref.shape[1]
    L = k_ref.shape[1]
    lane = lax.broadcasted_iota(jnp.int32, (1, LANES), 1)
    lo_mask = lane < DF_HEAD_DIM

    @pl.when(qi == 0)
    def _():
        kn_ref[...] = _half_rms(k_ref[0], kg_ref[...], lo_mask).astype(BF16)
        vb_ref[:, :LANES] = v_ref[0].astype(BF16)
        vb_ref[:, LANES:] = jnp.ones((L, LANES), BF16)

    lam = sc_ref[0]
    slope = sc_ref[1 + h] * LOG2E
    ts = min(ATTN_STRIP, tq)
    col = lax.broadcasted_iota(jnp.int32, (1, L), 1).astype(F32)
    pending = {}

    def scores(i):
        r0 = (i // 2) * ts
        qn = _half_rms(q_ref[0, r0:r0 + ts, :], qg_ref[...], lo_mask) * (DF_HEAD_DIM ** -0.5 * LOG2E)
        qq = jnp.where(lo_mask if i % 2 == 0 else ~lo_mask, qn, 0.0).astype(BF16)
        row = (lax.broadcasted_iota(jnp.int32, (ts, 1), 0) + (qi * tq + r0)).astype(F32)
        return _dot_nt(qq, kn_ref[...]) - jnp.abs(row - col) * slope

    def exps(i, s):
        return jnp.exp2(s - jnp.max(s, axis=-1, keepdims=True)).astype(BF16)

    def values(i, e):
        r = _dot(e, vb_ref[...])
        if i % 2 == 0:
            pending[i // 2] = r
            return
        r0 = (i // 2) * ts
        r1 = pending.pop(i // 2)
        o = (r1[:, :LANES] * (1.0 / r1[:, LANES:LANES + 1])
             - r[:, :LANES] * (lam / r[:, LANES:LANES + 1]))
        o_ref[0, r0:r0 + ts, :] = (_rms(o, sg_ref[...]) * out_scale).astype(o_ref.dtype)

    _pipeline3(2 * (tq // ts), scores, exps, values)


def diff_attention(proj, scal, q_g, k_g, subln_g, *, q_blk0, k_blk0, v_blk0, out_scale, tq):
    B, L, _ = proj.shape
    tq = min(tq, L)
    qg2 = jnp.concatenate([q_g, q_g]).reshape(1, LANES)
    kg2 = jnp.concatenate([k_g, k_g]).reshape(1, LANES)
    return pl.pallas_call(
        functools.partial(_diff_attn_kernel, out_scale=out_scale),
        grid=(B, DF_HEADS, L // tq),
        in_specs=[pl.BlockSpec(memory_space=pltpu.SMEM),
                  pl.BlockSpec((1, tq, LANES), lambda b, h, i: (b, i, q_blk0 + h)),
                  pl.BlockSpec((1, L, LANES), lambda b, h, i: (b, 0, k_blk0 + h)),
                  pl.BlockSpec((1, L, LANES), lambda b, h, i: (b, 0, v_blk0 + h)),
                  pl.BlockSpec((1, LANES), lambda b, h, i: (0, 0)),
                  pl.BlockSpec((1, LANES), lambda b, h, i: (0, 0)),
                  pl.BlockSpec((1, LANES), lambda b, h, i: (0, 0))],
        out_specs=pl.BlockSpec((1, tq, LANES), lambda b, h, i: (b, i, h)),
        out_shape=jax.ShapeDtypeStruct((B, L, DF_HEADS * DF_V_DIM), BF16),
        scratch_shapes=[pltpu.VMEM((L, LANES), BF16), pltpu.VMEM((L, 2 * LANES), BF16)],
        compiler_params=_params("parallel", "parallel", "arbitrary"),
        name="diff_attention",
    )(scal, proj, proj, proj, qg2, kg2, subln_g.reshape(1, LANES))


def _rope(x, cos, sin, first_mask):
    q = GQ_HEAD_DIM // 4
    rot = jnp.where(first_mask, -pltpu.roll(x, LANES - q, axis=1), pltpu.roll(x, q, axis=1))
    return x * cos + rot * sin


def _gqa_kernel(q_ref, k_ref, v_ref, cq_ref, sq_ref, ck_ref, sk_ref, qg_ref, kg_ref, o_ref, kn_ref, vb_ref):
    qi = pl.program_id(2)
    lane = lax.broadcasted_iota(jnp.int32, (1, LANES), 1)
    first_mask = (lane % (GQ_HEAD_DIM // 2)) < (GQ_HEAD_DIM // 4)

    @pl.when(qi == 0)
    def _():
        kn = _rms(k_ref[0], kg_ref[...])
        kn_ref[...] = _rope(kn, ck_ref[...], sk_ref[...], first_mask).astype(BF16)
        vb_ref[:, :LANES] = v_ref[0].astype(BF16)
        vb_ref[:, LANES:] = jnp.ones((k_ref.shape[1], LANES), BF16)

    scale = GQ_HEAD_DIM ** -0.5 * LOG2E
    tq = q_ref.shape[1]
    ts = min(ATTN_STRIP, tq)
    nr = tq // ts

    def where(i):
        r0 = (i % nr) * ts
        return slice(r0, r0 + ts), slice((i // nr) * GQ_HEAD_DIM, (i // nr + 1) * GQ_HEAD_DIM)

    def scores(i):
        rs, cs = where(i)
        qn = _rope(_rms(q_ref[0, rs, cs], qg_ref[...]), cq_ref[rs, :], sq_ref[rs, :], first_mask) * scale
        return _dot_nt(qn.astype(BF16), kn_ref[...])

    def exps(i, s):
        return jnp.exp2(s - jnp.max(s, axis=-1, keepdims=True)).astype(BF16)

    def values(i, e):
        rs, cs = where(i)
        res = _dot(e, vb_ref[...])
        o_ref[0, rs, cs] = (res[:, :LANES] * (1.0 / res[:, LANES:LANES + 1])).astype(o_ref.dtype)

    _pipeline3(GQ_GROUP * nr, scores, exps, values)


def gqa_attention(proj, cos, sin, q_g, k_g, *, q_blk0, k_blk0, v_blk0, tq):
    B, L, _ = proj.shape
    tq = min(tq, L)
    gw = GQ_GROUP * GQ_HEAD_DIM
    return pl.pallas_call(
        _gqa_kernel,
        grid=(B, GQ_KV_HEADS, L // tq),
        in_specs=[pl.BlockSpec((1, tq, gw), lambda b, g, i: (b, i, q_blk0 + g)),
                  pl.BlockSpec((1, L, LANES), lambda b, g, i: (b, 0, k_blk0 + g)),
                  pl.BlockSpec((1, L, LANES), lambda b, g, i: (b, 0, v_blk0 + g)),
                  pl.BlockSpec((tq, LANES), lambda b, g, i: (i, 0)),
                  pl.BlockSpec((tq, LANES), lambda b, g, i: (i, 0)),
                  pl.BlockSpec((L, LANES), lambda b, g, i: (0, 0)),
                  pl.BlockSpec((L, LANES), lambda b, g, i: (0, 0)),
                  pl.BlockSpec((1, LANES), lambda b, g, i: (0, 0)),
                  pl.BlockSpec((1, LANES), lambda b, g, i: (0, 0))],
        out_specs=pl.BlockSpec((1, tq, gw), lambda b, g, i: (b, i, g)),
        out_shape=jax.ShapeDtypeStruct((B, L, GQ_WIDTH), BF16),
        scratch_shapes=[pltpu.VMEM((L, LANES), BF16), pltpu.VMEM((L, 2 * LANES), BF16)],
        compiler_params=_params("parallel", "parallel", "arbitrary"),
        name="gqa_attention",
    )(proj, proj, proj, cos, sin, cos, sin, q_g.reshape(1, LANES), k_g.reshape(1, LANES))


def _hgrn2_intra(q, fz, v, vb, lb, *, reverse):
    return _hgrn2_intra_streams([(q, fz, v, vb, lb, reverse)])[0]


def _hgrn2_intra_streams(streams):
    C = HG_CHUNK
    nb = C // HG_BLOCK
    ns = range(len(streams))
    rev = [st[5] for st in streams]
    r_i = lax.broadcasted_iota(jnp.int32, (C, C), 0)
    c_i = lax.broadcasted_iota(jnp.int32, (C, C), 1)
    tri_f = jnp.where(r_i >= c_i, 1.0, 0.0).astype(BF16)
    tri_b = jnp.where(r_i <= c_i, 1.0, 0.0).astype(BF16)

    kk, lg = [], []
    for q, fz, v, vb, lb, _ in streams:
        f = lb + (1.0 - lb) * jax.nn.sigmoid(fz)
        kk.append(1.0 - f)
        lg.append(jnp.log(f) * LOG2E)

    b = []
    for i in ns:
        hi = lg[i].astype(BF16)
        r1 = lg[i] - hi.astype(F32)
        mid = r1.astype(BF16)
        lo = (r1 - mid.astype(F32)).astype(BF16)
        tri = tri_b if rev[i] else tri_f
        b.append(_dot(tri, hi) + _dot(tri, mid) + _dot(tri, lo))
    b_end = [b[i][0:1, :] if rev[i] else b[i][C - 1:C, :] for i in ns]
    qe = [(streams[i][0] * jnp.exp2(b[i])).astype(BF16) for i in ns]
    u = [_dot_tn(streams[i][3], (kk[i] * jnp.exp2(b_end[i] - b[i])).astype(BF16)) for i in ns]

    pos = lax.broadcasted_iota(jnp.int32, (nb, HG_BLOCK, LANES), 1)
    b3 = [b[i].reshape(nb, HG_BLOCK, LANES) for i in ns]
    q3 = [streams[i][0].reshape(nb, HG_BLOCK, LANES) for i in ns]
    seg_pos = lax.broadcasted_iota(jnp.int32, (C, HG_BLOCK * LANES), 1) // LANES
    row_pos = lax.broadcasted_iota(jnp.int32, (C, HG_BLOCK * LANES), 0) % HG_BLOCK
    kcat = []
    for i in ns:
        kb = kk[i].astype(BF16)
        kcat.append(jnp.where(seg_pos == row_pos, jnp.concatenate([kb] * HG_BLOCK, axis=1),
                              jnp.zeros((), BF16)))
    ps = [[] for i in ns]
    for s in range(HG_BLOCK):
        for i in ns:
            valid = (pos <= s) if rev[i] else (pos >= s)
            d = jnp.where(valid, b3[i] - b3[i][:, s:s + 1, :], -jnp.inf)
            ps[i].append((q3[i] * jnp.exp2(d)).reshape(C, LANES).astype(BF16))
    same_blk = (r_i // HG_BLOCK) == (c_i // HG_BLOCK)
    adiag = [_dot_nt(jnp.concatenate(ps[i], axis=1), kcat[i]) for i in ns]

    zrow = jnp.zeros((1, 1, LANES), F32)
    qt = []
    for i in ns:
        if rev[i]:
            rblk3 = jnp.concatenate([b3[i][1:, 0:1, :], zrow], axis=0)
        else:
            rblk3 = jnp.concatenate([zrow, b3[i][:nb - 1, HG_BLOCK - 1:HG_BLOCK, :]], axis=0)
        qt.append((q3[i] * jnp.exp2(b3[i] - rblk3)).reshape(C, LANES))
    rows = [[] for i in ns]
    for blk in range(nb):
        for i in ns:
            if rev[i]:
                edge = (blk + 1) * HG_BLOCK
                if edge == C:
                    rows[i].append(jnp.zeros((HG_BLOCK, C), F32))
                    continue
                kt = kk[i][edge:, :] * jnp.exp2(b[i][edge:edge + 1, :] - b[i][edge:, :])
                kfull = jnp.concatenate([jnp.zeros((edge, LANES), F32), kt], axis=0)
            else:
                edge = blk * HG_BLOCK
                if edge == 0:
                    rows[i].append(jnp.zeros((HG_BLOCK, C), F32))
                    continue
                kt = kk[i][:edge, :] * jnp.exp2(b[i][edge - 1:edge, :] - b[i][:edge, :])
                kfull = jnp.concatenate([kt, jnp.zeros((C - edge, LANES), F32)], axis=0)
            rows[i].append(_dot_nt(qt[i][blk * HG_BLOCK:(blk + 1) * HG_BLOCK, :].astype(BF16),
                                   kfull.astype(BF16)))
    outs = []
    for i in ns:
        att = jnp.where(same_blk, adiag[i], jnp.concatenate(rows[i], axis=0))
        o = _dot(att.astype(BF16), streams[i][3])
        outs.append((o, qe[i], u[i], jnp.exp2(b_end[i])))
    return outs


def _hgrn2_kernel(q_ref, ff_ref, fb_ref, i_ref, g_ref, lbf_ref, lbb_ref, ng_ref, o_ref, of_ref, ob_ref,
                  qf_ref, qb_ref, uf_ref, ub_ref, ef_ref, eb_ref, sf_ref, sb_ref):
    L = q_ref.shape[1]
    C = HG_CHUNK
    n = L // C
    qscale = HG_DK ** -0.5

    def rows(c):
        return pl.ds(pl.multiple_of(c * C, C), C)

    def intra(it, carry):
        streams = []
        dests = []
        for k in range(HG_INTRA_CHUNKS):
            c = it * HG_INTRA_CHUNKS + k
            r = rows(c)
            q = q_ref[0, r, :] * qscale
            v = i_ref[0, r, :]
            vb = v.astype(BF16)
            streams.append((q, ff_ref[0, r, :], v, vb, lbf_ref[0], False))
            dests.append((of_ref, qf_ref, uf_ref, ef_ref, r, c))
            streams.append((q, fb_ref[0, r, :], v, vb, lbb_ref[0], True))
            dests.append((ob_ref, qb_ref, ub_ref, eb_ref, r, c))
        for (o, qe, u, e), (o_s, q_s, u_s, e_s, r, c) in zip(_hgrn2_intra_streams(streams), dests):
            o_s[r, :] = o
            q_s[r, :] = qe
            u_s[c] = u
            e_s[c] = e
        return carry

    lax.fori_loop(0, n // HG_INTRA_CHUNKS, intra, 0)

    sf_ref[...] = jnp.zeros_like(sf_ref)
    sb_ref[...] = jnp.zeros_like(sb_ref)

    def inter(c, carry):
        for o_s, q_s, u_s, e_s, s_s, cc in ((of_ref, qf_ref, uf_ref, ef_ref, sf_ref, c),
                                            (ob_ref, qb_ref, ub_ref, eb_ref, sb_ref, n - 1 - c)):
            r = rows(cc)
            st = s_s[...]
            o_s[r, :] += _dot_nt(q_s[r, :], st.astype(BF16))
            s_s[...] = st * e_s[cc] + u_s[cc]
        return carry

    lax.fori_loop(0, n, inter, 0, unroll=2)
    o = _rms(of_ref[...] + ob_ref[...], ng_ref[...])
    g = g_ref[0]
    o_ref[0] = (o * (g * jax.nn.sigmoid(g))).astype(o_ref.dtype)


def hgrn2_mixer(proj, lb_f, lb_b, norm_g):
    B, L, _ = proj.shape
    H = HG_HEADS

    def col(k):
        return pl.BlockSpec((1, L, LANES), lambda b, h: (b, 0, k * H + h))

    lbspec = pl.BlockSpec((1, 1, LANES), lambda b, h: (h, 0, 0))
    return pl.pallas_call(
        _hgrn2_kernel,
        grid=(B, H),
        in_specs=[col(0), col(1), col(2), col(3), col(4), lbspec, lbspec,
                  pl.BlockSpec((1, LANES), lambda b, h: (0, 0))],
        out_specs=pl.BlockSpec((1, L, LANES), lambda b, h: (b, 0, h)),
        out_shape=jax.ShapeDtypeStruct((B, L, HG_WIDTH), BF16),
        scratch_shapes=[pltpu.VMEM((L, LANES), F32), pltpu.VMEM((L, LANES), F32),
                        pltpu.VMEM((L, LANES), BF16), pltpu.VMEM((L, LANES), BF16),
                        pltpu.VMEM((L // HG_CHUNK, LANES, LANES), F32),
                        pltpu.VMEM((L // HG_CHUNK, LANES, LANES), F32),
                        pltpu.VMEM((L // HG_CHUNK, 1, LANES), F32), pltpu.VMEM((L // HG_CHUNK, 1, LANES), F32),
                        pltpu.VMEM((LANES, LANES), F32), pltpu.VMEM((LANES, LANES), F32)],
        compiler_params=_params("parallel", "arbitrary"),
        name="hgrn2",
    )(proj, proj, proj, proj, proj, lb_f.reshape(H, 1, LANES), lb_b.reshape(H, 1, LANES),
      norm_g.reshape(1, LANES))


def _hyena_filter_kernel(z_ref, t_ref, w1_ref, b1_ref, w2_ref, b2_ref, w3_ref, b3_ref, w4f_ref, w4b_ref,
                         fr_ref, ad_ref, gs_ref, gd_ref, kn_ref):
    def hdot(a, b):
        return jnp.dot(a, b, precision=HIGHEST, preferred_element_type=F32)

    fr = fr_ref[...]
    h = jnp.sin(fr * (hdot(z_ref[...], w1_ref[...]) + b1_ref[...]))
    h = jnp.sin(fr * (hdot(h, w2_ref[...]) + b2_ref[...]))
    h = jnp.sin(fr * (hdot(h, w3_ref[...]) + b3_ref[...]))
    window = jnp.exp(-t_ref[...] * ad_ref[...])
    hf = hdot(h, w4f_ref[...]) * window
    hb = hdot(h, w4b_ref[...]) * window
    row = lax.broadcasted_iota(jnp.int32, hb.shape, 0)
    hb = jnp.where(row == 0, 0.0, hb)
    gs_ref[...] = hf + hb
    gd_ref[...] = hb - hf
    sgn = jnp.where(row % 2 == 0, 1.0, -1.0)
    kn_ref[...] = jnp.sum((hf + hb) * sgn, axis=0, keepdims=True) * (0.5 / hb.shape[0])


def _hyena_spectrum_kernel(a_ref, b_ref, gs_ref, gd_ref, kre_ref, kim_ref, *, n_fft):
    tf = a_ref.shape[0]
    f0 = pl.program_id(1) * tf
    row = lax.broadcasted_iota(jnp.int32, (tf, 1), 0) + f0
    wf = jnp.where(row == 0, 1.0 / n_fft, 2.0 / n_fft)
    kre_ref[...] = wf * jnp.dot(a_ref[...], gs_ref[...], precision=HIGHEST, preferred_element_type=F32)
    kim_ref[...] = wf * jnp.dot(b_ref[...], gd_ref[...], precision=HIGHEST, preferred_element_type=F32)


def hyena_filter_spectrum(L, cos_m, sin_m, w1, b1, w2, b2, w3, b3, w4, sin_freq):
    C = HY_WIDTH
    t = jnp.linspace(0.0, 1.0, L, dtype=F32)[:, None]
    w = (2.0 * math.pi / L) * jnp.arange(L, dtype=F32)[:, None]
    f = jnp.linspace(1e-4, HY_BANDS - 1, HY_BANDS, dtype=F32)[None, :]
    z = jnp.concatenate([t, jnp.cos(f * w), -jnp.sin(f * w)], axis=-1)
    zp = jnp.pad(z, ((0, 0), (0, LANES - HY_EMB)))
    w1p = jnp.pad(w1, ((0, LANES - HY_EMB), (0, 0)))
    max_decay = math.log(HY_TARGET) / HY_FAST_PCT
    min_decay = math.log(HY_TARGET) / HY_SLOW_PCT
    absd = jnp.abs(jnp.linspace(min_decay, max_decay, C, dtype=F32))[None, :]
    tc = min(512, C)
    full = lambda shape: pl.BlockSpec(shape, lambda j: (0, 0))
    gs, gd, knyq = pl.pallas_call(
        _hyena_filter_kernel,
        grid=(C // tc,),
        in_specs=[full((L, LANES)), full((L, 1)), full((LANES, HY_FILT)), full((1, HY_FILT)),
                  full((HY_FILT, HY_FILT)), full((1, HY_FILT)), full((HY_FILT, HY_FILT)), full((1, HY_FILT)),
                  pl.BlockSpec((HY_FILT, tc), lambda j: (0, j)),
                  pl.BlockSpec((HY_FILT, tc), lambda j: (0, C // tc + j)),
                  full((1, HY_FILT)),
                  pl.BlockSpec((1, tc), lambda j: (0, j))],
        out_specs=[pl.BlockSpec((L, tc), lambda j: (0, j)), pl.BlockSpec((L, tc), lambda j: (0, j)),
                   pl.BlockSpec((1, tc), lambda j: (0, j))],
        out_shape=[jax.ShapeDtypeStruct((L, C), F32), jax.ShapeDtypeStruct((L, C), F32),
                   jax.ShapeDtypeStruct((1, C), F32)],
        compiler_params=_params("arbitrary"),
        name="hyena_filter",
    )(zp, t, w1p, b1.reshape(1, -1), w2, b2.reshape(1, -1), w3, b3.reshape(1, -1), w4, w4,
      sin_freq.reshape(1, -1), absd)

    tf = min(256, L)
    kre, kim = pl.pallas_call(
        functools.partial(_hyena_spectrum_kernel, n_fft=2 * L),
        grid=(C // tc, L // tf),
        in_specs=[pl.BlockSpec((tf, L), lambda j, i: (i, 0)),
                  pl.BlockSpec((tf, L), lambda j, i: (i, 0)),
                  pl.BlockSpec((L, tc), lambda j, i: (0, j)),
                  pl.BlockSpec((L, tc), lambda j, i: (0, j))],
        out_specs=[pl.BlockSpec((tf, tc), lambda j, i: (i, j)), pl.BlockSpec((tf, tc), lambda j, i: (i, j))],
        out_shape=[jax.ShapeDtypeStruct((L, C), F32), jax.ShapeDtypeStruct((L, C), F32)],
        compiler_params=_params("parallel", "arbitrary"),
        name="hyena_spectrum",
    )(cos_m, sin_m, gs, gd)
    return kre, kim, knyq


def _short_conv(u, w_ref, b_ref):
    L = u.shape[0]
    row = lax.broadcasted_iota(jnp.int32, (L, 1), 0)
    prev = jnp.where(row == 0, 0.0, pltpu.roll(u, 1, axis=0))
    nxt = jnp.where(row == L - 1, 0.0, pltpu.roll(u, L - 1, axis=0))
    return prev * w_ref[0:1, :] + u * w_ref[1:2, :] + nxt * w_ref[2:3, :] + b_ref[...]


def _hyena_conv_kernel(x1_ref, x2_ref, v_ref, cw1_ref, cw2_ref, cw3_ref, cb1_ref, cb2_ref, cb3_ref,
                       a_ref, b_ref, kre_ref, kim_ref, knyq_ref, bias_ref, o_ref, y_ref, *, tf):
    L = v_ref.shape[1]
    x2 = _short_conv(x2_ref[0], cw2_ref, cb2_ref)
    vv = _short_conv(v_ref[0], cw3_ref, cb3_ref)
    vx = vv * x2
    vxb = vx.astype(BF16)
    sgn = jnp.where(lax.broadcasted_iota(jnp.int32, (L, 1), 0) % 2 == 0, 1.0, -1.0)
    v_nyq = jnp.sum(vx * sgn, axis=0, keepdims=True)
    y_ref[...] = vx * bias_ref[...] + sgn * (v_nyq * knyq_ref[...])
    for fb in range(L // tf):
        fs = slice(fb * tf, (fb + 1) * tf)
        p = _dot(a_ref[fs, :], vxb)
        q = _dot(b_ref[fs, :], vxb)
        kre = kre_ref[fs, :]
        kim = kim_ref[fs, :]
        r1 = (p * kre + q * kim).astype(BF16)
        r2 = (q * kre - p * kim).astype(BF16)
        y_ref[...] += _dot(a_ref[:, fs], r1) + _dot(b_ref[:, fs], r2)
    x1 = _short_conv(x1_ref[0], cw1_ref, cb1_ref)
    o_ref[0] = (y_ref[...] * x1).astype(o_ref.dtype)


def hyena_conv(proj, conv_w, conv_b, cos_b, sin_b, kre, kim, knyq, bias, *, tc):
    B, L, _ = proj.shape
    C = HY_WIDTH
    tc = min(tc, C)
    nc = C // tc
    tf = min(512, L)

    def grp(k):
        return pl.BlockSpec((1, L, tc), lambda c, b: (b, 0, k * nc + c))

    def cw(k):
        return pl.BlockSpec((HY_SHORT, tc), lambda c, b: (0, k * nc + c))

    def cb(k):
        return pl.BlockSpec((1, tc), lambda c, b: (0, k * nc + c))

    tbl = pl.BlockSpec((L, L), lambda c, b: (0, 0), pipeline_mode=pl.Buffered(1))
    chan = pl.BlockSpec((L, tc), lambda c, b: (0, c), pipeline_mode=pl.Buffered(1))
    vec = pl.BlockSpec((1, tc), lambda c, b: (0, c))
    return pl.pallas_call(
        functools.partial(_hyena_conv_kernel, tf=tf),
        grid=(nc, B),
        in_specs=[grp(0), grp(1), grp(2), cw(0), cw(1), cw(2), cb(0), cb(1), cb(2),
                  tbl, tbl, chan, chan, vec, vec],
        out_specs=pl.BlockSpec((1, L, tc), lambda c, b: (b, 0, c)),
        out_shape=jax.ShapeDtypeStruct((B, L, C), BF16),
        scratch_shapes=[pltpu.VMEM((L, tc), F32)],
        compiler_params=_params("parallel", "arbitrary"),
        name="hyena_conv",
    )(proj, proj, proj, conv_w, conv_w, conv_w, conv_b.reshape(1, -1), conv_b.reshape(1, -1),
      conv_b.reshape(1, -1), cos_b, sin_b, kre, kim, knyq, bias.reshape(1, -1))


def _dft_tables(L):
    n = 2 * L
    idx = jnp.arange(L, dtype=jnp.int32)
    m = (idx[:, None] * idx[None, :]) % n
    ang = m.astype(F32) * (2.0 * math.pi / n)
    return jnp.cos(ang), jnp.sin(ang)


def _axial_rope_tables(L):
    rows = L // GRID_W
    r, c = jnp.meshgrid(jnp.arange(rows, dtype=F32), jnp.arange(GRID_W, dtype=F32), indexing='ij')
    r = r.reshape(-1)
    c = c.reshape(-1)
    half = GQ_HEAD_DIM // 2
    inv = ROPE_THETA ** (-jnp.arange(0, half, 2, dtype=F32) / half)
    ang_r = r[:, None] * inv[None, :]
    ang_c = c[:, None] * inv[None, :]
    ang = jnp.concatenate([ang_r, ang_r, ang_c, ang_c], axis=-1)
    return jnp.cos(ang), jnp.sin(ang)


def _even_mixer(x, layer, norm_g, w_in, w_out, conv_w, conv_b, fw1, fb1, fw2, fb2, fw3, fb3, fw4, sin_freq,
                hy_bias, q_g, k_g, lam_q1, lam_k1, lam_q2, lam_k2, subln_g):
    B, L, D = x.shape
    x2 = x.reshape(B * L, D)
    proj = norm_matmul(x2, norm_g, w_in.astype(BF16), tm=1024, tn=512).reshape(B, L, -1)
    cos_m, sin_m = _dft_tables(L)
    kre, kim, knyq = hyena_filter_spectrum(L, cos_m, sin_m, fw1, fb1, fw2, fb2, fw3, fb3, fw4, sin_freq)
    o_a = hyena_conv(proj, conv_w, conv_b, cos_m.astype(BF16), sin_m.astype(BF16), kre, kim, knyq, hy_bias,
                     tc=256)
    lam_init = 0.8 - 0.6 * math.exp(-0.3 * layer)
    lam = (jnp.exp(jnp.sum(lam_q1.astype(F32) * lam_k1.astype(F32)))
           - jnp.exp(jnp.sum(lam_q2.astype(F32) * lam_k2.astype(F32))) + lam_init)
    slopes = jnp.asarray(np.array([2.0 ** (-8.0 * (h + 1) / DF_HEADS) for h in range(DF_HEADS)],
                                  dtype=np.float32))
    scal = jnp.concatenate([lam.reshape(1), slopes]).astype(F32)
    nb_hy = 3 * HY_WIDTH // LANES
    nb_qk = DF_QK_WIDTH // LANES
    o_b = diff_attention(proj, scal, q_g, k_g, subln_g, q_blk0=nb_hy, k_blk0=nb_hy + nb_qk,
                         v_blk0=nb_hy + 2 * nb_qk, out_scale=1.0 - lam_init, tq=512)
    out = proj2_residual(o_a.reshape(B * L, -1), o_b.reshape(B * L, -1), w_out.astype(BF16), x2,
                         tm=1024, tn=1024)
    return out.reshape(B, L, D)


def _odd_mixer(x, norm_g, lb_f, lb_b, w_in, w_out, hg_norm_g, q_g, k_g):
    B, L, D = x.shape
    x2 = x.reshape(B * L, D)
    proj = norm_matmul(x2, norm_g, w_in.astype(BF16), tm=1024, tn=512).reshape(B, L, -1)
    o_c = hgrn2_mixer(proj, lb_f, lb_b, hg_norm_g)
    cos, sin = _axial_rope_tables(L)
    gw = GQ_GROUP * GQ_HEAD_DIM
    q0 = 5 * HG_WIDTH
    k0 = q0 + GQ_WIDTH
    v0 = k0 + GQ_KV_HEADS * GQ_HEAD_DIM
    o_d = gqa_attention(proj, cos, sin, q_g, k_g, q_blk0=q0 // gw, k_blk0=k0 // LANES, v_blk0=v0 // LANES,
                        tq=512)
    out = proj2_residual(o_c.reshape(B * L, -1), o_d.reshape(B * L, -1), w_out.astype(BF16), x2,
                         tm=1024, tn=1024)
    return out.reshape(B, L, D)


def kernel(x, mem, norm_mix_g, norm_mem_q_g, norm_mem_kv_g, norm_ffn_g, ev_w_in, ev_w_out, hy_conv_w, hy_conv_b, hy_fw1, hy_fb1, hy_fw2, hy_fb2, hy_fw3, hy_fb3, hy_fw4, hy_sin_freq, hy_bias, df_q_g, df_k_g, df_lam_q1, df_lam_k1, df_lam_q2, df_lam_k2, df_subln_g, od_w_in, od_w_out, hg_lb_logits, hg_norm_g, gq_q_g, gq_k_g, ca_w_q, ca_w_kv, ca_w_out, ca_q_g, ca_k_g, mlp_w1, mlp_w2):
    B, L, D = x.shape
    M = mem.shape[1]
    depth = norm_mix_g.shape[0]
    lb_soft = jax.nn.softmax(hg_lb_logits.astype(F32), axis=1)
    lbs = jnp.cumsum(lb_soft, axis=1)
    lbs = lbs - lbs[:, :1]
    mem2 = mem.reshape(B * M, D)
    for layer in range(depth):
        j = layer // 2
        if layer % 2 == 0:
            x = _even_mixer(x, layer, norm_mix_g[layer], ev_w_in[j], ev_w_out[j], hy_conv_w[j], hy_conv_b[j],
                            hy_fw1[j], hy_fb1[j], hy_fw2[j], hy_fb2[j], hy_fw3[j], hy_fb3[j], hy_fw4[j],
                            hy_sin_freq[j], hy_bias[j], df_q_g[j], df_k_g[j], df_lam_q1[j], df_lam_k1[j],
                            df_lam_q2[j], df_lam_k2[j], df_subln_g[j])
        else:
            x = _odd_mixer(x, norm_mix_g[layer], lbs[0, layer], lbs[1, layer], od_w_in[j], od_w_out[j],
                           hg_norm_g[j], gq_q_g[j], gq_k_g[j])
        kv = norm_matmul(mem2, norm_mem_kv_g[layer], ca_w_kv[layer].astype(BF16), tm=1024, tn=512)
        x = cross_attention(x, kv.reshape(B, M, -1), norm_mem_q_g[layer], ca_w_q[layer].astype(BF16),
                            ca_q_g[layer], ca_k_g[layer], ca_w_out[layer].astype(BF16), tq=512)
        x = mlp_residual(x.reshape(B * L, D), norm_ffn_g[layer], mlp_w1[layer].astype(BF16),
                         mlp_w2[layer].astype(BF16), tm=512, th=1024).reshape(B, L, D)
    return x
```

```python
import functools
import math

import numpy as np
import jax
import jax.numpy as jnp
from jax import lax
from jax.experimental import pallas as pl
from jax.experimental.pallas import tpu as pltpu

F32 = jnp.float32
BF16 = jnp.bfloat16
HIGHEST = lax.Precision.HIGHEST

D_MODEL = 2048
NORM_EPS = 1e-6
GRID_W = 64
HY_WIDTH = D_MODEL // 2
HY_EMB = 33
HY_BANDS = (HY_EMB - 1) // 2
HY_FILT = 64
HY_SHORT = 3
HY_TARGET = 1e-2
HY_FAST_PCT = 0.3
HY_SLOW_PCT = 1.5
DF_HEADS = 8
DF_HEAD_DIM = 64
DF_V_DIM = 128
DF_QK_WIDTH = DF_HEADS * 2 * DF_HEAD_DIM
HG_WIDTH = D_MODEL // 2
HG_HEADS = 8
HG_DK = HG_WIDTH // HG_HEADS
GQ_WIDTH = D_MODEL - HG_WIDTH
GQ_HEADS = 8
GQ_KV_HEADS = 2
GQ_HEAD_DIM = GQ_WIDTH // GQ_HEADS
GQ_GROUP = GQ_HEADS // GQ_KV_HEADS
ROPE_THETA = 10000.0
CA_HEADS = 4
CA_HEAD_DIM = 128
CA_WIDTH = CA_HEADS * CA_HEAD_DIM

LANES = 128
SUBLANES = 8
VMEM_LIMIT_BYTES = 56 * 1024 * 1024

HG_CHUNK = 64
HG_BLOCK = SUBLANES
ATTN_STRIP = 256
HG_INTRA_CHUNKS = 8
LOG2E = 1.4426950408889634


def _params(*sem):
    return pltpu.CompilerParams(dimension_semantics=sem, vmem_limit_bytes=VMEM_LIMIT_BYTES)


def _rms(x, g):
    return x * lax.rsqrt(jnp.mean(x * x, axis=-1, keepdims=True) + NORM_EPS) * g


def _dot(a, b):
    return jnp.dot(a, b, preferred_element_type=F32)


def _dot_nt(a, b):
    return lax.dot_general(a, b, (((1,), (1,)), ((), ())), preferred_element_type=F32)


def _dot_tn(a, b):
    return lax.dot_general(a, b, (((0,), (0,)), ((), ())), preferred_element_type=F32)


def _pipeline3(n, stage_a, stage_b, stage_c):
    a = {}
    b = {}
    for t in range(n + 2):
        if t < n:
            a[t] = stage_a(t)
        if 0 <= t - 1 < n:
            b[t - 1] = stage_b(t - 1, a.pop(t - 1))
        if 0 <= t - 2 < n:
            stage_c(t - 2, b.pop(t - 2))


def _norm_matmul_kernel(x_ref, g_ref, w_ref, o_ref, hn_ref):
    @pl.when(pl.program_id(1) == 0)
    def _():
        hn_ref[...] = _rms(x_ref[...], g_ref[...]).astype(BF16)

    o_ref[...] = _dot(hn_ref[...], w_ref[...]).astype(o_ref.dtype)


def norm_matmul(x, g, w, *, tm, tn, out_dtype=F32):
    M, D = x.shape
    N = w.shape[1]
    tm = min(tm, M)
    tn = min(tn, N)
    return pl.pallas_call(
        _norm_matmul_kernel,
        grid=(M // tm, N // tn),
        in_specs=[pl.BlockSpec((tm, D), lambda i, j: (i, 0)),
                  pl.BlockSpec((1, D), lambda i, j: (0, 0)),
                  pl.BlockSpec((D, tn), lambda i, j: (0, j))],
        out_specs=pl.BlockSpec((tm, tn), lambda i, j: (i, j)),
        out_shape=jax.ShapeDtypeStruct((M, N), out_dtype),
        scratch_shapes=[pltpu.VMEM((tm, D), BF16)],
        compiler_params=_params("parallel", "arbitrary"),
        name="norm_matmul",
    )(x, g.reshape(1, D), w)


def _proj2_res_kernel(a1_ref, a2_ref, w1_ref, w2_ref, r_ref, o_ref):
    acc = _dot(a1_ref[...], w1_ref[...]) + _dot(a2_ref[...], w2_ref[...])
    o_ref[...] = r_ref[...] + acc


def proj2_residual(a1, a2, w, res, *, tm, tn):
    M, K1 = a1.shape
    K2 = a2.shape[1]
    assert K1 == K2
    N = w.shape[1]
    tm = min(tm, M)
    tn = min(tn, N)
    return pl.pallas_call(
        _proj2_res_kernel,
        grid=(M // tm, N // tn),
        in_specs=[pl.BlockSpec((tm, K1), lambda i, j: (i, 0)),
                  pl.BlockSpec((tm, K2), lambda i, j: (i, 0)),
                  pl.BlockSpec((K1, tn), lambda i, j: (0, j)),
                  pl.BlockSpec((K2, tn), lambda i, j: (1, j)),
                  pl.BlockSpec((tm, tn), lambda i, j: (i, j))],
        out_specs=pl.BlockSpec((tm, tn), lambda i, j: (i, j)),
        out_shape=jax.ShapeDtypeStruct((M, N), F32),
        compiler_params=_params("parallel", "arbitrary"),
        name="proj2_residual",
    )(a1, a2, w, w, res)


def _mlp_kernel(x_ref, g_ref, w1_ref, w2_ref, o_ref, hn_ref, acc_ref):
    j = pl.program_id(1)

    @pl.when(j == 0)
    def _():
        hn_ref[...] = _rms(x_ref[...], g_ref[...]).astype(BF16)
        acc_ref[...] = jnp.zeros_like(acc_ref)

    h1 = jnp.maximum(_dot(hn_ref[...], w1_ref[...]), 0.0)
    acc_ref[...] += _dot((h1 * h1).astype(BF16), w2_ref[...])

    @pl.when(j == pl.num_programs(1) - 1)
    def _():
        o_ref[...] = x_ref[...] + acc_ref[...]


def mlp_residual(x, g, w1, w2, *, tm, th):
    M, D = x.shape
    H = w1.shape[1]
    tm = min(tm, M)
    th = min(th, H)
    return pl.pallas_call(
        _mlp_kernel,
        grid=(M // tm, H // th),
        in_specs=[pl.BlockSpec((tm, D), lambda i, j: (i, 0)),
                  pl.BlockSpec((1, D), lambda i, j: (0, 0)),
                  pl.BlockSpec((D, th), lambda i, j: (0, j)),
                  pl.BlockSpec((th, D), lambda i, j: (j, 0))],
        out_specs=pl.BlockSpec((tm, D), lambda i, j: (i, 0)),
        out_shape=jax.ShapeDtypeStruct((M, D), F32),
        scratch_shapes=[pltpu.VMEM((tm, D), BF16), pltpu.VMEM((tm, D), F32)],
        compiler_params=_params("parallel", "arbitrary"),
        name="mlp_residual",
    )(x, g.reshape(1, D), w1, w2)


def _cross_attn_kernel(x_ref, g_ref, wq_ref, kv_ref, qg_ref, kg_ref, wo_ref, o_ref):
    x = x_ref[0]
    hn = _rms(x, g_ref[...]).astype(BF16)
    q = _dot(hn, wq_ref[...])
    kv = kv_ref[0]
    scale = CA_HEAD_DIM ** -0.5
    outs = []
    for h in range(CA_HEADS):
        sl = slice(h * CA_HEAD_DIM, (h + 1) * CA_HEAD_DIM)
        qh = (_rms(q[:, sl], qg_ref[...]) * scale).astype(BF16)
        kh = _rms(kv[:, sl], kg_ref[...]).astype(BF16)
        vh = kv[:, CA_WIDTH + h * CA_HEAD_DIM:CA_WIDTH + (h + 1) * CA_HEAD_DIM].astype(BF16)
        s = _dot_nt(qh, kh)
        e = jnp.exp(s - jnp.max(s, axis=-1, keepdims=True))
        p = e / jnp.sum(e, axis=-1, keepdims=True)
        outs.append(_dot(p.astype(BF16), vh))
    o = jnp.concatenate(outs, axis=-1).astype(BF16)
    o_ref[0] = x + _dot(o, wo_ref[...])


def cross_attention(x, kv, g, wq, qg, kg, wo, *, tq):
    B, L, D = x.shape
    M = kv.shape[1]
    tq = min(tq, L)
    return pl.pallas_call(
        _cross_attn_kernel,
        grid=(B, L // tq),
        in_specs=[pl.BlockSpec((1, tq, D), lambda b, i: (b, i, 0)),
                  pl.BlockSpec((1, D), lambda b, i: (0, 0)),
                  pl.BlockSpec((D, CA_WIDTH), lambda b, i: (0, 0)),
                  pl.BlockSpec((1, M, 2 * CA_WIDTH), lambda b, i: (b, 0, 0)),
                  pl.BlockSpec((1, CA_HEAD_DIM), lambda b, i: (0, 0)),
                  pl.BlockSpec((1, CA_HEAD_DIM), lambda b, i: (0, 0)),
                  pl.BlockSpec((CA_WIDTH, D), lambda b, i: (0, 0))],
        out_specs=pl.BlockSpec((1, tq, D), lambda b, i: (b, i, 0)),
        out_shape=jax.ShapeDtypeStruct((B, L, D), F32),
        compiler_params=_params("parallel", "arbitrary"),
        name="cross_attention",
    )(x, g.reshape(1, D), wq, kv, qg.reshape(1, -1), kg.reshape(1, -1), wo)


def _half_rms(x, g2, lo_mask):
    sq = x * x
    ms_lo = jnp.sum(jnp.where(lo_mask, sq, 0.0), axis=-1, keepdims=True)
    ms_hi = jnp.sum(jnp.where(lo_mask, 0.0, sq), axis=-1, keepdims=True)
    inv = jnp.where(lo_mask, lax.rsqrt(ms_lo * (1.0 / DF_HEAD_DIM) + NORM_EPS),
                    lax.rsqrt(ms_hi * (1.0 / DF_HEAD_DIM) + NORM_EPS))
    return x * inv * g2


def _diff_attn_kernel(sc_ref, q_ref, k_ref, v_ref, qg_ref, kg_ref, sg_ref, o_ref, kn_ref, vb_ref, *,
                      out_scale):
    h = pl.program_id(1)
    qi = pl.program_id(2)
    tq = q_ref.shape[1]
    L = k_ref.shape[1]
    lane = lax.broadcasted_iota(jnp.int32, (1, LANES), 1)
    lo_mask = lane < DF_HEAD_DIM

    @pl.when(qi == 0)
    def _():
        kn_ref[...] = _half_rms(k_ref[0], kg_ref[...], lo_mask).astype(BF16)
        vb_ref[:, :LANES] = v_ref[0].astype(BF16)
        vb_ref[:, LANES:] = jnp.ones((L, LANES), BF16)

    lam = sc_ref[0]
    slope = sc_ref[1 + h] * LOG2E
    ts = min(ATTN_STRIP, tq)
    col = lax.broadcasted_iota(jnp.int32, (1, L), 1).astype(F32)

    def scores(i):
        r0 = i * ts
        qn = _half_rms(q_ref[0, r0:r0 + ts, :], qg_ref[...], lo_mask) * (DF_HEAD_DIM ** -0.5 * LOG2E)
        row = (lax.broadcasted_iota(jnp.int32, (ts, 1), 0) + (qi * tq + r0)).astype(F32)
        bias = jnp.abs(row - col) * slope
        kn = kn_ref[...]
        return (_dot_nt(jnp.where(lo_mask, qn, 0.0).astype(BF16), kn) - bias,
                _dot_nt(jnp.where(lo_mask, 0.0, qn).astype(BF16), kn) - bias)

    def exps(i, ss):
        return tuple(jnp.exp2(s - jnp.max(s, axis=-1, keepdims=True)).astype(BF16) for s in ss)

    def values(i, es):
        r1, r2 = (_dot(e, vb_ref[...]) for e in es)
        o = (r1[:, :LANES] * (1.0 / r1[:, LANES:LANES + 1])
             - r2[:, :LANES] * (lam / r2[:, LANES:LANES + 1]))
        o_ref[0, i * ts:(i + 1) * ts, :] = (_rms(o, sg_ref[...]) * out_scale).astype(o_ref.dtype)

    _pipeline3(tq // ts, scores, exps, values)


def diff_attention(proj, scal, q_g, k_g, subln_g, *, q_blk0, k_blk0, v_blk0, out_scale, tq):
    B, L, _ = proj.shape
    tq = min(tq, L)
    qg2 = jnp.concatenate([q_g, q_g]).reshape(1, LANES)
    kg2 = jnp.concatenate([k_g, k_g]).reshape(1, LANES)
    return pl.pallas_call(
        functools.partial(_diff_attn_kernel, out_scale=out_scale),
        grid=(B, DF_HEADS, L // tq),
        in_specs=[pl.BlockSpec(memory_space=pltpu.SMEM),
                  pl.BlockSpec((1, tq, LANES), lambda b, h, i: (b, i, q_blk0 + h)),
                  pl.BlockSpec((1, L, LANES), lambda b, h, i: (b, 0, k_blk0 + h)),
                  pl.BlockSpec((1, L, LANES), lambda b, h, i: (b, 0, v_blk0 + h)),
                  pl.BlockSpec((1, LANES), lambda b, h, i: (0, 0)),
                  pl.BlockSpec((1, LANES), lambda b, h, i: (0, 0)),
                  pl.BlockSpec((1, LANES), lambda b, h, i: (0, 0))],
        out_specs=pl.BlockSpec((1, tq, LANES), lambda b, h, i: (b, i, h)),
        out_shape=jax.ShapeDtypeStruct((B, L, DF_HEADS * DF_V_DIM), BF16),
        scratch_shapes=[pltpu.VMEM((L, LANES), BF16), pltpu.VMEM((L, 2 * LANES), BF16)],
        compiler_params=_params("parallel", "parallel", "arbitrary"),
        name="diff_attention",
    )(scal, proj, proj, proj, qg2, kg2, subln_g.reshape(1, LANES))


def _rope(x, cos, sin, first_mask):
    q = GQ_HEAD_DIM // 4
    rot = jnp.where(first_mask, -pltpu.roll(x, LANES - q, axis=1), pltpu.roll(x, q, axis=1))
    return x * cos + rot * sin


def _gqa_kernel(q_ref, k_ref, v_ref, cq_ref, sq_ref, ck_ref, sk_ref, qg_ref, kg_ref, o_ref, kn_ref, vb_ref):
    qi = pl.program_id(2)
    lane = lax.broadcasted_iota(jnp.int32, (1, LANES), 1)
    first_mask = (lane % (GQ_HEAD_DIM // 2)) < (GQ_HEAD_DIM // 4)

    @pl.when(qi == 0)
    def _():
        kn = _rms(k_ref[0], kg_ref[...])
        kn_ref[...] = _rope(kn, ck_ref[...], sk_ref[...], first_mask).astype(BF16)
        vb_ref[:, :LANES] = v_ref[0].astype(BF16)
        vb_ref[:, LANES:] = jnp.ones((k_ref.shape[1], LANES), BF16)

    scale = GQ_HEAD_DIM ** -0.5 * LOG2E
    tq = q_ref.shape[1]
    ts = min(ATTN_STRIP, tq)
    nr = tq // ts

    def where(i):
        r0 = (i % nr) * ts
        return slice(r0, r0 + ts), slice((i // nr) * GQ_HEAD_DIM, (i // nr + 1) * GQ_HEAD_DIM)

    def scores(i):
        rs, cs = where(i)
        qn = _rope(_rms(q_ref[0, rs, cs], qg_ref[...]), cq_ref[rs, :], sq_ref[rs, :], first_mask) * scale
        return _dot_nt(qn.astype(BF16), kn_ref[...])

    def exps(i, s):
        return jnp.exp2(s - jnp.max(s, axis=-1, keepdims=True)).astype(BF16)

    def values(i, e):
        rs, cs = where(i)
        res = _dot(e, vb_ref[...])
        o_ref[0, rs, cs] = (res[:, :LANES] * (1.0 / res[:, LANES:LANES + 1])).astype(o_ref.dtype)

    _pipeline3(GQ_GROUP * nr, scores, exps, values)


def gqa_attention(proj, cos, sin, q_g, k_g, *, q_blk0, k_blk0, v_blk0, tq):
    B, L, _ = proj.shape
    tq = min(tq, L)
    gw = GQ_GROUP * GQ_HEAD_DIM
    return pl.pallas_call(
        _gqa_kernel,
        grid=(B, GQ_KV_HEADS, L // tq),
        in_specs=[pl.BlockSpec((1, tq, gw), lambda b, g, i: (b, i, q_blk0 + g)),
                  pl.BlockSpec((1, L, LANES), lambda b, g, i: (b, 0, k_blk0 + g)),
                  pl.BlockSpec((1, L, LANES), lambda b, g, i: (b, 0, v_blk0 + g)),
                  pl.BlockSpec((tq, LANES), lambda b, g, i: (i, 0)),
                  pl.BlockSpec((tq, LANES), lambda b, g, i: (i, 0)),
                  pl.BlockSpec((L, LANES), lambda b, g, i: (0, 0)),
                  pl.BlockSpec((L, LANES), lambda b, g, i: (0, 0)),
                  pl.BlockSpec((1, LANES), lambda b, g, i: (0, 0)),
                  pl.BlockSpec((1, LANES), lambda b, g, i: (0, 0))],
        out_specs=pl.BlockSpec((1, tq, gw), lambda b, g, i: (b, i, g)),
        out_shape=jax.ShapeDtypeStruct((B, L, GQ_WIDTH), BF16),
        scratch_shapes=[pltpu.VMEM((L, LANES), BF16), pltpu.VMEM((L, 2 * LANES), BF16)],
        compiler_params=_params("parallel", "parallel", "arbitrary"),
        name="gqa_attention",
    )(proj, proj, proj, cos, sin, cos, sin, q_g.reshape(1, LANES), k_g.reshape(1, LANES))


def _hgrn2_intra_streams(streams):
    C = HG_CHUNK
    nb = C // HG_BLOCK
    ns = range(len(streams))
    rev = [st[5] for st in streams]
    r_i = lax.broadcasted_iota(jnp.int32, (C, C), 0)
    c_i = lax.broadcasted_iota(jnp.int32, (C, C), 1)

    kk, lg = [], []
    for q, fz, v, vb, lb, _ in streams:
        f = lb + (1.0 - lb) * jax.nn.sigmoid(fz)
        kk.append(1.0 - f)
        lg.append(jnp.log(f) * LOG2E)

    tri_f = jnp.where(r_i >= c_i, 1.0, 0.0).astype(BF16)
    tri_b = jnp.where(r_i <= c_i, 1.0, 0.0).astype(BF16)
    b = []
    for i in ns:
        hi = lg[i].astype(BF16)
        r1 = lg[i] - hi.astype(F32)
        mid = r1.astype(BF16)
        lo = (r1 - mid.astype(F32)).astype(BF16)
        tri = tri_b if rev[i] else tri_f
        b.append(_dot(tri, hi) + _dot(tri, mid) + _dot(tri, lo))
    b_end = [b[i][0:1, :] if rev[i] else b[i][C - 1:C, :] for i in ns]
    qe = [(streams[i][0] * jnp.exp2(b[i])).astype(BF16) for i in ns]
    u = [_dot_tn(streams[i][3], (kk[i] * jnp.exp2(b_end[i] - b[i])).astype(BF16)) for i in ns]

    pos = lax.broadcasted_iota(jnp.int32, (nb, HG_BLOCK, LANES), 1)
    b3 = [b[i].reshape(nb, HG_BLOCK, LANES) for i in ns]
    q3 = [streams[i][0].reshape(nb, HG_BLOCK, LANES) for i in ns]
    seg_pos = lax.broadcasted_iota(jnp.int32, (C, HG_BLOCK * LANES), 1) // LANES
    row_pos = lax.broadcasted_iota(jnp.int32, (C, HG_BLOCK * LANES), 0) % HG_BLOCK
    kcat = []
    for i in ns:
        kb = kk[i].astype(BF16)
        kcat.append(jnp.where(seg_pos == row_pos, jnp.concatenate([kb] * HG_BLOCK, axis=1),
                              jnp.zeros((), BF16)))
    ps = [[] for i in ns]
    for s in range(HG_BLOCK):
        for i in ns:
            valid = (pos <= s) if rev[i] else (pos >= s)
            d = jnp.where(valid, b3[i] - b3[i][:, s:s + 1, :], -jnp.inf)
            ps[i].append((q3[i] * jnp.exp2(d)).reshape(C, LANES).astype(BF16))
    same_blk = (r_i // HG_BLOCK) == (c_i // HG_BLOCK)
    adiag = [_dot_nt(jnp.concatenate(ps[i], axis=1), kcat[i]) for i in ns]

    zrow = jnp.zeros((1, 1, LANES), F32)
    qt = []
    for i in ns:
        if rev[i]:
            rblk3 = jnp.concatenate([b3[i][1:, 0:1, :], zrow], axis=0)
        else:
            rblk3 = jnp.concatenate([zrow, b3[i][:nb - 1, HG_BLOCK - 1:HG_BLOCK, :]], axis=0)
        qt.append((q3[i] * jnp.exp2(b3[i] - rblk3)).reshape(C, LANES))
    rows = [[] for i in ns]
    for blk in range(nb):
        for i in ns:
            if rev[i]:
                edge = (blk + 1) * HG_BLOCK
                if edge == C:
                    rows[i].append(jnp.zeros((HG_BLOCK, C), F32))
                    continue
                kt = kk[i][edge:, :] * jnp.exp2(b[i][edge:edge + 1, :] - b[i][edge:, :])
                kfull = jnp.concatenate([jnp.zeros((edge, LANES), F32), kt], axis=0)
            else:
                edge = blk * HG_BLOCK
                if edge == 0:
                    rows[i].append(jnp.zeros((HG_BLOCK, C), F32))
                    continue
                kt = kk[i][:edge, :] * jnp.exp2(b[i][edge - 1:edge, :] - b[i][:edge, :])
                kfull = jnp.concatenate([kt, jnp.zeros((C - edge, LANES), F32)], axis=0)
            rows[i].append(_dot_nt(qt[i][blk * HG_BLOCK:(blk + 1) * HG_BLOCK, :].astype(BF16),
                                   kfull.astype(BF16)))
    outs = []
    for i in ns:
        att = jnp.where(same_blk, adiag[i], jnp.concatenate(rows[i], axis=0))
        o = _dot(att.astype(BF16), streams[i][3])
        outs.append((o, qe[i], u[i], jnp.exp2(b_end[i])))
    return outs


def _hgrn2_kernel(q_ref, ff_ref, fb_ref, i_ref, g_ref, lbf_ref, lbb_ref, ng_ref, o_ref, of_ref, ob_ref,
                  sf_ref, sb_ref):
    L = q_ref.shape[1]
    C = HG_CHUNK
    n = L // C
    gsize = math.gcd(HG_INTRA_CHUNKS, n)
    qscale = HG_DK ** -0.5

    def rows(c):
        return pl.ds(pl.multiple_of(c * C, C), C)

    sf_ref[...] = jnp.zeros_like(sf_ref)
    sb_ref[...] = jnp.zeros_like(sb_ref)

    def group(it, carry):
        streams = []
        dests = []
        for k in range(gsize):
            c = it * gsize + k
            for fz_ref, lb_ref, o_s, s_s, cc, reverse in ((ff_ref, lbf_ref, of_ref, sf_ref, c, False),
                                                          (fb_ref, lbb_ref, ob_ref, sb_ref, n - 1 - c, True)):
                r = rows(cc)
                v = i_ref[0, r, :]
                streams.append((q_ref[0, r, :] * qscale, fz_ref[0, r, :], v, v.astype(BF16), lb_ref[0], reverse))
                dests.append((o_s, s_s, r))
        state = {id(sf_ref): sf_ref[...], id(sb_ref): sb_ref[...]}
        for (o, qe, u, e), (o_s, s_s, r) in zip(_hgrn2_intra_streams(streams), dests):
            st = state[id(s_s)]
            o_s[r, :] = o + _dot_nt(qe, st.astype(BF16))
            state[id(s_s)] = st * e + u
        sf_ref[...] = state[id(sf_ref)]
        sb_ref[...] = state[id(sb_ref)]
        return carry

    lax.fori_loop(0, n // gsize, group, 0)
    o = _rms(of_ref[...] + ob_ref[...], ng_ref[...])
    g = g_ref[0]
    o_ref[0] = (o * (g * jax.nn.sigmoid(g))).astype(o_ref.dtype)


def hgrn2_mixer(proj, lb_f, lb_b, norm_g):
    B, L, _ = proj.shape
    H = HG_HEADS

    def col(k):
        return pl.BlockSpec((1, L, LANES), lambda b, h: (b, 0, k * H + h))

    lbspec = pl.BlockSpec((1, 1, LANES), lambda b, h: (h, 0, 0))
    return pl.pallas_call(
        _hgrn2_kernel,
        grid=(B, H),
        in_specs=[col(0), col(1), col(2), col(3), col(4), lbspec, lbspec,
                  pl.BlockSpec((1, LANES), lambda b, h: (0, 0))],
        out_specs=pl.BlockSpec((1, L, LANES), lambda b, h: (b, 0, h)),
        out_shape=jax.ShapeDtypeStruct((B, L, HG_WIDTH), BF16),
        scratch_shapes=[pltpu.VMEM((L, LANES), F32), pltpu.VMEM((L, LANES), F32),
                        pltpu.VMEM((LANES, LANES), F32), pltpu.VMEM((LANES, LANES), F32)],
        compiler_params=_params("parallel", "arbitrary"),
        name="hgrn2",
    )(proj, proj, proj, proj, proj, lb_f.reshape(H, 1, LANES), lb_b.reshape(H, 1, LANES),
      norm_g.reshape(1, LANES))


def _hyena_filter_kernel(z_ref, t_ref, w1_ref, b1_ref, w2_ref, b2_ref, w3_ref, b3_ref, w4f_ref, w4b_ref,
                         fr_ref, ad_ref, gs_ref, gd_ref, kn_ref):
    def hdot(a, b):
        return jnp.dot(a, b, precision=HIGHEST, preferred_element_type=F32)

    fr = fr_ref[...]
    h = jnp.sin(fr * (hdot(z_ref[...], w1_ref[...]) + b1_ref[...]))
    h = jnp.sin(fr * (hdot(h, w2_ref[...]) + b2_ref[...]))
    h = jnp.sin(fr * (hdot(h, w3_ref[...]) + b3_ref[...]))
    window = jnp.exp(-t_ref[...] * ad_ref[...])
    hf = hdot(h, w4f_ref[...]) * window
    hb = hdot(h, w4b_ref[...]) * window
    row = lax.broadcasted_iota(jnp.int32, hb.shape, 0)
    hb = jnp.where(row == 0, 0.0, hb)
    gs_ref[...] = hf + hb
    gd_ref[...] = hb - hf
    sgn = jnp.where(row % 2 == 0, 1.0, -1.0)
    kn_ref[...] = jnp.sum((hf + hb) * sgn, axis=0, keepdims=True) * (0.5 / hb.shape[0])


def _hyena_spectrum_kernel(a_ref, b_ref, gs_ref, gd_ref, kre_ref, kim_ref, *, n_fft):
    tf = a_ref.shape[0]
    f0 = pl.program_id(1) * tf
    row = lax.broadcasted_iota(jnp.int32, (tf, 1), 0) + f0
    wf = jnp.where(row == 0, 1.0 / n_fft, 2.0 / n_fft)
    kre_ref[...] = wf * jnp.dot(a_ref[...], gs_ref[...], precision=HIGHEST, preferred_element_type=F32)
    kim_ref[...] = wf * jnp.dot(b_ref[...], gd_ref[...], precision=HIGHEST, preferred_element_type=F32)


def hyena_filter_spectrum(L, cos_m, sin_m, w1, b1, w2, b2, w3, b3, w4, sin_freq):
    C = HY_WIDTH
    t = jnp.linspace(0.0, 1.0, L, dtype=F32)[:, None]
    w = (2.0 * math.pi / L) * jnp.arange(L, dtype=F32)[:, None]
    f = jnp.linspace(1e-4, HY_BANDS - 1, HY_BANDS, dtype=F32)[None, :]
    z = jnp.concatenate([t, jnp.cos(f * w), -jnp.sin(f * w)], axis=-1)
    zp = jnp.pad(z, ((0, 0), (0, LANES - HY_EMB)))
    w1p = jnp.pad(w1, ((0, LANES - HY_EMB), (0, 0)))
    max_decay = math.log(HY_TARGET) / HY_FAST_PCT
    min_decay = math.log(HY_TARGET) / HY_SLOW_PCT
    absd = jnp.abs(jnp.linspace(min_decay, max_decay, C, dtype=F32))[None, :]
    tc = min(512, C)
    full = lambda shape: pl.BlockSpec(shape, lambda j: (0, 0))
    gs, gd, knyq = pl.pallas_call(
        _hyena_filter_kernel,
        grid=(C // tc,),
        in_specs=[full((L, LANES)), full((L, 1)), full((LANES, HY_FILT)), full((1, HY_FILT)),
                  full((HY_FILT, HY_FILT)), full((1, HY_FILT)), full((HY_FILT, HY_FILT)), full((1, HY_FILT)),
                  pl.BlockSpec((HY_FILT, tc), lambda j: (0, j)),
                  pl.BlockSpec((HY_FILT, tc), lambda j: (0, C // tc + j)),
                  full((1, HY_FILT)),
                  pl.BlockSpec((1, tc), lambda j: (0, j))],
        out_specs=[pl.BlockSpec((L, tc), lambda j: (0, j)), pl.BlockSpec((L, tc), lambda j: (0, j)),
                   pl.BlockSpec((1, tc), lambda j: (0, j))],
        out_shape=[jax.ShapeDtypeStruct((L, C), F32), jax.ShapeDtypeStruct((L, C), F32),
                   jax.ShapeDtypeStruct((1, C), F32)],
        compiler_params=_params("arbitrary"),
        name="hyena_filter",
    )(zp, t, w1p, b1.reshape(1, -1), w2, b2.reshape(1, -1), w3, b3.reshape(1, -1), w4, w4,
      sin_freq.reshape(1, -1), absd)

    tf = min(256, L)
    kre, kim = pl.pallas_call(
        functools.partial(_hyena_spectrum_kernel, n_fft=2 * L),
        grid=(C // tc, L // tf),
        in_specs=[pl.BlockSpec((tf, L), lambda j, i: (i, 0)),
                  pl.BlockSpec((tf, L), lambda j, i: (i, 0)),
                  pl.BlockSpec((L, tc), lambda j, i: (0, j)),
                  pl.BlockSpec((L, tc), lambda j, i: (0, j))],
        out_specs=[pl.BlockSpec((tf, tc), lambda j, i: (i, j)), pl.BlockSpec((tf, tc), lambda j, i: (i, j))],
        out_shape=[jax.ShapeDtypeStruct((L, C), F32), jax.ShapeDtypeStruct((L, C), F32)],
        compiler_params=_params("parallel", "arbitrary"),
        name="hyena_spectrum",
    )(cos_m, sin_m, gs, gd)
    return kre, kim, knyq


def _short_conv(u, w_ref, b_ref):
    L = u.shape[0]
    row = lax.broadcasted_iota(jnp.int32, (L, 1), 0)
    prev = jnp.where(row == 0, 0.0, pltpu.roll(u, 1, axis=0))
    nxt = jnp.where(row == L - 1, 0.0, pltpu.roll(u, L - 1, axis=0))
    return prev * w_ref[0:1, :] + u * w_ref[1:2, :] + nxt * w_ref[2:3, :] + b_ref[...]


def _hyena_conv_kernel(x1_ref, x2_ref, v_ref, cw1_ref, cw2_ref, cw3_ref, cb1_ref, cb2_ref, cb3_ref,
                       a_ref, b_ref, kre_ref, kim_ref, knyq_ref, bias_ref, o_ref, r1_ref, r2_ref, *, tf):
    L = v_ref.shape[1]
    x2 = _short_conv(x2_ref[0], cw2_ref, cb2_ref)
    vv = _short_conv(v_ref[0], cw3_ref, cb3_ref)
    vx = vv * x2
    vxb = vx.astype(BF16)
    sgn = jnp.where(lax.broadcasted_iota(jnp.int32, (L, 1), 0) % 2 == 0, 1.0, -1.0)
    def forward(fb):
        fs = slice(fb * tf, (fb + 1) * tf)
        return _dot(a_ref[fs, :], vxb), _dot(b_ref[fs, :], vxb)

    def product(fb, pq):
        p, q = pq
        fs = slice(fb * tf, (fb + 1) * tf)
        kre = kre_ref[fs, :]
        kim = kim_ref[fs, :]
        r1_ref[fs, :] = (p * kre + q * kim).astype(BF16)
        r2_ref[fs, :] = (q * kre - p * kim).astype(BF16)

    _pipeline3(L // tf, forward, product, lambda fb, _: None)

    v_nyq = jnp.sum(vx * sgn, axis=0, keepdims=True)
    y0 = vx * bias_ref[...] + sgn * (v_nyq * knyq_ref[...])
    x1 = _short_conv(x1_ref[0], cw1_ref, cb1_ref)
    y = _dot(a_ref[...], r1_ref[...]) + _dot(b_ref[...], r2_ref[...])
    o_ref[0] = ((y0 + y) * x1).astype(o_ref.dtype)


def hyena_conv(proj, conv_w, conv_b, cos_b, sin_b, kre, kim, knyq, bias, *, tc):
    B, L, _ = proj.shape
    C = HY_WIDTH
    tc = min(tc, C)
    nc = C // tc
    tf = min(512, L)

    def grp(k):
        return pl.BlockSpec((1, L, tc), lambda c, b: (b, 0, k * nc + c))

    def cw(k):
        return pl.BlockSpec((HY_SHORT, tc), lambda c, b: (0, k * nc + c))

    def cb(k):
        return pl.BlockSpec((1, tc), lambda c, b: (0, k * nc + c))

    tbl = pl.BlockSpec((L, L), lambda c, b: (0, 0), pipeline_mode=pl.Buffered(1))
    chan = pl.BlockSpec((L, tc), lambda c, b: (0, c), pipeline_mode=pl.Buffered(1))
    vec = pl.BlockSpec((1, tc), lambda c, b: (0, c))
    return pl.pallas_call(
        functools.partial(_hyena_conv_kernel, tf=tf),
        grid=(nc, B),
        in_specs=[grp(0), grp(1), grp(2), cw(0), cw(1), cw(2), cb(0), cb(1), cb(2),
                  tbl, tbl, chan, chan, vec, vec],
        out_specs=pl.BlockSpec((1, L, tc), lambda c, b: (b, 0, c)),
        out_shape=jax.ShapeDtypeStruct((B, L, C), BF16),
        scratch_shapes=[pltpu.VMEM((L, tc), BF16), pltpu.VMEM((L, tc), BF16)],
        compiler_params=_params("parallel", "arbitrary"),
        name="hyena_conv",
    )(proj, proj, proj, conv_w, conv_w, conv_w, conv_b.reshape(1, -1), conv_b.reshape(1, -1),
      conv_b.reshape(1, -1), cos_b, sin_b, kre, kim, knyq, bias.reshape(1, -1))


def _dft_tables(L):
    n = 2 * L
    idx = jnp.arange(L, dtype=jnp.int32)
    m = (idx[:, None] * idx[None, :]) % n
    ang = m.astype(F32) * (2.0 * math.pi / n)
    return jnp.cos(ang), jnp.sin(ang)


def _axial_rope_tables(L):
    rows = L // GRID_W
    r, c = jnp.meshgrid(jnp.arange(rows, dtype=F32), jnp.arange(GRID_W, dtype=F32), indexing='ij')
    r = r.reshape(-1)
    c = c.reshape(-1)
    half = GQ_HEAD_DIM // 2
    inv = ROPE_THETA ** (-jnp.arange(0, half, 2, dtype=F32) / half)
    ang_r = r[:, None] * inv[None, :]
    ang_c = c[:, None] * inv[None, :]
    ang = jnp.concatenate([ang_r, ang_r, ang_c, ang_c], axis=-1)
    return jnp.cos(ang), jnp.sin(ang)


def _even_mixer(x, layer, norm_g, w_in, w_out, conv_w, conv_b, fw1, fb1, fw2, fb2, fw3, fb3, fw4, sin_freq,
                hy_bias, q_g, k_g, lam_q1, lam_k1, lam_q2, lam_k2, subln_g):
    B, L, D = x.shape
    x2 = x.reshape(B * L, D)
    proj = norm_matmul(x2, norm_g, w_in.astype(BF16), tm=1024, tn=1024).reshape(B, L, -1)
    cos_m, sin_m = _dft_tables(L)
    kre, kim, knyq = hyena_filter_spectrum(L, cos_m, sin_m, fw1, fb1, fw2, fb2, fw3, fb3, fw4, sin_freq)
    o_a = hyena_conv(proj, conv_w, conv_b, cos_m.astype(BF16), sin_m.astype(BF16), kre, kim, knyq, hy_bias,
                     tc=256)
    lam_init = 0.8 - 0.6 * math.exp(-0.3 * layer)
    lam = (jnp.exp(jnp.sum(lam_q1.astype(F32) * lam_k1.astype(F32)))
           - jnp.exp(jnp.sum(lam_q2.astype(F32) * lam_k2.astype(F32))) + lam_init)
    slopes = jnp.asarray(np.array([2.0 ** (-8.0 * (h + 1) / DF_HEADS) for h in range(DF_HEADS)],
                                  dtype=np.float32))
    scal = jnp.concatenate([lam.reshape(1), slopes]).astype(F32)
    nb_hy = 3 * HY_WIDTH // LANES
    nb_qk = DF_QK_WIDTH // LANES
    o_b = diff_attention(proj, scal, q_g, k_g, subln_g, q_blk0=nb_hy, k_blk0=nb_hy + nb_qk,
                         v_blk0=nb_hy + 2 * nb_qk, out_scale=1.0 - lam_init, tq=1024)
    out = proj2_residual(o_a.reshape(B * L, -1), o_b.reshape(B * L, -1), w_out.astype(BF16), x2,
                         tm=1024, tn=1024)
    return out.reshape(B, L, D)


def _odd_mixer(x, norm_g, lb_f, lb_b, w_in, w_out, hg_norm_g, q_g, k_g):
    B, L, D = x.shape
    x2 = x.reshape(B * L, D)
    proj = norm_matmul(x2, norm_g, w_in.astype(BF16), tm=1024, tn=1664).reshape(B, L, -1)
    o_c = hgrn2_mixer(proj, lb_f, lb_b, hg_norm_g)
    cos, sin = _axial_rope_tables(L)
    gw = GQ_GROUP * GQ_HEAD_DIM
    q0 = 5 * HG_WIDTH
    k0 = q0 + GQ_WIDTH
    v0 = k0 + GQ_KV_HEADS * GQ_HEAD_DIM
    o_d = gqa_attention(proj, cos, sin, q_g, k_g, q_blk0=q0 // gw, k_blk0=k0 // LANES, v_blk0=v0 // LANES,
                        tq=512)
    out = proj2_residual(o_c.reshape(B * L, -1), o_d.reshape(B * L, -1), w_out.astype(BF16), x2,
                         tm=1024, tn=1024)
    return out.reshape(B, L, D)


def kernel(x, mem, norm_mix_g, norm_mem_q_g, norm_mem_kv_g, norm_ffn_g, ev_w_in, ev_w_out, hy_conv_w, hy_conv_b, hy_fw1, hy_fb1, hy_fw2, hy_fb2, hy_fw3, hy_fb3, hy_fw4, hy_sin_freq, hy_bias, df_q_g, df_k_g, df_lam_q1, df_lam_k1, df_lam_q2, df_lam_k2, df_subln_g, od_w_in, od_w_out, hg_lb_logits, hg_norm_g, gq_q_g, gq_k_g, ca_w_q, ca_w_kv, ca_w_out, ca_q_g, ca_k_g, mlp_w1, mlp_w2):
    B, L, D = x.shape
    M = mem.shape[1]
    depth = norm_mix_g.shape[0]
    lb_soft = jax.nn.softmax(hg_lb_logits.astype(F32), axis=1)
    lbs = jnp.cumsum(lb_soft, axis=1)
    lbs = lbs - lbs[:, :1]
    mem2 = mem.reshape(B * M, D)
    for layer in range(depth):
        j = layer // 2
        if layer % 2 == 0:
            x = _even_mixer(x, layer, norm_mix_g[layer], ev_w_in[j], ev_w_out[j], hy_conv_w[j], hy_conv_b[j],
                            hy_fw1[j], hy_fb1[j], hy_fw2[j], hy_fb2[j], hy_fw3[j], hy_fb3[j], hy_fw4[j],
                            hy_sin_freq[j], hy_bias[j], df_q_g[j], df_k_g[j], df_lam_q1[j], df_lam_k1[j],
                            df_lam_q2[j], df_lam_k2[j], df_subln_g[j])
        else:
            x = _odd_mixer(x, norm_mix_g[layer], lbs[0, layer], lbs[1, layer], od_w_in[j], od_w_out[j],
                           hg_norm_g[j], gq_q_g[j], gq_k_g[j])
        kv = norm_matmul(mem2, norm_mem_kv_g[layer], ca_w_kv[layer].astype(BF16), tm=1024, tn=1024)
        x = cross_attention(x, kv.reshape(B, M, -1), norm_mem_q_g[layer], ca_w_q[layer].astype(BF16),
                            ca_q_g[layer], ca_k_g[layer], ca_w_out[layer].astype(BF16), tq=512)
        x = mlp_residual(x.reshape(B * L, D), norm_ffn_g[layer], mlp_w1[layer].astype(BF16),
                         mlp_w2[layer].astype(BF16), tm=512, th=1024).reshape(B, L, D)
    return x
```

```python
import functools
import math

import numpy as np
import jax
import jax.numpy as jnp
from jax import lax
from jax.experimental import pallas as pl
from jax.experimental.pallas import tpu as pltpu

F32 = jnp.float32
BF16 = jnp.bfloat16
HIGHEST = lax.Precision.HIGHEST

D_MODEL = 2048
NORM_EPS = 1e-6
GRID_W = 64
HY_WIDTH = D_MODEL // 2
HY_EMB = 33
HY_BANDS = (HY_EMB - 1) // 2
HY_FILT = 64
HY_SHORT = 3
HY_TARGET = 1e-2
HY_FAST_PCT = 0.3
HY_SLOW_PCT = 1.5
DF_HEADS = 8
DF_HEAD_DIM = 64
DF_V_DIM = 128
DF_QK_WIDTH = DF_HEADS * 2 * DF_HEAD_DIM
HG_WIDTH = D_MODEL // 2
HG_HEADS = 8
HG_DK = HG_WIDTH // HG_HEADS
GQ_WIDTH = D_MODEL - HG_WIDTH
GQ_HEADS = 8
GQ_KV_HEADS = 2
GQ_HEAD_DIM = GQ_WIDTH // GQ_HEADS
GQ_GROUP = GQ_HEADS // GQ_KV_HEADS
ROPE_THETA = 10000.0
CA_HEADS = 4
CA_HEAD_DIM = 128
CA_WIDTH = CA_HEADS * CA_HEAD_DIM

LANES = 128
SUBLANES = 8
VMEM_LIMIT_BYTES = 56 * 1024 * 1024

HG_CHUNK = 64
HG_BLOCK = SUBLANES
ATTN_STRIP = 256
HG_INTRA_CHUNKS = 8
LOG2E = 1.4426950408889634


def _params(*sem):
    return pltpu.CompilerParams(dimension_semantics=sem, vmem_limit_bytes=VMEM_LIMIT_BYTES)


def _rms(x, g):
    return x * lax.rsqrt(jnp.mean(x * x, axis=-1, keepdims=True) + NORM_EPS) * g


def _dot(a, b):
    return jnp.dot(a, b, preferred_element_type=F32)


def _dot_nt(a, b):
    return lax.dot_general(a, b, (((1,), (1,)), ((), ())), preferred_element_type=F32)


def _dot_tn(a, b):
    return lax.dot_general(a, b, (((0,), (0,)), ((), ())), preferred_element_type=F32)


def _pipeline3(n, stage_a, stage_b, stage_c):
    a = {}
    b = {}
    for t in range(n + 2):
        if t < n:
            a[t] = stage_a(t)
        if 0 <= t - 1 < n:
            b[t - 1] = stage_b(t - 1, a.pop(t - 1))
        if 0 <= t - 2 < n:
            stage_c(t - 2, b.pop(t - 2))


def _cast_kernel(w_ref, o_ref):
    o_ref[...] = w_ref[0].astype(o_ref.dtype)


def layer_weight_bf16(w, layer, *, block_bytes=8 * 1024 * 1024):
    _, K, N = w.shape
    tr = min(K, block_bytes // (4 * N))
    return pl.pallas_call(
        _cast_kernel,
        grid=(K // tr,),
        in_specs=[pl.BlockSpec((1, tr, N), lambda i: (layer, i, 0))],
        out_specs=pl.BlockSpec((tr, N), lambda i: (i, 0)),
        out_shape=jax.ShapeDtypeStruct((K, N), BF16),
        compiler_params=_params("parallel"),
        name="weight_bf16",
    )(w)


def _norm_matmul_kernel(x_ref, g_ref, w_ref, o_ref, hn_ref):
    @pl.when(pl.program_id(1) == 0)
    def _():
        hn_ref[...] = _rms(x_ref[...], g_ref[...]).astype(BF16)

    o_ref[...] = _dot(hn_ref[...], w_ref[...]).astype(o_ref.dtype)


def norm_matmul(x, g, w, *, tm, tn, out_dtype=F32):
    M, D = x.shape
    N = w.shape[1]
    tm = min(tm, M)
    tn = min(tn, N)
    return pl.pallas_call(
        _norm_matmul_kernel,
        grid=(M // tm, N // tn),
        in_specs=[pl.BlockSpec((tm, D), lambda i, j: (i, 0)),
                  pl.BlockSpec((1, D), lambda i, j: (0, 0)),
                  pl.BlockSpec((D, tn), lambda i, j: (0, j))],
        out_specs=pl.BlockSpec((tm, tn), lambda i, j: (i, j)),
        out_shape=jax.ShapeDtypeStruct((M, N), out_dtype),
        scratch_shapes=[pltpu.VMEM((tm, D), BF16)],
        compiler_params=_params("parallel", "arbitrary"),
        name="norm_matmul",
    )(x, g.reshape(1, D), w)


def _proj2_res_kernel(a1_ref, a2_ref, w1_ref, w2_ref, r_ref, o_ref):
    acc = _dot(a1_ref[...], w1_ref[...]) + _dot(a2_ref[...], w2_ref[...])
    o_ref[...] = r_ref[...] + acc


def proj2_residual(a1, a2, w, res, *, tm, tn):
    M, K1 = a1.shape
    K2 = a2.shape[1]
    assert K1 == K2
    N = w.shape[1]
    tm = min(tm, M)
    tn = min(tn, N)
    return pl.pallas_call(
        _proj2_res_kernel,
        grid=(M // tm, N // tn),
        in_specs=[pl.BlockSpec((tm, K1), lambda i, j: (i, 0)),
                  pl.BlockSpec((tm, K2), lambda i, j: (i, 0)),
                  pl.BlockSpec((K1, tn), lambda i, j: (0, j)),
                  pl.BlockSpec((K2, tn), lambda i, j: (1, j)),
                  pl.BlockSpec((tm, tn), lambda i, j: (i, j))],
        out_specs=pl.BlockSpec((tm, tn), lambda i, j: (i, j)),
        out_shape=jax.ShapeDtypeStruct((M, N), F32),
        compiler_params=_params("parallel", "arbitrary"),
        name="proj2_residual",
    )(a1, a2, w, w, res)


def _mlp_kernel(x_ref, g_ref, w1_ref, w2_ref, o_ref, hn_ref, acc_ref):
    j = pl.program_id(1)

    @pl.when(j == 0)
    def _():
        hn_ref[...] = _rms(x_ref[...], g_ref[...]).astype(BF16)
        acc_ref[...] = jnp.zeros_like(acc_ref)

    h1 = jnp.maximum(_dot(hn_ref[...], w1_ref[...]), 0.0)
    acc_ref[...] += _dot((h1 * h1).astype(BF16), w2_ref[...])

    @pl.when(j == pl.num_programs(1) - 1)
    def _():
        o_ref[...] = x_ref[...] + acc_ref[...]


def mlp_residual(x, g, w1, w2, *, tm, th):
    M, D = x.shape
    H = w1.shape[1]
    tm = min(tm, M)
    th = min(th, H)
    return pl.pallas_call(
        _mlp_kernel,
        grid=(M // tm, H // th),
        in_specs=[pl.BlockSpec((tm, D), lambda i, j: (i, 0)),
                  pl.BlockSpec((1, D), lambda i, j: (0, 0)),
                  pl.BlockSpec((D, th), lambda i, j: (0, j)),
                  pl.BlockSpec((th, D), lambda i, j: (j, 0))],
        out_specs=pl.BlockSpec((tm, D), lambda i, j: (i, 0)),
        out_shape=jax.ShapeDtypeStruct((M, D), F32),
        scratch_shapes=[pltpu.VMEM((tm, D), BF16), pltpu.VMEM((tm, D), F32)],
        compiler_params=_params("parallel", "arbitrary"),
        name="mlp_residual",
    )(x, g.reshape(1, D), w1, w2)


def _cross_attn_kernel(x_ref, g_ref, wq_ref, kv_ref, qg_ref, kg_ref, wo_ref, o_ref):
    x = x_ref[0]
    hn = _rms(x, g_ref[...]).astype(BF16)
    q = _dot(hn, wq_ref[...])
    kv = kv_ref[0]
    scale = CA_HEAD_DIM ** -0.5
    outs = []
    for h in range(CA_HEADS):
        sl = slice(h * CA_HEAD_DIM, (h + 1) * CA_HEAD_DIM)
        qh = (_rms(q[:, sl], qg_ref[...]) * scale).astype(BF16)
        kh = _rms(kv[:, sl], kg_ref[...]).astype(BF16)
        vh = kv[:, CA_WIDTH + h * CA_HEAD_DIM:CA_WIDTH + (h + 1) * CA_HEAD_DIM].astype(BF16)
        s = _dot_nt(qh, kh)
        e = jnp.exp(s - jnp.max(s, axis=-1, keepdims=True))
        p = e / jnp.sum(e, axis=-1, keepdims=True)
        outs.append(_dot(p.astype(BF16), vh))
    o = jnp.concatenate(outs, axis=-1).astype(BF16)
    o_ref[0] = x + _dot(o, wo_ref[...])


def cross_attention(x, kv, g, wq, qg, kg, wo, *, tq):
    B, L, D = x.shape
    M = kv.shape[1]
    tq = min(tq, L)
    return pl.pallas_call(
        _cross_attn_kernel,
        grid=(B, L // tq),
        in_specs=[pl.BlockSpec((1, tq, D), lambda b, i: (b, i, 0)),
                  pl.BlockSpec((1, D), lambda b, i: (0, 0)),
                  pl.BlockSpec((D, CA_WIDTH), lambda b, i: (0, 0)),
                  pl.BlockSpec((1, M, 2 * CA_WIDTH), lambda b, i: (b, 0, 0)),
                  pl.BlockSpec((1, CA_HEAD_DIM), lambda b, i: (0, 0)),
                  pl.BlockSpec((1, CA_HEAD_DIM), lambda b, i: (0, 0)),
                  pl.BlockSpec((CA_WIDTH, D), lambda b, i: (0, 0))],
        out_specs=pl.BlockSpec((1, tq, D), lambda b, i: (b, i, 0)),
        out_shape=jax.ShapeDtypeStruct((B, L, D), F32),
        compiler_params=_params("parallel", "arbitrary"),
        name="cross_attention",
    )(x, g.reshape(1, D), wq, kv, qg.reshape(1, -1), kg.reshape(1, -1), wo)


def _half_rms(x, g2, lo_mask):
    sq = x * x
    ms_lo = jnp.sum(jnp.where(lo_mask, sq, 0.0), axis=-1, keepdims=True)
    ms_hi = jnp.sum(jnp.where(lo_mask, 0.0, sq), axis=-1, keepdims=True)
    inv = jnp.where(lo_mask, lax.rsqrt(ms_lo * (1.0 / DF_HEAD_DIM) + NORM_EPS),
                    lax.rsqrt(ms_hi * (1.0 / DF_HEAD_DIM) + NORM_EPS))
    return x * inv * g2


def _diff_attn_kernel(sc_ref, q_ref, k_ref, v_ref, qg_ref, kg_ref, sg_ref, o_ref, kn_ref, vb_ref, *,
                      out_scale):
    h = pl.program_id(1)
    qi = pl.program_id(2)
    tq = q_ref.shape[1]
    L = k_ref.shape[1]
    lane = lax.broadcasted_iota(jnp.int32, (1, LANES), 1)
    lo_mask = lane < DF_HEAD_DIM

    @pl.when(qi == 0)
    def _():
        kn_ref[...] = _half_rms(k_ref[0], kg_ref[...], lo_mask).astype(BF16)
        vb_ref[:, :LANES] = v_ref[0].astype(BF16)
        vb_ref[:, LANES:] = jnp.ones((L, LANES), BF16)

    lam = sc_ref[0]
    slope = sc_ref[1 + h] * LOG2E
    ts = min(ATTN_STRIP, tq)
    col = lax.broadcasted_iota(jnp.int32, (1, L), 1).astype(F32)

    def scores(i):
        r0 = i * ts
        qn = _half_rms(q_ref[0, r0:r0 + ts, :], qg_ref[...], lo_mask) * (DF_HEAD_DIM ** -0.5 * LOG2E)
        row = (lax.broadcasted_iota(jnp.int32, (ts, 1), 0) + (qi * tq + r0)).astype(F32)
        bias = jnp.abs(row - col) * slope
        kn = kn_ref[...]
        return (_dot_nt(jnp.where(lo_mask, qn, 0.0).astype(BF16), kn) - bias,
                _dot_nt(jnp.where(lo_mask, 0.0, qn).astype(BF16), kn) - bias)

    def exps(i, ss):
        return tuple(jnp.exp2(s - jnp.max(s, axis=-1, keepdims=True)).astype(BF16) for s in ss)

    def values(i, es):
        r1, r2 = (_dot(e, vb_ref[...]) for e in es)
        o = (r1[:, :LANES] * (1.0 / r1[:, LANES:LANES + 1])
             - r2[:, :LANES] * (lam / r2[:, LANES:LANES + 1]))
        o_ref[0, i * ts:(i + 1) * ts, :] = (_rms(o, sg_ref[...]) * out_scale).astype(o_ref.dtype)

    _pipeline3(tq // ts, scores, exps, values)


def diff_attention(proj, scal, q_g, k_g, subln_g, *, q_blk0, k_blk0, v_blk0, out_scale, tq):
    B, L, _ = proj.shape
    tq = min(tq, L)
    qg2 = jnp.concatenate([q_g, q_g]).reshape(1, LANES)
    kg2 = jnp.concatenate([k_g, k_g]).reshape(1, LANES)
    return pl.pallas_call(
        functools.partial(_diff_attn_kernel, out_scale=out_scale),
        grid=(B, DF_HEADS, L // tq),
        in_specs=[pl.BlockSpec(memory_space=pltpu.SMEM),
                  pl.BlockSpec((1, tq, LANES), lambda b, h, i: (b, i, q_blk0 + h)),
                  pl.BlockSpec((1, L, LANES), lambda b, h, i: (b, 0, k_blk0 + h)),
                  pl.BlockSpec((1, L, LANES), lambda b, h, i: (b, 0, v_blk0 + h)),
                  pl.BlockSpec((1, LANES), lambda b, h, i: (0, 0)),
                  pl.BlockSpec((1, LANES), lambda b, h, i: (0, 0)),
                  pl.BlockSpec((1, LANES), lambda b, h, i: (0, 0))],
        out_specs=pl.BlockSpec((1, tq, LANES), lambda b, h, i: (b, i, h)),
        out_shape=jax.ShapeDtypeStruct((B, L, DF_HEADS * DF_V_DIM), BF16),
        scratch_shapes=[pltpu.VMEM((L, LANES), BF16), pltpu.VMEM((L, 2 * LANES), BF16)],
        compiler_params=_params("parallel", "parallel", "arbitrary"),
        name="diff_attention",
    )(scal, proj, proj, proj, qg2, kg2, subln_g.reshape(1, LANES))


def _rope(x, cos, sin, first_mask):
    q = GQ_HEAD_DIM // 4
    rot = jnp.where(first_mask, -pltpu.roll(x, LANES - q, axis=1), pltpu.roll(x, q, axis=1))
    return x * cos + rot * sin


def _gqa_kernel(q_ref, k_ref, v_ref, cq_ref, sq_ref, ck_ref, sk_ref, qg_ref, kg_ref, o_ref, kn_ref, vb_ref):
    qi = pl.program_id(2)
    lane = lax.broadcasted_iota(jnp.int32, (1, LANES), 1)
    first_mask = (lane % (GQ_HEAD_DIM // 2)) < (GQ_HEAD_DIM // 4)

    @pl.when(qi == 0)
    def _():
        kn = _rms(k_ref[0], kg_ref[...])
        kn_ref[...] = _rope(kn, ck_ref[...], sk_ref[...], first_mask).astype(BF16)
        vb_ref[:, :LANES] = v_ref[0].astype(BF16)
        vb_ref[:, LANES:] = jnp.ones((k_ref.shape[1], LANES), BF16)

    scale = GQ_HEAD_DIM ** -0.5 * LOG2E
    tq = q_ref.shape[1]
    ts = min(ATTN_STRIP, tq)
    nr = tq // ts

    def where(i):
        r0 = (i % nr) * ts
        return slice(r0, r0 + ts), slice((i // nr) * GQ_HEAD_DIM, (i // nr + 1) * GQ_HEAD_DIM)

    def scores(i):
        rs, cs = where(i)
        qn = _rope(_rms(q_ref[0, rs, cs], qg_ref[...]), cq_ref[rs, :], sq_ref[rs, :], first_mask) * scale
        return _dot_nt(qn.astype(BF16), kn_ref[...])

    def exps(i, s):
        return jnp.exp2(s - jnp.max(s, axis=-1, keepdims=True)).astype(BF16)

    def values(i, e):
        rs, cs = where(i)
        res = _dot(e, vb_ref[...])
        o_ref[0, rs, cs] = (res[:, :LANES] * (1.0 / res[:, LANES:LANES + 1])).astype(o_ref.dtype)

    _pipeline3(GQ_GROUP * nr, scores, exps, values)


def gqa_attention(proj, cos, sin, q_g, k_g, *, q_blk0, k_blk0, v_blk0, tq):
    B, L, _ = proj.shape
    tq = min(tq, L)
    gw = GQ_GROUP * GQ_HEAD_DIM
    return pl.pallas_call(
        _gqa_kernel,
        grid=(B, GQ_KV_HEADS, L // tq),
        in_specs=[pl.BlockSpec((1, tq, gw), lambda b, g, i: (b, i, q_blk0 + g)),
                  pl.BlockSpec((1, L, LANES), lambda b, g, i: (b, 0, k_blk0 + g)),
                  pl.BlockSpec((1, L, LANES), lambda b, g, i: (b, 0, v_blk0 + g)),
                  pl.BlockSpec((tq, LANES), lambda b, g, i: (i, 0)),
                  pl.BlockSpec((tq, LANES), lambda b, g, i: (i, 0)),
                  pl.BlockSpec((L, LANES), lambda b, g, i: (0, 0)),
                  pl.BlockSpec((L, LANES), lambda b, g, i: (0, 0)),
                  pl.BlockSpec((1, LANES), lambda b, g, i: (0, 0)),
                  pl.BlockSpec((1, LANES), lambda b, g, i: (0, 0))],
        out_specs=pl.BlockSpec((1, tq, gw), lambda b, g, i: (b, i, g)),
        out_shape=jax.ShapeDtypeStruct((B, L, GQ_WIDTH), BF16),
        scratch_shapes=[pltpu.VMEM((L, LANES), BF16), pltpu.VMEM((L, 2 * LANES), BF16)],
        compiler_params=_params("parallel", "parallel", "arbitrary"),
        name="gqa_attention",
    )(proj, proj, proj, cos, sin, cos, sin, q_g.reshape(1, LANES), k_g.reshape(1, LANES))


def _hgrn2_intra_streams(streams):
    C = HG_CHUNK
    nb = C // HG_BLOCK
    ns = range(len(streams))
    rev = [st[5] for st in streams]
    r_i = lax.broadcasted_iota(jnp.int32, (C, C), 0)
    c_i = lax.broadcasted_iota(jnp.int32, (C, C), 1)

    kk, lg = [], []
    for q, fz, v, vb, lb, _ in streams:
        f = lb + (1.0 - lb) * jax.nn.sigmoid(fz)
        kk.append(1.0 - f)
        lg.append(jnp.log(f) * LOG2E)

    tri_f = jnp.where(r_i >= c_i, 1.0, 0.0).astype(BF16)
    tri_b = jnp.where(r_i <= c_i, 1.0, 0.0).astype(BF16)
    b = []
    for i in ns:
        hi = lg[i].astype(BF16)
        r1 = lg[i] - hi.astype(F32)
        mid = r1.astype(BF16)
        lo = (r1 - mid.astype(F32)).astype(BF16)
        tri = tri_b if rev[i] else tri_f
        b.append(_dot(tri, hi) + _dot(tri, mid) + _dot(tri, lo))
    b_end = [b[i][0:1, :] if rev[i] else b[i][C - 1:C, :] for i in ns]
    qe = [(streams[i][0] * jnp.exp2(b[i])).astype(BF16) for i in ns]
    u = [_dot_tn(streams[i][3], (kk[i] * jnp.exp2(b_end[i] - b[i])).astype(BF16)) for i in ns]

    pos = lax.broadcasted_iota(jnp.int32, (nb, HG_BLOCK, LANES), 1)
    b3 = [b[i].reshape(nb, HG_BLOCK, LANES) for i in ns]
    q3 = [streams[i][0].reshape(nb, HG_BLOCK, LANES) for i in ns]
    seg_pos = lax.broadcasted_iota(jnp.int32, (C, HG_BLOCK * LANES), 1) // LANES
    row_pos = lax.broadcasted_iota(jnp.int32, (C, HG_BLOCK * LANES), 0) % HG_BLOCK
    kcat = []
    for i in ns:
        kb = kk[i].astype(BF16)
        kcat.append(jnp.where(seg_pos == row_pos, jnp.concatenate([kb] * HG_BLOCK, axis=1),
                              jnp.zeros((), BF16)))
    ps = [[] for i in ns]
    for s in range(HG_BLOCK):
        for i in ns:
            valid = (pos <= s) if rev[i] else (pos >= s)
            d = jnp.where(valid, b3[i] - b3[i][:, s:s + 1, :], -jnp.inf)
            ps[i].append((q3[i] * jnp.exp2(d)).reshape(C, LANES).astype(BF16))
    same_blk = (r_i // HG_BLOCK) == (c_i // HG_BLOCK)
    adiag = [_dot_nt(jnp.concatenate(ps[i], axis=1), kcat[i]) for i in ns]

    zrow = jnp.zeros((1, 1, LANES), F32)
    qt = []
    for i in ns:
        if rev[i]:
            rblk3 = jnp.concatenate([b3[i][1:, 0:1, :], zrow], axis=0)
        else:
            rblk3 = jnp.concatenate([zrow, b3[i][:nb - 1, HG_BLOCK - 1:HG_BLOCK, :]], axis=0)
        qt.append((q3[i] * jnp.exp2(b3[i] - rblk3)).reshape(C, LANES))
    rows = [[] for i in ns]
    for blk in range(nb):
        for i in ns:
            if rev[i]:
                edge = (blk + 1) * HG_BLOCK
                if edge == C:
                    rows[i].append(jnp.zeros((HG_BLOCK, C), F32))
                    continue
                kt = kk[i][edge:, :] * jnp.exp2(b[i][edge:edge + 1, :] - b[i][edge:, :])
                kfull = jnp.concatenate([jnp.zeros((edge, LANES), F32), kt], axis=0)
            else:
                edge = blk * HG_BLOCK
                if edge == 0:
                    rows[i].append(jnp.zeros((HG_BLOCK, C), F32))
                    continue
                kt = kk[i][:edge, :] * jnp.exp2(b[i][edge - 1:edge, :] - b[i][:edge, :])
                kfull = jnp.concatenate([kt, jnp.zeros((C - edge, LANES), F32)], axis=0)
            rows[i].append(_dot_nt(qt[i][blk * HG_BLOCK:(blk + 1) * HG_BLOCK, :].astype(BF16),
                                   kfull.astype(BF16)))
    outs = []
    for i in ns:
        att = jnp.where(same_blk, adiag[i], jnp.concatenate(rows[i], axis=0))
        o = _dot(att.astype(BF16), streams[i][3])
        outs.append((o, qe[i], u[i], jnp.exp2(b_end[i])))
    return outs


def _hgrn2_kernel(q_ref, ff_ref, fb_ref, i_ref, g_ref, lbf_ref, lbb_ref, ng_ref, o_ref, of_ref, ob_ref,
                  sf_ref, sb_ref):
    L = q_ref.shape[1]
    C = HG_CHUNK
    n = L // C
    gsize = math.gcd(HG_INTRA_CHUNKS, n)
    qscale = HG_DK ** -0.5

    def rows(c):
        return pl.ds(pl.multiple_of(c * C, C), C)

    sf_ref[...] = jnp.zeros_like(sf_ref)
    sb_ref[...] = jnp.zeros_like(sb_ref)

    def group(it, carry):
        streams = []
        dests = []
        for k in range(gsize):
            c = it * gsize + k
            for fz_ref, lb_ref, o_s, s_s, cc, reverse in ((ff_ref, lbf_ref, of_ref, sf_ref, c, False),
                                                          (fb_ref, lbb_ref, ob_ref, sb_ref, n - 1 - c, True)):
                r = rows(cc)
                v = i_ref[0, r, :]
                streams.append((q_ref[0, r, :] * qscale, fz_ref[0, r, :], v, v.astype(BF16), lb_ref[0], reverse))
                dests.append((o_s, s_s, r))
        state = {id(sf_ref): sf_ref[...], id(sb_ref): sb_ref[...]}
        for (o, qe, u, e), (o_s, s_s, r) in zip(_hgrn2_intra_streams(streams), dests):
            st = state[id(s_s)]
            o_s[r, :] = o + _dot_nt(qe, st.astype(BF16))
            state[id(s_s)] = st * e + u
        sf_ref[...] = state[id(sf_ref)]
        sb_ref[...] = state[id(sb_ref)]
        return carry

    lax.fori_loop(0, n // gsize, group, 0)
    o = _rms(of_ref[...] + ob_ref[...], ng_ref[...])
    g = g_ref[0]
    o_ref[0] = (o * (g * jax.nn.sigmoid(g))).astype(o_ref.dtype)


def hgrn2_mixer(proj, lb_f, lb_b, norm_g):
    B, L, _ = proj.shape
    H = HG_HEADS

    def col(k):
        return pl.BlockSpec((1, L, LANES), lambda b, h: (b, 0, k * H + h))

    lbspec = pl.BlockSpec((1, 1, LANES), lambda b, h: (h, 0, 0))
    return pl.pallas_call(
        _hgrn2_kernel,
        grid=(B, H),
        in_specs=[col(0), col(1), col(2), col(3), col(4), lbspec, lbspec,
                  pl.BlockSpec((1, LANES), lambda b, h: (0, 0))],
        out_specs=pl.BlockSpec((1, L, LANES), lambda b, h: (b, 0, h)),
        out_shape=jax.ShapeDtypeStruct((B, L, HG_WIDTH), BF16),
        scratch_shapes=[pltpu.VMEM((L, LANES), F32), pltpu.VMEM((L, LANES), F32),
                        pltpu.VMEM((LANES, LANES), F32), pltpu.VMEM((LANES, LANES), F32)],
        compiler_params=_params("parallel", "arbitrary"),
        name="hgrn2",
    )(proj, proj, proj, proj, proj, lb_f.reshape(H, 1, LANES), lb_b.reshape(H, 1, LANES),
      norm_g.reshape(1, LANES))


def _hyena_filter_kernel(z_ref, t_ref, w1_ref, b1_ref, w2_ref, b2_ref, w3_ref, b3_ref, w4f_ref, w4b_ref,
                         fr_ref, ad_ref, gs_ref, gd_ref, kn_ref):
    def hdot(a, b):
        return jnp.dot(a, b, precision=HIGHEST, preferred_element_type=F32)

    fr = fr_ref[...]
    h = jnp.sin(fr * (hdot(z_ref[...], w1_ref[...]) + b1_ref[...]))
    h = jnp.sin(fr * (hdot(h, w2_ref[...]) + b2_ref[...]))
    h = jnp.sin(fr * (hdot(h, w3_ref[...]) + b3_ref[...]))
    window = jnp.exp(-t_ref[...] * ad_ref[...])
    hf = hdot(h, w4f_ref[...]) * window
    hb = hdot(h, w4b_ref[...]) * window
    row = lax.broadcasted_iota(jnp.int32, hb.shape, 0)
    hb = jnp.where(row == 0, 0.0, hb)
    gs_ref[...] = hf + hb
    gd_ref[...] = hb - hf
    sgn = jnp.where(row % 2 == 0, 1.0, -1.0)
    kn_ref[...] = jnp.sum((hf + hb) * sgn, axis=0, keepdims=True) * (0.5 / hb.shape[0])


def _split2(x):
    hi = x.astype(BF16)
    return hi, (x - hi.astype(F32)).astype(BF16)


def _hyena_spectrum_kernel(ah_ref, al_ref, bh_ref, bl_ref, gs_ref, gd_ref, kre_ref, kim_ref, sh_ref, sl_ref,
                           dh_ref, dl_ref, *, n_fft):
    @pl.when(pl.program_id(1) == 0)
    def _():
        sh_ref[...], sl_ref[...] = _split2(gs_ref[...])
        dh_ref[...], dl_ref[...] = _split2(gd_ref[...])

    tf = ah_ref.shape[0]
    f0 = pl.program_id(1) * tf
    row = lax.broadcasted_iota(jnp.int32, (tf, 1), 0) + f0
    wf = jnp.where(row == 0, 1.0 / n_fft, 2.0 / n_fft)
    ah = ah_ref[...]
    bh = bh_ref[...]
    kre_ref[...] = wf * (_dot(ah, sh_ref[...]) + _dot(ah, sl_ref[...]) + _dot(al_ref[...], sh_ref[...]))
    kim_ref[...] = wf * (_dot(bh, dh_ref[...]) + _dot(bh, dl_ref[...]) + _dot(bl_ref[...], dh_ref[...]))


def hyena_filter_spectrum(L, cos_hi, cos_lo, sin_hi, sin_lo, w1, b1, w2, b2, w3, b3, w4, sin_freq):
    C = HY_WIDTH
    t = jnp.linspace(0.0, 1.0, L, dtype=F32)[:, None]
    w = (2.0 * math.pi / L) * jnp.arange(L, dtype=F32)[:, None]
    f = jnp.linspace(1e-4, HY_BANDS - 1, HY_BANDS, dtype=F32)[None, :]
    z = jnp.concatenate([t, jnp.cos(f * w), -jnp.sin(f * w)], axis=-1)
    zp = jnp.pad(z, ((0, 0), (0, LANES - HY_EMB)))
    w1p = jnp.pad(w1, ((0, LANES - HY_EMB), (0, 0)))
    max_decay = math.log(HY_TARGET) / HY_FAST_PCT
    min_decay = math.log(HY_TARGET) / HY_SLOW_PCT
    absd = jnp.abs(jnp.linspace(min_decay, max_decay, C, dtype=F32))[None, :]
    tc = min(512, C)
    full = lambda shape: pl.BlockSpec(shape, lambda j: (0, 0))
    gs, gd, knyq = pl.pallas_call(
        _hyena_filter_kernel,
        grid=(C // tc,),
        in_specs=[full((L, LANES)), full((L, 1)), full((LANES, HY_FILT)), full((1, HY_FILT)),
                  full((HY_FILT, HY_FILT)), full((1, HY_FILT)), full((HY_FILT, HY_FILT)), full((1, HY_FILT)),
                  pl.BlockSpec((HY_FILT, tc), lambda j: (0, j)),
                  pl.BlockSpec((HY_FILT, tc), lambda j: (0, C // tc + j)),
                  full((1, HY_FILT)),
                  pl.BlockSpec((1, tc), lambda j: (0, j))],
        out_specs=[pl.BlockSpec((L, tc), lambda j: (0, j)), pl.BlockSpec((L, tc), lambda j: (0, j)),
                   pl.BlockSpec((1, tc), lambda j: (0, j))],
        out_shape=[jax.ShapeDtypeStruct((L, C), F32), jax.ShapeDtypeStruct((L, C), F32),
                   jax.ShapeDtypeStruct((1, C), F32)],
        compiler_params=_params("arbitrary"),
        name="hyena_filter",
    )(zp, t, w1p, b1.reshape(1, -1), w2, b2.reshape(1, -1), w3, b3.reshape(1, -1), w4, w4,
      sin_freq.reshape(1, -1), absd)

    tf = min(256, L)
    tbl = pl.BlockSpec((tf, L), lambda j, i: (i, 0))
    kre, kim = pl.pallas_call(
        functools.partial(_hyena_spectrum_kernel, n_fft=2 * L),
        grid=(C // tc, L // tf),
        in_specs=[tbl, tbl, tbl, tbl,
                  pl.BlockSpec((L, tc), lambda j, i: (0, j)),
                  pl.BlockSpec((L, tc), lambda j, i: (0, j))],
        out_specs=[pl.BlockSpec((tf, tc), lambda j, i: (i, j)), pl.BlockSpec((tf, tc), lambda j, i: (i, j))],
        out_shape=[jax.ShapeDtypeStruct((L, C), F32), jax.ShapeDtypeStruct((L, C), F32)],
        scratch_shapes=[pltpu.VMEM((L, tc), BF16)] * 4,
        compiler_params=_params("parallel", "arbitrary"),
        name="hyena_spectrum",
    )(cos_hi, cos_lo, sin_hi, sin_lo, gs, gd)
    return kre, kim, knyq


def _short_conv(u, w_ref, b_ref):
    L = u.shape[0]
    row = lax.broadcasted_iota(jnp.int32, (L, 1), 0)
    prev = jnp.where(row == 0, 0.0, pltpu.roll(u, 1, axis=0))
    nxt = jnp.where(row == L - 1, 0.0, pltpu.roll(u, L - 1, axis=0))
    return prev * w_ref[0:1, :] + u * w_ref[1:2, :] + nxt * w_ref[2:3, :] + b_ref[...]


def _hyena_conv_kernel(x1_ref, x2_ref, v_ref, cw1_ref, cw2_ref, cw3_ref, cb1_ref, cb2_ref, cb3_ref,
                       a_ref, b_ref, kre_ref, kim_ref, knyq_ref, bias_ref, o_ref, r1_ref, r2_ref, *, tf):
    L = v_ref.shape[1]
    x2 = _short_conv(x2_ref[0], cw2_ref, cb2_ref)
    vv = _short_conv(v_ref[0], cw3_ref, cb3_ref)
    vx = vv * x2
    vxb = vx.astype(BF16)
    sgn = jnp.where(lax.broadcasted_iota(jnp.int32, (L, 1), 0) % 2 == 0, 1.0, -1.0)
    def forward(fb):
        fs = slice(fb * tf, (fb + 1) * tf)
        return _dot(a_ref[fs, :], vxb), _dot(b_ref[fs, :], vxb)

    def product(fb, pq):
        p, q = pq
        fs = slice(fb * tf, (fb + 1) * tf)
        kre = kre_ref[fs, :]
        kim = kim_ref[fs, :]
        r1_ref[fs, :] = (p * kre + q * kim).astype(BF16)
        r2_ref[fs, :] = (q * kre - p * kim).astype(BF16)

    _pipeline3(L // tf, forward, product, lambda fb, _: None)

    v_nyq = jnp.sum(vx * sgn, axis=0, keepdims=True)
    y0 = vx * bias_ref[...] + sgn * (v_nyq * knyq_ref[...])
    x1 = _short_conv(x1_ref[0], cw1_ref, cb1_ref)
    y = _dot(a_ref[...], r1_ref[...]) + _dot(b_ref[...], r2_ref[...])
    o_ref[0] = ((y0 + y) * x1).astype(o_ref.dtype)


def hyena_conv(proj, conv_w, conv_b, cos_b, sin_b, kre, kim, knyq, bias, *, tc):
    B, L, _ = proj.shape
    C = HY_WIDTH
    tc = min(tc, C)
    nc = C // tc
    tf = min(512, L)

    def grp(k):
        return pl.BlockSpec((1, L, tc), lambda c, b: (b, 0, k * nc + c))

    def cw(k):
        return pl.BlockSpec((HY_SHORT, tc), lambda c, b: (0, k * nc + c))

    def cb(k):
        return pl.BlockSpec((1, tc), lambda c, b: (0, k * nc + c))

    tbl = pl.BlockSpec((L, L), lambda c, b: (0, 0), pipeline_mode=pl.Buffered(1))
    chan = pl.BlockSpec((L, tc), lambda c, b: (0, c), pipeline_mode=pl.Buffered(1))
    vec = pl.BlockSpec((1, tc), lambda c, b: (0, c))
    return pl.pallas_call(
        functools.partial(_hyena_conv_kernel, tf=tf),
        grid=(nc, B),
        in_specs=[grp(0), grp(1), grp(2), cw(0), cw(1), cw(2), cb(0), cb(1), cb(2),
                  tbl, tbl, chan, chan, vec, vec],
        out_specs=pl.BlockSpec((1, L, tc), lambda c, b: (b, 0, c)),
        out_shape=jax.ShapeDtypeStruct((B, L, C), BF16),
        scratch_shapes=[pltpu.VMEM((L, tc), BF16), pltpu.VMEM((L, tc), BF16)],
        compiler_params=_params("parallel", "arbitrary"),
        name="hyena_conv",
    )(proj, proj, proj, conv_w, conv_w, conv_w, conv_b.reshape(1, -1), conv_b.reshape(1, -1),
      conv_b.reshape(1, -1), cos_b, sin_b, kre, kim, knyq, bias.reshape(1, -1))


def _dft_tables(L):
    n = 2 * L
    r = math.gcd(GRID_W, L)
    j = jnp.arange(L, dtype=jnp.int32)[None, :]

    def cs(f):
        ang = ((f[:, None] * j) % n).astype(F32) * (2.0 * math.pi / n)
        return jnp.cos(ang), jnp.sin(ang)

    c0, s0 = cs(jnp.arange(r, dtype=jnp.int32))
    c1, s1 = cs(jnp.arange(L // r, dtype=jnp.int32) * r)
    cos_m = (c1[:, None, :] * c0[None, :, :] - s1[:, None, :] * s0[None, :, :]).reshape(L, L)
    sin_m = (s1[:, None, :] * c0[None, :, :] + c1[:, None, :] * s0[None, :, :]).reshape(L, L)
    cos_hi = cos_m.astype(BF16)
    sin_hi = sin_m.astype(BF16)
    return (cos_hi, (cos_m - cos_hi.astype(F32)).astype(BF16),
            sin_hi, (sin_m - sin_hi.astype(F32)).astype(BF16))


def _axial_rope_tables(L):
    rows = L // GRID_W
    r, c = jnp.meshgrid(jnp.arange(rows, dtype=F32), jnp.arange(GRID_W, dtype=F32), indexing='ij')
    r = r.reshape(-1)
    c = c.reshape(-1)
    half = GQ_HEAD_DIM // 2
    inv = ROPE_THETA ** (-jnp.arange(0, half, 2, dtype=F32) / half)
    ang_r = r[:, None] * inv[None, :]
    ang_c = c[:, None] * inv[None, :]
    ang = jnp.concatenate([ang_r, ang_r, ang_c, ang_c], axis=-1)
    return jnp.cos(ang), jnp.sin(ang)


def _even_mixer(x, layer, norm_g, w_in, w_out, conv_w, conv_b, fw1, fb1, fw2, fb2, fw3, fb3, fw4, sin_freq,
                hy_bias, q_g, k_g, lam_q1, lam_k1, lam_q2, lam_k2, subln_g):
    B, L, D = x.shape
    x2 = x.reshape(B * L, D)
    proj = norm_matmul(x2, norm_g, w_in.astype(BF16), tm=1024, tn=1024).reshape(B, L, -1)
    cos_hi, cos_lo, sin_hi, sin_lo = _dft_tables(L)
    kre, kim, knyq = hyena_filter_spectrum(L, cos_hi, cos_lo, sin_hi, sin_lo, fw1, fb1, fw2, fb2, fw3, fb3, fw4,
                                           sin_freq)
    o_a = hyena_conv(proj, conv_w, conv_b, cos_hi, sin_hi, kre, kim, knyq, hy_bias,
                     tc=256)
    lam_init = 0.8 - 0.6 * math.exp(-0.3 * layer)
    lam = (jnp.exp(jnp.sum(lam_q1.astype(F32) * lam_k1.astype(F32)))
           - jnp.exp(jnp.sum(lam_q2.astype(F32) * lam_k2.astype(F32))) + lam_init)
    slopes = jnp.asarray(np.array([2.0 ** (-8.0 * (h + 1) / DF_HEADS) for h in range(DF_HEADS)],
                                  dtype=np.float32))
    scal = jnp.concatenate([lam.reshape(1), slopes]).astype(F32)
    nb_hy = 3 * HY_WIDTH // LANES
    nb_qk = DF_QK_WIDTH // LANES
    o_b = diff_attention(proj, scal, q_g, k_g, subln_g, q_blk0=nb_hy, k_blk0=nb_hy + nb_qk,
                         v_blk0=nb_hy + 2 * nb_qk, out_scale=1.0 - lam_init, tq=1024)
    out = proj2_residual(o_a.reshape(B * L, -1), o_b.reshape(B * L, -1), w_out.astype(BF16), x2,
                         tm=1024, tn=1024)
    return out.reshape(B, L, D)


def _odd_mixer(x, norm_g, lb_f, lb_b, w_in, w_out, hg_norm_g, q_g, k_g):
    B, L, D = x.shape
    x2 = x.reshape(B * L, D)
    proj = norm_matmul(x2, norm_g, w_in.astype(BF16), tm=1024, tn=1664).reshape(B, L, -1)
    o_c = hgrn2_mixer(proj, lb_f, lb_b, hg_norm_g)
    cos, sin = _axial_rope_tables(L)
    gw = GQ_GROUP * GQ_HEAD_DIM
    q0 = 5 * HG_WIDTH
    k0 = q0 + GQ_WIDTH
    v0 = k0 + GQ_KV_HEADS * GQ_HEAD_DIM
    o_d = gqa_attention(proj, cos, sin, q_g, k_g, q_blk0=q0 // gw, k_blk0=k0 // LANES, v_blk0=v0 // LANES,
                        tq=512)
    out = proj2_residual(o_c.reshape(B * L, -1), o_d.reshape(B * L, -1), w_out.astype(BF16), x2,
                         tm=1024, tn=1024)
    return out.reshape(B, L, D)


def kernel(x, mem, norm_mix_g, norm_mem_q_g, norm_mem_kv_g, norm_ffn_g, ev_w_in, ev_w_out, hy_conv_w, hy_conv_b, hy_fw1, hy_fb1, hy_fw2, hy_fb2, hy_fw3, hy_fb3, hy_fw4, hy_sin_freq, hy_bias, df_q_g, df_k_g, df_lam_q1, df_lam_k1, df_lam_q2, df_lam_k2, df_subln_g, od_w_in, od_w_out, hg_lb_logits, hg_norm_g, gq_q_g, gq_k_g, ca_w_q, ca_w_kv, ca_w_out, ca_q_g, ca_k_g, mlp_w1, mlp_w2):
    B, L, D = x.shape
    M = mem.shape[1]
    depth = norm_mix_g.shape[0]
    lb_soft = jax.nn.softmax(hg_lb_logits.astype(F32), axis=1)
    lbs = jnp.cumsum(lb_soft, axis=1)
    lbs = lbs - lbs[:, :1]
    mem2 = mem.reshape(B * M, D)
    for layer in range(depth):
        j = layer // 2
        if layer % 2 == 0:
            x = _even_mixer(x, layer, norm_mix_g[layer], ev_w_in[j], ev_w_out[j], hy_conv_w[j], hy_conv_b[j],
                            hy_fw1[j], hy_fb1[j], hy_fw2[j], hy_fb2[j], hy_fw3[j], hy_fb3[j], hy_fw4[j],
                            hy_sin_freq[j], hy_bias[j], df_q_g[j], df_k_g[j], df_lam_q1[j], df_lam_k1[j],
                            df_lam_q2[j], df_lam_k2[j], df_subln_g[j])
        else:
            x = _odd_mixer(x, norm_mix_g[layer], lbs[0, layer], lbs[1, layer], od_w_in[j], od_w_out[j],
                           hg_norm_g[j], gq_q_g[j], gq_k_g[j])
        kv = norm_matmul(mem2, norm_mem_kv_g[layer], ca_w_kv[layer].astype(BF16), tm=1024, tn=1024)
        x = cross_attention(x, kv.reshape(B, M, -1), norm_mem_q_g[layer], ca_w_q[layer].astype(BF16),
                            ca_q_g[layer], ca_k_g[layer], ca_w_out[layer].astype(BF16), tq=512)
        x = mlp_residual(x.reshape(B * L, D), norm_ffn_g[layer], layer_weight_bf16(mlp_w1, layer),
                         layer_weight_bf16(mlp_w2, layer), tm=512, th=1024).reshape(B, L, D)
    return x
```

```python
import functools
import math

import numpy as np
import jax
import jax.numpy as jnp
from jax import lax
from jax.experimental import pallas as pl
from jax.experimental.pallas import tpu as pltpu

F32 = jnp.float32
BF16 = jnp.bfloat16
HIGHEST = lax.Precision.HIGHEST

D_MODEL = 2048
NORM_EPS = 1e-6
GRID_W = 64
HY_WIDTH = D_MODEL // 2
HY_EMB = 33
HY_BANDS = (HY_EMB - 1) // 2
HY_FILT = 64
HY_SHORT = 3
HY_TARGET = 1e-2
HY_FAST_PCT = 0.3
HY_SLOW_PCT = 1.5
DF_HEADS = 8
DF_HEAD_DIM = 64
DF_V_DIM = 128
DF_QK_WIDTH = DF_HEADS * 2 * DF_HEAD_DIM
HG_WIDTH = D_MODEL // 2
HG_HEADS = 8
HG_DK = HG_WIDTH // HG_HEADS
GQ_WIDTH = D_MODEL - HG_WIDTH
GQ_HEADS = 8
GQ_KV_HEADS = 2
GQ_HEAD_DIM = GQ_WIDTH // GQ_HEADS
GQ_GROUP = GQ_HEADS // GQ_KV_HEADS
ROPE_THETA = 10000.0
CA_HEADS = 4
CA_HEAD_DIM = 128
CA_WIDTH = CA_HEADS * CA_HEAD_DIM

LANES = 128
SUBLANES = 8
VMEM_LIMIT_BYTES = 56 * 1024 * 1024

HG_CHUNK = 64
HG_BLOCK = SUBLANES
ATTN_STRIP = 128
HG_INTRA_CHUNKS = 8
LOG2E = 1.4426950408889634


def _params(*sem):
    return pltpu.CompilerParams(dimension_semantics=sem, vmem_limit_bytes=VMEM_LIMIT_BYTES)


def _rms(x, g):
    return x * lax.rsqrt(jnp.mean(x * x, axis=-1, keepdims=True) + NORM_EPS) * g


def _dot(a, b):
    return jnp.dot(a, b, preferred_element_type=F32)


def _dot_nt(a, b):
    return lax.dot_general(a, b, (((1,), (1,)), ((), ())), preferred_element_type=F32)


def _dot_tn(a, b):
    return lax.dot_general(a, b, (((0,), (0,)), ((), ())), preferred_element_type=F32)


def _pipeline3(n, stage_a, stage_b, stage_c):
    a = {}
    b = {}
    for t in range(n + 2):
        if t < n:
            a[t] = stage_a(t)
        if 0 <= t - 1 < n:
            b[t - 1] = stage_b(t - 1, a.pop(t - 1))
        if 0 <= t - 2 < n:
            stage_c(t - 2, b.pop(t - 2))


def _cast_kernel(w_ref, o_ref):
    o_ref[...] = w_ref[0].astype(o_ref.dtype)


def layer_weight_bf16(w, layer, *, block_bytes=8 * 1024 * 1024):
    _, K, N = w.shape
    tr = min(K, block_bytes // (4 * N))
    return pl.pallas_call(
        _cast_kernel,
        grid=(K // tr,),
        in_specs=[pl.BlockSpec((1, tr, N), lambda i: (layer, i, 0))],
        out_specs=pl.BlockSpec((tr, N), lambda i: (i, 0)),
        out_shape=jax.ShapeDtypeStruct((K, N), BF16),
        compiler_params=_params("parallel"),
        name="weight_bf16",
    )(w)


def _norm_matmul_kernel(x_ref, g_ref, w_ref, o_ref, hn_ref):
    @pl.when(pl.program_id(1) == 0)
    def _():
        hn_ref[...] = _rms(x_ref[...], g_ref[...]).astype(BF16)

    o_ref[...] = _dot(hn_ref[...], w_ref[...]).astype(o_ref.dtype)


def norm_matmul(x, g, w, *, tm, tn, out_dtype=F32):
    M, D = x.shape
    N = w.shape[1]
    tm = min(tm, M)
    tn = min(tn, N)
    return pl.pallas_call(
        _norm_matmul_kernel,
        grid=(M // tm, N // tn),
        in_specs=[pl.BlockSpec((tm, D), lambda i, j: (i, 0)),
                  pl.BlockSpec((1, D), lambda i, j: (0, 0)),
                  pl.BlockSpec((D, tn), lambda i, j: (0, j))],
        out_specs=pl.BlockSpec((tm, tn), lambda i, j: (i, j)),
        out_shape=jax.ShapeDtypeStruct((M, N), out_dtype),
        scratch_shapes=[pltpu.VMEM((tm, D), BF16)],
        compiler_params=_params("parallel", "arbitrary"),
        name="norm_matmul",
    )(x, g.reshape(1, D), w)


def _proj2_res_kernel(a1_ref, a2_ref, w1_ref, w2_ref, r_ref, o_ref):
    acc = _dot(a1_ref[...], w1_ref[...]) + _dot(a2_ref[...], w2_ref[...])
    o_ref[...] = r_ref[...] + acc


def proj2_residual(a1, a2, w, res, *, tm, tn):
    M, K1 = a1.shape
    K2 = a2.shape[1]
    assert K1 == K2
    N = w.shape[1]
    tm = min(tm, M)
    tn = min(tn, N)
    return pl.pallas_call(
        _proj2_res_kernel,
        grid=(M // tm, N // tn),
        in_specs=[pl.BlockSpec((tm, K1), lambda i, j: (i, 0)),
                  pl.BlockSpec((tm, K2), lambda i, j: (i, 0)),
                  pl.BlockSpec((K1, tn), lambda i, j: (0, j)),
                  pl.BlockSpec((K2, tn), lambda i, j: (1, j)),
                  pl.BlockSpec((tm, tn), lambda i, j: (i, j))],
        out_specs=pl.BlockSpec((tm, tn), lambda i, j: (i, j)),
        out_shape=jax.ShapeDtypeStruct((M, N), F32),
        compiler_params=_params("parallel", "arbitrary"),
        name="proj2_residual",
    )(a1, a2, w, w, res)


def _mlp_kernel(x_ref, g_ref, w1_ref, w2_ref, o_ref, hn_ref, acc_ref):
    j = pl.program_id(1)

    last = pl.num_programs(1) - 1

    @pl.when(j == 0)
    def _():
        x = x_ref[...]
        hn_ref[...] = _rms(x, g_ref[...]).astype(BF16)
        acc_ref[...] = x

    def accumulated():
        h1 = jnp.maximum(_dot(hn_ref[...], w1_ref[...]), 0.0)
        return acc_ref[...] + _dot((h1 * h1).astype(BF16), w2_ref[...])

    @pl.when(j != last)
    def _():
        acc_ref[...] = accumulated()

    @pl.when(j == last)
    def _():
        o_ref[...] = accumulated()


def mlp_residual(x, g, w1, w2, *, tm, th):
    M, D = x.shape
    H = w1.shape[1]
    tm = min(tm, M)
    th = min(th, H)
    return pl.pallas_call(
        _mlp_kernel,
        grid=(M // tm, H // th),
        in_specs=[pl.BlockSpec((tm, D), lambda i, j: (i, 0)),
                  pl.BlockSpec((1, D), lambda i, j: (0, 0)),
                  pl.BlockSpec((D, th), lambda i, j: (0, j)),
                  pl.BlockSpec((th, D), lambda i, j: (j, 0))],
        out_specs=pl.BlockSpec((tm, D), lambda i, j: (i, 0)),
        out_shape=jax.ShapeDtypeStruct((M, D), F32),
        scratch_shapes=[pltpu.VMEM((tm, D), BF16), pltpu.VMEM((tm, D), F32)],
        compiler_params=_params("parallel", "arbitrary"),
        name="mlp_residual",
    )(x, g.reshape(1, D), w1, w2)


def _cross_attn_kernel(x_ref, g_ref, wq_ref, kv_ref, qg_ref, kg_ref, wo_ref, o_ref):
    x = x_ref[0]
    hn = _rms(x, g_ref[...]).astype(BF16)
    q = _dot(hn, wq_ref[...])
    kv = kv_ref[0]
    scale = CA_HEAD_DIM ** -0.5 * LOG2E
    ones = jnp.ones((kv.shape[0], CA_HEAD_DIM), BF16)
    outs = []
    for h in range(CA_HEADS):
        sl = slice(h * CA_HEAD_DIM, (h + 1) * CA_HEAD_DIM)
        qh = (_rms(q[:, sl], qg_ref[...]) * scale).astype(BF16)
        kh = _rms(kv[:, sl], kg_ref[...]).astype(BF16)
        vh = kv[:, CA_WIDTH + h * CA_HEAD_DIM:CA_WIDTH + (h + 1) * CA_HEAD_DIM].astype(BF16)
        s = _dot_nt(qh, kh)
        e = jnp.exp2(s - jnp.max(s, axis=-1, keepdims=True)).astype(BF16)
        res = _dot(e, jnp.concatenate([vh, ones], axis=1))
        outs.append(res[:, :CA_HEAD_DIM] * (1.0 / res[:, CA_HEAD_DIM:CA_HEAD_DIM + 1]))
    o = jnp.concatenate(outs, axis=-1).astype(BF16)
    o_ref[0] = x + _dot(o, wo_ref[...])


def cross_attention(x, kv, g, wq, qg, kg, wo, *, tq):
    B, L, D = x.shape
    M = kv.shape[1]
    tq = min(tq, L)
    return pl.pallas_call(
        _cross_attn_kernel,
        grid=(B, L // tq),
        in_specs=[pl.BlockSpec((1, tq, D), lambda b, i: (b, i, 0)),
                  pl.BlockSpec((1, D), lambda b, i: (0, 0)),
                  pl.BlockSpec((D, CA_WIDTH), lambda b, i: (0, 0)),
                  pl.BlockSpec((1, M, 2 * CA_WIDTH), lambda b, i: (b, 0, 0)),
                  pl.BlockSpec((1, CA_HEAD_DIM), lambda b, i: (0, 0)),
                  pl.BlockSpec((1, CA_HEAD_DIM), lambda b, i: (0, 0)),
                  pl.BlockSpec((CA_WIDTH, D), lambda b, i: (0, 0))],
        out_specs=pl.BlockSpec((1, tq, D), lambda b, i: (b, i, 0)),
        out_shape=jax.ShapeDtypeStruct((B, L, D), F32),
        compiler_params=_params("parallel", "arbitrary"),
        name="cross_attention",
    )(x, g.reshape(1, D), wq, kv, qg.reshape(1, -1), kg.reshape(1, -1), wo)


def _half_rms(x, g2, lo_mask):
    sq = x * x
    ms_lo = jnp.sum(jnp.where(lo_mask, sq, 0.0), axis=-1, keepdims=True)
    ms_hi = jnp.sum(jnp.where(lo_mask, 0.0, sq), axis=-1, keepdims=True)
    inv = jnp.where(lo_mask, lax.rsqrt(ms_lo * (1.0 / DF_HEAD_DIM) + NORM_EPS),
                    lax.rsqrt(ms_hi * (1.0 / DF_HEAD_DIM) + NORM_EPS))
    return x * inv * g2


def _diff_attn_kernel(sc_ref, q_ref, k_ref, v_ref, qg_ref, kg_ref, sg_ref, o_ref, kn_ref, vb_ref, *,
                      out_scale):
    h = pl.program_id(1)
    qi = pl.program_id(2)
    tq = q_ref.shape[1]
    L = k_ref.shape[1]
    lane = lax.broadcasted_iota(jnp.int32, (1, LANES), 1)
    lo_mask = lane < DF_HEAD_DIM

    @pl.when(qi == 0)
    def _():
        kn_ref[...] = _half_rms(k_ref[0], kg_ref[...], lo_mask).astype(BF16)
        vb_ref[:, :LANES] = v_ref[0].astype(BF16)
        vb_ref[:, LANES:] = jnp.ones((L, LANES), BF16)

    lam = sc_ref[0]
    slope = sc_ref[1 + h] * LOG2E
    ts = min(ATTN_STRIP, tq)
    col = lax.broadcasted_iota(jnp.int32, (1, L), 1).astype(F32)

    def scores(i):
        r0 = i * ts
        qn = _half_rms(q_ref[0, r0:r0 + ts, :], qg_ref[...], lo_mask) * (DF_HEAD_DIM ** -0.5 * LOG2E)
        row = (lax.broadcasted_iota(jnp.int32, (ts, 1), 0) + (qi * tq + r0)).astype(F32)
        bias = jnp.abs(row - col) * slope
        kn = kn_ref[...]
        return (_dot_nt(jnp.where(lo_mask, qn, 0.0).astype(BF16), kn) - bias,
                _dot_nt(jnp.where(lo_mask, 0.0, qn).astype(BF16), kn) - bias)

    def exps(i, ss):
        return tuple(jnp.exp2(s - jnp.max(s, axis=-1, keepdims=True)).astype(BF16) for s in ss)

    def values(i, es):
        r1, r2 = (_dot(e, vb_ref[...]) for e in es)
        o = (r1[:, :LANES] * (1.0 / r1[:, LANES:LANES + 1])
             - r2[:, :LANES] * (lam / r2[:, LANES:LANES + 1]))
        o_ref[0, i * ts:(i + 1) * ts, :] = (_rms(o, sg_ref[...]) * out_scale).astype(o_ref.dtype)

    _pipeline3(tq // ts, scores, exps, values)


def diff_attention(proj, scal, q_g, k_g, subln_g, *, q_blk0, k_blk0, v_blk0, out_scale, tq):
    B, L, _ = proj.shape
    tq = min(tq, L)
    qg2 = jnp.concatenate([q_g, q_g]).reshape(1, LANES)
    kg2 = jnp.concatenate([k_g, k_g]).reshape(1, LANES)
    return pl.pallas_call(
        functools.partial(_diff_attn_kernel, out_scale=out_scale),
        grid=(B, DF_HEADS, L // tq),
        in_specs=[pl.BlockSpec(memory_space=pltpu.SMEM),
                  pl.BlockSpec((1, tq, LANES), lambda b, h, i: (b, i, q_blk0 + h)),
                  pl.BlockSpec((1, L, LANES), lambda b, h, i: (b, 0, k_blk0 + h)),
                  pl.BlockSpec((1, L, LANES), lambda b, h, i: (b, 0, v_blk0 + h)),
                  pl.BlockSpec((1, LANES), lambda b, h, i: (0, 0)),
                  pl.BlockSpec((1, LANES), lambda b, h, i: (0, 0)),
                  pl.BlockSpec((1, LANES), lambda b, h, i: (0, 0))],
        out_specs=pl.BlockSpec((1, tq, LANES), lambda b, h, i: (b, i, h)),
        out_shape=jax.ShapeDtypeStruct((B, L, DF_HEADS * DF_V_DIM), BF16),
        scratch_shapes=[pltpu.VMEM((L, LANES), BF16), pltpu.VMEM((L, 2 * LANES), BF16)],
        compiler_params=_params("parallel", "parallel", "arbitrary"),
        name="diff_attention",
    )(scal, proj, proj, proj, qg2, kg2, subln_g.reshape(1, LANES))


def _rope(x, cos, sin, first_mask):
    q = GQ_HEAD_DIM // 4
    rot = jnp.where(first_mask, -pltpu.roll(x, LANES - q, axis=1), pltpu.roll(x, q, axis=1))
    return x * cos + rot * sin


def _gqa_kernel(q_ref, k_ref, v_ref, cq_ref, sq_ref, ck_ref, sk_ref, qg_ref, kg_ref, o_ref, kn_ref, vb_ref):
    qi = pl.program_id(2)
    lane = lax.broadcasted_iota(jnp.int32, (1, LANES), 1)
    first_mask = (lane % (GQ_HEAD_DIM // 2)) < (GQ_HEAD_DIM // 4)

    @pl.when(qi == 0)
    def _():
        kn = _rms(k_ref[0], kg_ref[...])
        kn_ref[...] = _rope(kn, ck_ref[...], sk_ref[...], first_mask).astype(BF16)
        vb_ref[:, :LANES] = v_ref[0].astype(BF16)
        vb_ref[:, LANES:] = jnp.ones((k_ref.shape[1], LANES), BF16)

    scale = GQ_HEAD_DIM ** -0.5 * LOG2E
    tq = q_ref.shape[1]
    ts = min(ATTN_STRIP, tq)
    nr = tq // ts

    def where(i):
        r0 = (i % nr) * ts
        return slice(r0, r0 + ts), slice((i // nr) * GQ_HEAD_DIM, (i // nr + 1) * GQ_HEAD_DIM)

    def scores(i):
        rs, cs = where(i)
        qn = _rope(_rms(q_ref[0, rs, cs], qg_ref[...]), cq_ref[rs, :], sq_ref[rs, :], first_mask) * scale
        return _dot_nt(qn.astype(BF16), kn_ref[...])

    def exps(i, s):
        return jnp.exp2(s - jnp.max(s, axis=-1, keepdims=True)).astype(BF16)

    def values(i, e):
        rs, cs = where(i)
        res = _dot(e, vb_ref[...])
        o_ref[0, rs, cs] = (res[:, :LANES] * (1.0 / res[:, LANES:LANES + 1])).astype(o_ref.dtype)

    _pipeline3(GQ_GROUP * nr, scores, exps, values)


def gqa_attention(proj, cos, sin, q_g, k_g, *, q_blk0, k_blk0, v_blk0, tq):
    B, L, _ = proj.shape
    tq = min(tq, L)
    gw = GQ_GROUP * GQ_HEAD_DIM
    return pl.pallas_call(
        _gqa_kernel,
        grid=(B, GQ_KV_HEADS, L // tq),
        in_specs=[pl.BlockSpec((1, tq, gw), lambda b, g, i: (b, i, q_blk0 + g)),
                  pl.BlockSpec((1, L, LANES), lambda b, g, i: (b, 0, k_blk0 + g)),
                  pl.BlockSpec((1, L, LANES), lambda b, g, i: (b, 0, v_blk0 + g)),
                  pl.BlockSpec((tq, LANES), lambda b, g, i: (i, 0)),
                  pl.BlockSpec((tq, LANES), lambda b, g, i: (i, 0)),
                  pl.BlockSpec((L, LANES), lambda b, g, i: (0, 0)),
                  pl.BlockSpec((L, LANES), lambda b, g, i: (0, 0)),
                  pl.BlockSpec((1, LANES), lambda b, g, i: (0, 0)),
                  pl.BlockSpec((1, LANES), lambda b, g, i: (0, 0))],
        out_specs=pl.BlockSpec((1, tq, gw), lambda b, g, i: (b, i, g)),
        out_shape=jax.ShapeDtypeStruct((B, L, GQ_WIDTH), BF16),
        scratch_shapes=[pltpu.VMEM((L, LANES), BF16), pltpu.VMEM((L, 2 * LANES), BF16)],
        compiler_params=_params("parallel", "parallel", "arbitrary"),
        name="gqa_attention",
    )(proj, proj, proj, cos, sin, cos, sin, q_g.reshape(1, LANES), k_g.reshape(1, LANES))


def _hgrn2_intra_streams(streams):
    C = HG_CHUNK
    nb = C // HG_BLOCK
    ns = range(len(streams))
    rev = [st[5] for st in streams]
    r_i = lax.broadcasted_iota(jnp.int32, (C, C), 0)
    c_i = lax.broadcasted_iota(jnp.int32, (C, C), 1)

    kk, lg = [], []
    for q, fz, v, vb, lb, _ in streams:
        f = lb + (1.0 - lb) * jax.nn.sigmoid(fz)
        kk.append(1.0 - f)
        lg.append(jnp.log(f) * LOG2E)

    tri_f = jnp.where(r_i >= c_i, 1.0, 0.0).astype(BF16)
    tri_b = jnp.where(r_i <= c_i, 1.0, 0.0).astype(BF16)
    b = []
    for i in ns:
        hi = lg[i].astype(BF16)
        r1 = lg[i] - hi.astype(F32)
        mid = r1.astype(BF16)
        lo = (r1 - mid.astype(F32)).astype(BF16)
        tri = tri_b if rev[i] else tri_f
        b.append(_dot(tri, hi) + _dot(tri, mid) + _dot(tri, lo))
    b_end = [b[i][0:1, :] if rev[i] else b[i][C - 1:C, :] for i in ns]
    qe = [(streams[i][0] * jnp.exp2(b[i])).astype(BF16) for i in ns]
    u = [_dot_tn(streams[i][3], (kk[i] * jnp.exp2(b_end[i] - b[i])).astype(BF16)) for i in ns]

    pos = lax.broadcasted_iota(jnp.int32, (nb, HG_BLOCK, LANES), 1)
    b3 = [b[i].reshape(nb, HG_BLOCK, LANES) for i in ns]
    q3 = [streams[i][0].reshape(nb, HG_BLOCK, LANES) for i in ns]
    seg_pos = lax.broadcasted_iota(jnp.int32, (C, HG_BLOCK * LANES), 1) // LANES
    row_pos = lax.broadcasted_iota(jnp.int32, (C, HG_BLOCK * LANES), 0) % HG_BLOCK
    kcat = []
    for i in ns:
        kb = kk[i].astype(BF16)
        kcat.append(jnp.where(seg_pos == row_pos, jnp.concatenate([kb] * HG_BLOCK, axis=1),
                              jnp.zeros((), BF16)))
    ps = [[] for i in ns]
    for s in range(HG_BLOCK):
        for i in ns:
            valid = (pos <= s) if rev[i] else (pos >= s)
            d = jnp.where(valid, b3[i] - b3[i][:, s:s + 1, :], -jnp.inf)
            ps[i].append((q3[i] * jnp.exp2(d)).reshape(C, LANES).astype(BF16))
    same_blk = (r_i // HG_BLOCK) == (c_i // HG_BLOCK)
    adiag = [_dot_nt(jnp.concatenate(ps[i], axis=1), kcat[i]) for i in ns]

    zrow = jnp.zeros((1, 1, LANES), F32)
    qt = []
    for i in ns:
        if rev[i]:
            rblk3 = jnp.concatenate([b3[i][1:, 0:1, :], zrow], axis=0)
        else:
            rblk3 = jnp.concatenate([zrow, b3[i][:nb - 1, HG_BLOCK - 1:HG_BLOCK, :]], axis=0)
        qt.append((q3[i] * jnp.exp2(b3[i] - rblk3)).reshape(C, LANES))
    rows = [[] for i in ns]
    for blk in range(nb):
        for i in ns:
            if rev[i]:
                edge = (blk + 1) * HG_BLOCK
                if edge == C:
                    rows[i].append(jnp.zeros((HG_BLOCK, C), F32))
                    continue
                kt = kk[i][edge:, :] * jnp.exp2(b[i][edge:edge + 1, :] - b[i][edge:, :])
                kfull = jnp.concatenate([jnp.zeros((edge, LANES), F32), kt], axis=0)
            else:
                edge = blk * HG_BLOCK
                if edge == 0:
                    rows[i].append(jnp.zeros((HG_BLOCK, C), F32))
                    continue
                kt = kk[i][:edge, :] * jnp.exp2(b[i][edge - 1:edge, :] - b[i][:edge, :])
                kfull = jnp.concatenate([kt, jnp.zeros((C - edge, LANES), F32)], axis=0)
            rows[i].append(_dot_nt(qt[i][blk * HG_BLOCK:(blk + 1) * HG_BLOCK, :].astype(BF16),
                                   kfull.astype(BF16)))
    outs = []
    for i in ns:
        att = jnp.where(same_blk, adiag[i], jnp.concatenate(rows[i], axis=0))
        o = _dot(att.astype(BF16), streams[i][3])
        outs.append((o, qe[i], u[i], jnp.exp2(b_end[i])))
    return outs


def _hgrn2_kernel(q_ref, ff_ref, fb_ref, i_ref, g_ref, lbf_ref, lbb_ref, ng_ref, o_ref, of_ref, ob_ref,
                  sf_ref, sb_ref):
    L = q_ref.shape[1]
    C = HG_CHUNK
    n = L // C
    gsize = math.gcd(HG_INTRA_CHUNKS, n)
    qscale = HG_DK ** -0.5

    def rows(c):
        return pl.ds(pl.multiple_of(c * C, C), C)

    sf_ref[...] = jnp.zeros_like(sf_ref)
    sb_ref[...] = jnp.zeros_like(sb_ref)

    def group(it, carry):
        streams = []
        dests = []
        for k in range(gsize):
            c = it * gsize + k
            for fz_ref, lb_ref, o_s, s_s, cc, reverse in ((ff_ref, lbf_ref, of_ref, sf_ref, c, False),
                                                          (fb_ref, lbb_ref, ob_ref, sb_ref, n - 1 - c, True)):
                r = rows(cc)
                v = i_ref[0, r, :]
                streams.append((q_ref[0, r, :] * qscale, fz_ref[0, r, :], v, v.astype(BF16), lb_ref[0], reverse))
                dests.append((o_s, s_s, r))
        state = {id(sf_ref): sf_ref[...], id(sb_ref): sb_ref[...]}
        for (o, qe, u, e), (o_s, s_s, r) in zip(_hgrn2_intra_streams(streams), dests):
            st = state[id(s_s)]
            o_s[r, :] = o + _dot_nt(qe, st.astype(BF16))
            state[id(s_s)] = st * e + u
        sf_ref[...] = state[id(sf_ref)]
        sb_ref[...] = state[id(sb_ref)]
        return carry

    lax.fori_loop(0, n // gsize, group, 0)
    o = _rms(of_ref[...] + ob_ref[...], ng_ref[...])
    g = g_ref[0]
    o_ref[0] = (o * (g * jax.nn.sigmoid(g))).astype(o_ref.dtype)


def hgrn2_mixer(proj, lb_f, lb_b, norm_g):
    B, L, _ = proj.shape
    H = HG_HEADS

    def col(k):
        return pl.BlockSpec((1, L, LANES), lambda b, h: (b, 0, k * H + h))

    lbspec = pl.BlockSpec((1, 1, LANES), lambda b, h: (h, 0, 0))
    return pl.pallas_call(
        _hgrn2_kernel,
        grid=(B, H),
        in_specs=[col(0), col(1), col(2), col(3), col(4), lbspec, lbspec,
                  pl.BlockSpec((1, LANES), lambda b, h: (0, 0))],
        out_specs=pl.BlockSpec((1, L, LANES), lambda b, h: (b, 0, h)),
        out_shape=jax.ShapeDtypeStruct((B, L, HG_WIDTH), BF16),
        scratch_shapes=[pltpu.VMEM((L, LANES), F32), pltpu.VMEM((L, LANES), F32),
                        pltpu.VMEM((LANES, LANES), F32), pltpu.VMEM((LANES, LANES), F32)],
        compiler_params=_params("parallel", "arbitrary"),
        name="hgrn2",
    )(proj, proj, proj, proj, proj, lb_f.reshape(H, 1, LANES), lb_b.reshape(H, 1, LANES),
      norm_g.reshape(1, LANES))


def _hyena_filter_kernel(z_ref, t_ref, w1_ref, b1_ref, w2_ref, b2_ref, w3_ref, b3_ref, w4f_ref, w4b_ref,
                         fr_ref, ad_ref, gs_ref, gd_ref, kn_ref):
    def hdot(a, b):
        return jnp.dot(a, b, precision=HIGHEST, preferred_element_type=F32)

    fr = fr_ref[...]
    h = jnp.sin(fr * (hdot(z_ref[...], w1_ref[...]) + b1_ref[...]))
    h = jnp.sin(fr * (hdot(h, w2_ref[...]) + b2_ref[...]))
    h = jnp.sin(fr * (hdot(h, w3_ref[...]) + b3_ref[...]))
    window = jnp.exp(-t_ref[...] * ad_ref[...])
    hf = hdot(h, w4f_ref[...]) * window
    hb = hdot(h, w4b_ref[...]) * window
    row = lax.broadcasted_iota(jnp.int32, hb.shape, 0)
    hb = jnp.where(row == 0, 0.0, hb)
    gs_ref[...] = hf + hb
    gd_ref[...] = hb - hf
    sgn = jnp.where(row % 2 == 0, 1.0, -1.0)
    kn_ref[...] = jnp.sum((hf + hb) * sgn, axis=0, keepdims=True) * (0.5 / hb.shape[0])


def _split2(x):
    hi = x.astype(BF16)
    return hi, (x - hi.astype(F32)).astype(BF16)


def _hyena_spectrum_kernel(ah_ref, al_ref, bh_ref, bl_ref, gs_ref, gd_ref, kre_ref, kim_ref, sh_ref, sl_ref,
                           dh_ref, dl_ref, *, n_fft):
    @pl.when(pl.program_id(1) == 0)
    def _():
        sh_ref[...], sl_ref[...] = _split2(gs_ref[...])
        dh_ref[...], dl_ref[...] = _split2(gd_ref[...])

    tf = ah_ref.shape[0]
    f0 = pl.program_id(1) * tf
    row = lax.broadcasted_iota(jnp.int32, (tf, 1), 0) + f0
    wf = jnp.where(row == 0, 1.0 / n_fft, 2.0 / n_fft)
    ah = ah_ref[...]
    bh = bh_ref[...]
    kre_ref[...] = wf * (_dot(ah, sh_ref[...]) + _dot(ah, sl_ref[...]) + _dot(al_ref[...], sh_ref[...]))
    kim_ref[...] = wf * (_dot(bh, dh_ref[...]) + _dot(bh, dl_ref[...]) + _dot(bl_ref[...], dh_ref[...]))


def hyena_filter_spectrum(L, cos_hi, cos_lo, sin_hi, sin_lo, w1, b1, w2, b2, w3, b3, w4, sin_freq):
    C = HY_WIDTH
    t = jnp.linspace(0.0, 1.0, L, dtype=F32)[:, None]
    w = (2.0 * math.pi / L) * jnp.arange(L, dtype=F32)[:, None]
    f = jnp.linspace(1e-4, HY_BANDS - 1, HY_BANDS, dtype=F32)[None, :]
    z = jnp.concatenate([t, jnp.cos(f * w), -jnp.sin(f * w)], axis=-1)
    zp = jnp.pad(z, ((0, 0), (0, LANES - HY_EMB)))
    w1p = jnp.pad(w1, ((0, LANES - HY_EMB), (0, 0)))
    max_decay = math.log(HY_TARGET) / HY_FAST_PCT
    min_decay = math.log(HY_TARGET) / HY_SLOW_PCT
    absd = jnp.abs(jnp.linspace(min_decay, max_decay, C, dtype=F32))[None, :]
    tc = min(512, C)
    full = lambda shape: pl.BlockSpec(shape, lambda j: (0, 0))
    gs, gd, knyq = pl.pallas_call(
        _hyena_filter_kernel,
        grid=(C // tc,),
        in_specs=[full((L, LANES)), full((L, 1)), full((LANES, HY_FILT)), full((1, HY_FILT)),
                  full((HY_FILT, HY_FILT)), full((1, HY_FILT)), full((HY_FILT, HY_FILT)), full((1, HY_FILT)),
                  pl.BlockSpec((HY_FILT, tc), lambda j: (0, j)),
                  pl.BlockSpec((HY_FILT, tc), lambda j: (0, C // tc + j)),
                  full((1, HY_FILT)),
                  pl.BlockSpec((1, tc), lambda j: (0, j))],
        out_specs=[pl.BlockSpec((L, tc), lambda j: (0, j)), pl.BlockSpec((L, tc), lambda j: (0, j)),
                   pl.BlockSpec((1, tc), lambda j: (0, j))],
        out_shape=[jax.ShapeDtypeStruct((L, C), F32), jax.ShapeDtypeStruct((L, C), F32),
                   jax.ShapeDtypeStruct((1, C), F32)],
        compiler_params=_params("arbitrary"),
        name="hyena_filter",
    )(zp, t, w1p, b1.reshape(1, -1), w2, b2.reshape(1, -1), w3, b3.reshape(1, -1), w4, w4,
      sin_freq.reshape(1, -1), absd)

    tf = min(256, L)
    tbl = pl.BlockSpec((tf, L), lambda j, i: (i, 0))
    kre, kim = pl.pallas_call(
        functools.partial(_hyena_spectrum_kernel, n_fft=2 * L),
        grid=(C // tc, L // tf),
        in_specs=[tbl, tbl, tbl, tbl,
                  pl.BlockSpec((L, tc), lambda j, i: (0, j)),
                  pl.BlockSpec((L, tc), lambda j, i: (0, j))],
        out_specs=[pl.BlockSpec((tf, tc), lambda j, i: (i, j)), pl.BlockSpec((tf, tc), lambda j, i: (i, j))],
        out_shape=[jax.ShapeDtypeStruct((L, C), F32), jax.ShapeDtypeStruct((L, C), F32)],
        scratch_shapes=[pltpu.VMEM((L, tc), BF16)] * 4,
        compiler_params=_params("parallel", "arbitrary"),
        name="hyena_spectrum",
    )(cos_hi, cos_lo, sin_hi, sin_lo, gs, gd)
    return kre, kim, knyq


def _short_conv(u, w_ref, b_ref):
    L = u.shape[0]
    row = lax.broadcasted_iota(jnp.int32, (L, 1), 0)
    prev = jnp.where(row == 0, 0.0, pltpu.roll(u, 1, axis=0))
    nxt = jnp.where(row == L - 1, 0.0, pltpu.roll(u, L - 1, axis=0))
    return prev * w_ref[0:1, :] + u * w_ref[1:2, :] + nxt * w_ref[2:3, :] + b_ref[...]


def _hyena_conv_kernel(x1_ref, x2_ref, v_ref, cw1_ref, cw2_ref, cw3_ref, cb1_ref, cb2_ref, cb3_ref,
                       a_ref, b_ref, kre_ref, kim_ref, knyq_ref, bias_ref, o_ref, r1_ref, r2_ref, *, tf):
    L = v_ref.shape[1]
    x2 = _short_conv(x2_ref[0], cw2_ref, cb2_ref)
    vv = _short_conv(v_ref[0], cw3_ref, cb3_ref)
    vx = vv * x2
    vxb = vx.astype(BF16)
    sgn = jnp.where(lax.broadcasted_iota(jnp.int32, (L, 1), 0) % 2 == 0, 1.0, -1.0)
    def forward(fb):
        fs = slice(fb * tf, (fb + 1) * tf)
        return _dot(a_ref[fs, :], vxb), _dot(b_ref[fs, :], vxb)

    def product(fb, pq):
        p, q = pq
        fs = slice(fb * tf, (fb + 1) * tf)
        kre = kre_ref[fs, :]
        kim = kim_ref[fs, :]
        r1_ref[fs, :] = (p * kre + q * kim).astype(BF16)
        r2_ref[fs, :] = (q * kre - p * kim).astype(BF16)

    _pipeline3(L // tf, forward, product, lambda fb, _: None)

    v_nyq = jnp.sum(vx * sgn, axis=0, keepdims=True)
    y0 = vx * bias_ref[...] + sgn * (v_nyq * knyq_ref[...])
    x1 = _short_conv(x1_ref[0], cw1_ref, cb1_ref)
    y = _dot(a_ref[...], r1_ref[...]) + _dot(b_ref[...], r2_ref[...])
    o_ref[0] = ((y0 + y) * x1).astype(o_ref.dtype)


def hyena_conv(proj, conv_w, conv_b, cos_b, sin_b, kre, kim, knyq, bias, *, tc):
    B, L, _ = proj.shape
    C = HY_WIDTH
    tc = min(tc, C)
    nc = C // tc
    tf = min(512, L)

    def grp(k):
        return pl.BlockSpec((1, L, tc), lambda c, b: (b, 0, k * nc + c))

    def cw(k):
        return pl.BlockSpec((HY_SHORT, tc), lambda c, b: (0, k * nc + c))

    def cb(k):
        return pl.BlockSpec((1, tc), lambda c, b: (0, k * nc + c))

    tbl = pl.BlockSpec((L, L), lambda c, b: (0, 0), pipeline_mode=pl.Buffered(1))
    chan = pl.BlockSpec((L, tc), lambda c, b: (0, c), pipeline_mode=pl.Buffered(1))
    vec = pl.BlockSpec((1, tc), lambda c, b: (0, c))
    return pl.pallas_call(
        functools.partial(_hyena_conv_kernel, tf=tf),
        grid=(nc, B),
        in_specs=[grp(0), grp(1), grp(2), cw(0), cw(1), cw(2), cb(0), cb(1), cb(2),
                  tbl, tbl, chan, chan, vec, vec],
        out_specs=pl.BlockSpec((1, L, tc), lambda c, b: (b, 0, c)),
        out_shape=jax.ShapeDtypeStruct((B, L, C), BF16),
        scratch_shapes=[pltpu.VMEM((L, tc), BF16), pltpu.VMEM((L, tc), BF16)],
        compiler_params=_params("parallel", "arbitrary"),
        name="hyena_conv",
    )(proj, proj, proj, conv_w, conv_w, conv_w, conv_b.reshape(1, -1), conv_b.reshape(1, -1),
      conv_b.reshape(1, -1), cos_b, sin_b, kre, kim, knyq, bias.reshape(1, -1))


def _dft_tables(L):
    n = 2 * L
    r = math.gcd(GRID_W, L)
    j = jnp.arange(L, dtype=jnp.int32)[None, :]

    def cs(f):
        ang = ((f[:, None] * j) % n).astype(F32) * (2.0 * math.pi / n)
        return jnp.cos(ang), jnp.sin(ang)

    c0, s0 = cs(jnp.arange(r, dtype=jnp.int32))
    c1, s1 = cs(jnp.arange(L // r, dtype=jnp.int32) * r)
    cos_m = (c1[:, None, :] * c0[None, :, :] - s1[:, None, :] * s0[None, :, :]).reshape(L, L)
    sin_m = (s1[:, None, :] * c0[None, :, :] + c1[:, None, :] * s0[None, :, :]).reshape(L, L)
    cos_hi = cos_m.astype(BF16)
    sin_hi = sin_m.astype(BF16)
    return (cos_hi, (cos_m - cos_hi.astype(F32)).astype(BF16),
            sin_hi, (sin_m - sin_hi.astype(F32)).astype(BF16))


def _axial_rope_tables(L):
    rows = L // GRID_W
    r, c = jnp.meshgrid(jnp.arange(rows, dtype=F32), jnp.arange(GRID_W, dtype=F32), indexing='ij')
    r = r.reshape(-1)
    c = c.reshape(-1)
    half = GQ_HEAD_DIM // 2
    inv = ROPE_THETA ** (-jnp.arange(0, half, 2, dtype=F32) / half)
    ang_r = r[:, None] * inv[None, :]
    ang_c = c[:, None] * inv[None, :]
    ang = jnp.concatenate([ang_r, ang_r, ang_c, ang_c], axis=-1)
    return jnp.cos(ang), jnp.sin(ang)


def _even_mixer(x, layer, norm_g, w_in, w_out, conv_w, conv_b, fw1, fb1, fw2, fb2, fw3, fb3, fw4, sin_freq,
                hy_bias, q_g, k_g, lam_q1, lam_k1, lam_q2, lam_k2, subln_g):
    B, L, D = x.shape
    x2 = x.reshape(B * L, D)
    proj = norm_matmul(x2, norm_g, w_in.astype(BF16), tm=1024, tn=1024).reshape(B, L, -1)
    cos_hi, cos_lo, sin_hi, sin_lo = _dft_tables(L)
    kre, kim, knyq = hyena_filter_spectrum(L, cos_hi, cos_lo, sin_hi, sin_lo, fw1, fb1, fw2, fb2, fw3, fb3, fw4,
                                           sin_freq)
    o_a = hyena_conv(proj, conv_w, conv_b, cos_hi, sin_hi, kre, kim, knyq, hy_bias,
                     tc=256)
    lam_init = 0.8 - 0.6 * math.exp(-0.3 * layer)
    lam = (jnp.exp(jnp.sum(lam_q1.astype(F32) * lam_k1.astype(F32)))
           - jnp.exp(jnp.sum(lam_q2.astype(F32) * lam_k2.astype(F32))) + lam_init)
    slopes = jnp.asarray(np.array([2.0 ** (-8.0 * (h + 1) / DF_HEADS) for h in range(DF_HEADS)],
                                  dtype=np.float32))
    scal = jnp.concatenate([lam.reshape(1), slopes]).astype(F32)
    nb_hy = 3 * HY_WIDTH // LANES
    nb_qk = DF_QK_WIDTH // LANES
    o_b = diff_attention(proj, scal, q_g, k_g, subln_g, q_blk0=nb_hy, k_blk0=nb_hy + nb_qk,
                         v_blk0=nb_hy + 2 * nb_qk, out_scale=1.0 - lam_init, tq=1024)
    out = proj2_residual(o_a.reshape(B * L, -1), o_b.reshape(B * L, -1), w_out.astype(BF16), x2,
                         tm=1024, tn=1024)
    return out.reshape(B, L, D)


def _odd_mixer(x, norm_g, lb_f, lb_b, w_in, w_out, hg_norm_g, q_g, k_g):
    B, L, D = x.shape
    x2 = x.reshape(B * L, D)
    proj = norm_matmul(x2, norm_g, w_in.astype(BF16), tm=1024, tn=1664).reshape(B, L, -1)
    o_c = hgrn2_mixer(proj, lb_f, lb_b, hg_norm_g)
    cos, sin = _axial_rope_tables(L)
    gw = GQ_GROUP * GQ_HEAD_DIM
    q0 = 5 * HG_WIDTH
    k0 = q0 + GQ_WIDTH
    v0 = k0 + GQ_KV_HEADS * GQ_HEAD_DIM
    o_d = gqa_attention(proj, cos, sin, q_g, k_g, q_blk0=q0 // gw, k_blk0=k0 // LANES, v_blk0=v0 // LANES,
                        tq=1024)
    out = proj2_residual(o_c.reshape(B * L, -1), o_d.reshape(B * L, -1), w_out.astype(BF16), x2,
                         tm=1024, tn=1024)
    return out.reshape(B, L, D)


def kernel(x, mem, norm_mix_g, norm_mem_q_g, norm_mem_kv_g, norm_ffn_g, ev_w_in, ev_w_out, hy_conv_w, hy_conv_b, hy_fw1, hy_fb1, hy_fw2, hy_fb2, hy_fw3, hy_fb3, hy_fw4, hy_sin_freq, hy_bias, df_q_g, df_k_g, df_lam_q1, df_lam_k1, df_lam_q2, df_lam_k2, df_subln_g, od_w_in, od_w_out, hg_lb_logits, hg_norm_g, gq_q_g, gq_k_g, ca_w_q, ca_w_kv, ca_w_out, ca_q_g, ca_k_g, mlp_w1, mlp_w2):
    B, L, D = x.shape
    M = mem.shape[1]
    depth = norm_mix_g.shape[0]
    lb_soft = jax.nn.softmax(hg_lb_logits.astype(F32), axis=1)
    lbs = jnp.cumsum(lb_soft, axis=1)
    lbs = lbs - lbs[:, :1]
    mem2 = mem.reshape(B * M, D)
    for layer in range(depth):
        j = layer // 2
        if layer % 2 == 0:
            x = _even_mixer(x, layer, norm_mix_g[layer], ev_w_in[j], ev_w_out[j], hy_conv_w[j], hy_conv_b[j],
                            hy_fw1[j], hy_fb1[j], hy_fw2[j], hy_fb2[j], hy_fw3[j], hy_fb3[j], hy_fw4[j],
                            hy_sin_freq[j], hy_bias[j], df_q_g[j], df_k_g[j], df_lam_q1[j], df_lam_k1[j],
                            df_lam_q2[j], df_lam_k2[j], df_subln_g[j])
        else:
            x = _odd_mixer(x, norm_mix_g[layer], lbs[0, layer], lbs[1, layer], od_w_in[j], od_w_out[j],
                           hg_norm_g[j], gq_q_g[j], gq_k_g[j])
        kv = norm_matmul(mem2, norm_mem_kv_g[layer], ca_w_kv[layer].astype(BF16), tm=1024, tn=1024)
        x = cross_attention(x, kv.reshape(B, M, -1), norm_mem_q_g[layer], ca_w_q[layer].astype(BF16),
                            ca_q_g[layer], ca_k_g[layer], ca_w_out[layer].astype(BF16), tq=512)
        x = mlp_residual(x.reshape(B * L, D), norm_ffn_g[layer], layer_weight_bf16(mlp_w1, layer),
                         layer_weight_bf16(mlp_w2, layer), tm=512, th=1024).reshape(B, L, D)
    return x
```

```python
import functools
import math

import numpy as np
import jax
import jax.numpy as jnp
from jax import lax
from jax.experimental import pallas as pl
from jax.experimental.pallas import tpu as pltpu

F32 = jnp.float32
BF16 = jnp.bfloat16
HIGHEST = lax.Precision.HIGHEST

D_MODEL = 2048
NORM_EPS = 1e-6
GRID_W = 64
HY_WIDTH = D_MODEL // 2
HY_EMB = 33
HY_BANDS = (HY_EMB - 1) // 2
HY_FILT = 64
HY_SHORT = 3
HY_TARGET = 1e-2
HY_FAST_PCT = 0.3
HY_SLOW_PCT = 1.5
DF_HEADS = 8
DF_HEAD_DIM = 64
DF_V_DIM = 128
DF_QK_WIDTH = DF_HEADS * 2 * DF_HEAD_DIM
HG_WIDTH = D_MODEL // 2
HG_HEADS = 8
HG_DK = HG_WIDTH // HG_HEADS
GQ_WIDTH = D_MODEL - HG_WIDTH
GQ_HEADS = 8
GQ_KV_HEADS = 2
GQ_HEAD_DIM = GQ_WIDTH // GQ_HEADS
GQ_GROUP = GQ_HEADS // GQ_KV_HEADS
ROPE_THETA = 10000.0
CA_HEADS = 4
CA_HEAD_DIM = 128
CA_WIDTH = CA_HEADS * CA_HEAD_DIM

LANES = 128
SUBLANES = 8
VMEM_LIMIT_BYTES = 56 * 1024 * 1024

HG_CHUNK = 64
HG_BLOCK = SUBLANES
ATTN_STRIP = 128
HG_INTRA_CHUNKS = 16
LOG2E = 1.4426950408889634


def _params(*sem):
    return pltpu.CompilerParams(dimension_semantics=sem, vmem_limit_bytes=VMEM_LIMIT_BYTES)


def _rms(x, g):
    return x * lax.rsqrt(jnp.mean(x * x, axis=-1, keepdims=True) + NORM_EPS) * g


def _dot(a, b):
    return jnp.dot(a, b, preferred_element_type=F32)


def _dot_nt(a, b):
    return lax.dot_general(a, b, (((1,), (1,)), ((), ())), preferred_element_type=F32)


def _dot_tn(a, b):
    return lax.dot_general(a, b, (((0,), (0,)), ((), ())), preferred_element_type=F32)


def _pipeline3(n, stage_a, stage_b, stage_c):
    a = {}
    b = {}
    for t in range(n + 2):
        if t < n:
            a[t] = stage_a(t)
        if 0 <= t - 1 < n:
            b[t - 1] = stage_b(t - 1, a.pop(t - 1))
        if 0 <= t - 2 < n:
            stage_c(t - 2, b.pop(t - 2))


def _cast_kernel(w_ref, o_ref):
    o_ref[...] = w_ref[0].astype(o_ref.dtype)


def layer_weight_bf16(w, layer, *, block_bytes=8 * 1024 * 1024):
    _, K, N = w.shape
    tr = min(K, block_bytes // (4 * N))
    return pl.pallas_call(
        _cast_kernel,
        grid=(K // tr,),
        in_specs=[pl.BlockSpec((1, tr, N), lambda i: (layer, i, 0))],
        out_specs=pl.BlockSpec((tr, N), lambda i: (i, 0)),
        out_shape=jax.ShapeDtypeStruct((K, N), BF16),
        compiler_params=_params("parallel"),
        name="weight_bf16",
    )(w)


def _norm_matmul_kernel(x_ref, g_ref, w_ref, o_ref, hn_ref):
    @pl.when(pl.program_id(1) == 0)
    def _():
        hn_ref[...] = _rms(x_ref[...], g_ref[...]).astype(BF16)

    o_ref[...] = _dot(hn_ref[...], w_ref[...]).astype(o_ref.dtype)


def norm_matmul(x, g, w, *, tm, tn, out_dtype=F32):
    M, D = x.shape
    N = w.shape[1]
    tm = min(tm, M)
    tn = min(tn, N)
    return pl.pallas_call(
        _norm_matmul_kernel,
        grid=(M // tm, N // tn),
        in_specs=[pl.BlockSpec((tm, D), lambda i, j: (i, 0)),
                  pl.BlockSpec((1, D), lambda i, j: (0, 0)),
                  pl.BlockSpec((D, tn), lambda i, j: (0, j))],
        out_specs=pl.BlockSpec((tm, tn), lambda i, j: (i, j)),
        out_shape=jax.ShapeDtypeStruct((M, N), out_dtype),
        scratch_shapes=[pltpu.VMEM((tm, D), BF16)],
        compiler_params=_params("parallel", "arbitrary"),
        name="norm_matmul",
    )(x, g.reshape(1, D), w)


def _proj2_res_kernel(a1_ref, a2_ref, w1_ref, w2_ref, r_ref, o_ref):
    acc = _dot(a1_ref[...], w1_ref[...]) + _dot(a2_ref[...], w2_ref[...])
    o_ref[...] = r_ref[...] + acc


def proj2_residual(a1, a2, w, res, *, tm, tn):
    M, K1 = a1.shape
    K2 = a2.shape[1]
    assert K1 == K2
    N = w.shape[1]
    tm = min(tm, M)
    tn = min(tn, N)
    return pl.pallas_call(
        _proj2_res_kernel,
        grid=(M // tm, N // tn),
        in_specs=[pl.BlockSpec((tm, K1), lambda i, j: (i, 0)),
                  pl.BlockSpec((tm, K2), lambda i, j: (i, 0)),
                  pl.BlockSpec((K1, tn), lambda i, j: (0, j)),
                  pl.BlockSpec((K2, tn), lambda i, j: (1, j)),
                  pl.BlockSpec((tm, tn), lambda i, j: (i, j))],
        out_specs=pl.BlockSpec((tm, tn), lambda i, j: (i, j)),
        out_shape=jax.ShapeDtypeStruct((M, N), F32),
        compiler_params=_params("parallel", "arbitrary"),
        name="proj2_residual",
    )(a1, a2, w, w, res)


def _mlp_kernel(x_ref, g_ref, w1_ref, w2_ref, o_ref, hn_ref, acc_ref):
    j = pl.program_id(1)

    last = pl.num_programs(1) - 1

    @pl.when(j == 0)
    def _():
        x = x_ref[...]
        hn_ref[...] = _rms(x, g_ref[...]).astype(BF16)
        acc_ref[...] = x

    def accumulated():
        h1 = jnp.maximum(_dot(hn_ref[...], w1_ref[...]), 0.0)
        return acc_ref[...] + _dot((h1 * h1).astype(BF16), w2_ref[...])

    @pl.when(j != last)
    def _():
        acc_ref[...] = accumulated()

    @pl.when(j == last)
    def _():
        o_ref[...] = accumulated()


def mlp_residual(x, g, w1, w2, *, tm, th):
    M, D = x.shape
    H = w1.shape[1]
    tm = min(tm, M)
    th = min(th, H)
    return pl.pallas_call(
        _mlp_kernel,
        grid=(M // tm, H // th),
        in_specs=[pl.BlockSpec((tm, D), lambda i, j: (i, 0)),
                  pl.BlockSpec((1, D), lambda i, j: (0, 0)),
                  pl.BlockSpec((D, th), lambda i, j: (0, j)),
                  pl.BlockSpec((th, D), lambda i, j: (j, 0))],
        out_specs=pl.BlockSpec((tm, D), lambda i, j: (i, 0)),
        out_shape=jax.ShapeDtypeStruct((M, D), F32),
        scratch_shapes=[pltpu.VMEM((tm, D), BF16), pltpu.VMEM((tm, D), F32)],
        compiler_params=_params("parallel", "arbitrary"),
        name="mlp_residual",
    )(x, g.reshape(1, D), w1, w2)


def _cross_attn_kernel(x_ref, g_ref, wq_ref, kv_ref, qg_ref, kg_ref, wo_ref, o_ref):
    x = x_ref[0]
    hn = _rms(x, g_ref[...]).astype(BF16)
    q = _dot(hn, wq_ref[...])
    kv = kv_ref[0]
    scale = CA_HEAD_DIM ** -0.5 * LOG2E
    ones = jnp.ones((kv.shape[0], CA_HEAD_DIM), BF16)
    outs = []
    for h in range(CA_HEADS):
        sl = slice(h * CA_HEAD_DIM, (h + 1) * CA_HEAD_DIM)
        qh = (_rms(q[:, sl], qg_ref[...]) * scale).astype(BF16)
        kh = _rms(kv[:, sl], kg_ref[...]).astype(BF16)
        vh = kv[:, CA_WIDTH + h * CA_HEAD_DIM:CA_WIDTH + (h + 1) * CA_HEAD_DIM].astype(BF16)
        s = _dot_nt(qh, kh)
        e = jnp.exp2(s - jnp.max(s, axis=-1, keepdims=True)).astype(BF16)
        res = _dot(e, jnp.concatenate([vh, ones], axis=1))
        outs.append(res[:, :CA_HEAD_DIM] * (1.0 / res[:, CA_HEAD_DIM:CA_HEAD_DIM + 1]))
    o = jnp.concatenate(outs, axis=-1).astype(BF16)
    o_ref[0] = x + _dot(o, wo_ref[...])


def cross_attention(x, kv, g, wq, qg, kg, wo, *, tq):
    B, L, D = x.shape
    M = kv.shape[1]
    tq = min(tq, L)
    return pl.pallas_call(
        _cross_attn_kernel,
        grid=(B, L // tq),
        in_specs=[pl.BlockSpec((1, tq, D), lambda b, i: (b, i, 0)),
                  pl.BlockSpec((1, D), lambda b, i: (0, 0)),
                  pl.BlockSpec((D, CA_WIDTH), lambda b, i: (0, 0)),
                  pl.BlockSpec((1, M, 2 * CA_WIDTH), lambda b, i: (b, 0, 0)),
                  pl.BlockSpec((1, CA_HEAD_DIM), lambda b, i: (0, 0)),
                  pl.BlockSpec((1, CA_HEAD_DIM), lambda b, i: (0, 0)),
                  pl.BlockSpec((CA_WIDTH, D), lambda b, i: (0, 0))],
        out_specs=pl.BlockSpec((1, tq, D), lambda b, i: (b, i, 0)),
        out_shape=jax.ShapeDtypeStruct((B, L, D), F32),
        compiler_params=_params("parallel", "arbitrary"),
        name="cross_attention",
    )(x, g.reshape(1, D), wq, kv, qg.reshape(1, -1), kg.reshape(1, -1), wo)


def _half_rms(x, g2, lo_mask):
    sq = x * x
    ms_lo = jnp.sum(jnp.where(lo_mask, sq, 0.0), axis=-1, keepdims=True)
    ms_hi = jnp.sum(jnp.where(lo_mask, 0.0, sq), axis=-1, keepdims=True)
    inv = jnp.where(lo_mask, lax.rsqrt(ms_lo * (1.0 / DF_HEAD_DIM) + NORM_EPS),
                    lax.rsqrt(ms_hi * (1.0 / DF_HEAD_DIM) + NORM_EPS))
    return x * inv * g2


def _diff_attn_kernel(sc_ref, q_ref, k_ref, v_ref, qg_ref, kg_ref, sg_ref, o_ref, kn_ref, vb_ref, *,
                      out_scale):
    h = pl.program_id(1)
    qi = pl.program_id(2)
    tq = q_ref.shape[1]
    L = k_ref.shape[1]
    lane = lax.broadcasted_iota(jnp.int32, (1, LANES), 1)
    lo_mask = lane < DF_HEAD_DIM

    def prepare_kv():
        kn_ref[...] = _half_rms(k_ref[0], kg_ref[...], lo_mask).astype(BF16)
        vb_ref[:, :LANES] = v_ref[0].astype(BF16)
        vb_ref[:, LANES:] = jnp.ones((L, LANES), BF16)

    if tq == L:
        prepare_kv()
    else:
        pl.when(qi == 0)(prepare_kv)

    lam = sc_ref[0]
    slope = sc_ref[1 + h] * LOG2E
    ts = min(ATTN_STRIP, tq)
    col = lax.broadcasted_iota(jnp.int32, (1, L), 1).astype(F32)

    def scores(i):
        r0 = i * ts
        qn = _half_rms(q_ref[0, r0:r0 + ts, :], qg_ref[...], lo_mask) * (DF_HEAD_DIM ** -0.5 * LOG2E)
        row = (lax.broadcasted_iota(jnp.int32, (ts, 1), 0) + (qi * tq + r0)).astype(F32)
        bias = jnp.abs(row - col) * slope
        kn = kn_ref[...]
        return (_dot_nt(jnp.where(lo_mask, qn, 0.0).astype(BF16), kn) - bias,
                _dot_nt(jnp.where(lo_mask, 0.0, qn).astype(BF16), kn) - bias)

    def exps(i, ss):
        return tuple(jnp.exp2(s - jnp.max(s, axis=-1, keepdims=True)).astype(BF16) for s in ss)

    def values(i, es):
        r1, r2 = (_dot(e, vb_ref[...]) for e in es)
        o = (r1[:, :LANES] * (1.0 / r1[:, LANES:LANES + 1])
             - r2[:, :LANES] * (lam / r2[:, LANES:LANES + 1]))
        o_ref[0, i * ts:(i + 1) * ts, :] = (_rms(o, sg_ref[...]) * out_scale).astype(o_ref.dtype)

    _pipeline3(tq // ts, scores, exps, values)


def diff_attention(proj, scal, q_g, k_g, subln_g, *, q_blk0, k_blk0, v_blk0, out_scale, tq):
    B, L, _ = proj.shape
    tq = min(tq, L)
    qg2 = jnp.concatenate([q_g, q_g]).reshape(1, LANES)
    kg2 = jnp.concatenate([k_g, k_g]).reshape(1, LANES)
    return pl.pallas_call(
        functools.partial(_diff_attn_kernel, out_scale=out_scale),
        grid=(B, DF_HEADS, L // tq),
        in_specs=[pl.BlockSpec(memory_space=pltpu.SMEM),
                  pl.BlockSpec((1, tq, LANES), lambda b, h, i: (b, i, q_blk0 + h)),
                  pl.BlockSpec((1, L, LANES), lambda b, h, i: (b, 0, k_blk0 + h)),
                  pl.BlockSpec((1, L, LANES), lambda b, h, i: (b, 0, v_blk0 + h)),
                  pl.BlockSpec((1, LANES), lambda b, h, i: (0, 0)),
                  pl.BlockSpec((1, LANES), lambda b, h, i: (0, 0)),
                  pl.BlockSpec((1, LANES), lambda b, h, i: (0, 0))],
        out_specs=pl.BlockSpec((1, tq, LANES), lambda b, h, i: (b, i, h)),
        out_shape=jax.ShapeDtypeStruct((B, L, DF_HEADS * DF_V_DIM), BF16),
        scratch_shapes=[pltpu.VMEM((L, LANES), BF16), pltpu.VMEM((L, 2 * LANES), BF16)],
        compiler_params=_params("parallel", "parallel", "arbitrary"),
        name="diff_attention",
    )(scal, proj, proj, proj, qg2, kg2, subln_g.reshape(1, LANES))


def _rope(x, cos, sin, first_mask):
    q = GQ_HEAD_DIM // 4
    rot = jnp.where(first_mask, -pltpu.roll(x, LANES - q, axis=1), pltpu.roll(x, q, axis=1))
    return x * cos + rot * sin


def _gqa_kernel(q_ref, k_ref, v_ref, cq_ref, sq_ref, ck_ref, sk_ref, qg_ref, kg_ref, o_ref, kn_ref, vb_ref):
    qi = pl.program_id(2)
    lane = lax.broadcasted_iota(jnp.int32, (1, LANES), 1)
    first_mask = (lane % (GQ_HEAD_DIM // 2)) < (GQ_HEAD_DIM // 4)

    @pl.when(qi == 0)
    def _():
        kn = _rms(k_ref[0], kg_ref[...])
        kn_ref[...] = _rope(kn, ck_ref[...], sk_ref[...], first_mask).astype(BF16)
        vb_ref[:, :LANES] = v_ref[0].astype(BF16)
        vb_ref[:, LANES:] = jnp.ones((k_ref.shape[1], LANES), BF16)

    scale = GQ_HEAD_DIM ** -0.5 * LOG2E
    tq = q_ref.shape[1]
    ts = min(ATTN_STRIP, tq)
    nr = tq // ts

    def where(i):
        r0 = (i % nr) * ts
        return slice(r0, r0 + ts), slice((i // nr) * GQ_HEAD_DIM, (i // nr + 1) * GQ_HEAD_DIM)

    def scores(i):
        rs, cs = where(i)
        qn = _rope(_rms(q_ref[0, rs, cs], qg_ref[...]), cq_ref[rs, :], sq_ref[rs, :], first_mask) * scale
        return _dot_nt(qn.astype(BF16), kn_ref[...])

    def exps(i, s):
        return jnp.exp2(s - jnp.max(s, axis=-1, keepdims=True)).astype(BF16)

    def values(i, e):
        rs, cs = where(i)
        res = _dot(e, vb_ref[...])
        o_ref[0, rs, cs] = (res[:, :LANES] * (1.0 / res[:, LANES:LANES + 1])).astype(o_ref.dtype)

    _pipeline3(GQ_GROUP * nr, scores, exps, values)


def gqa_attention(proj, cos, sin, q_g, k_g, *, q_blk0, k_blk0, v_blk0, tq):
    B, L, _ = proj.shape
    tq = min(tq, L)
    gw = GQ_GROUP * GQ_HEAD_DIM
    return pl.pallas_call(
        _gqa_kernel,
        grid=(B, GQ_KV_HEADS, L // tq),
        in_specs=[pl.BlockSpec((1, tq, gw), lambda b, g, i: (b, i, q_blk0 + g)),
                  pl.BlockSpec((1, L, LANES), lambda b, g, i: (b, 0, k_blk0 + g)),
                  pl.BlockSpec((1, L, LANES), lambda b, g, i: (b, 0, v_blk0 + g)),
                  pl.BlockSpec((tq, LANES), lambda b, g, i: (i, 0)),
                  pl.BlockSpec((tq, LANES), lambda b, g, i: (i, 0)),
                  pl.BlockSpec((L, LANES), lambda b, g, i: (0, 0)),
                  pl.BlockSpec((L, LANES), lambda b, g, i: (0, 0)),
                  pl.BlockSpec((1, LANES), lambda b, g, i: (0, 0)),
                  pl.BlockSpec((1, LANES), lambda b, g, i: (0, 0))],
        out_specs=pl.BlockSpec((1, tq, gw), lambda b, g, i: (b, i, g)),
        out_shape=jax.ShapeDtypeStruct((B, L, GQ_WIDTH), BF16),
        scratch_shapes=[pltpu.VMEM((L, LANES), BF16), pltpu.VMEM((L, 2 * LANES), BF16)],
        compiler_params=_params("parallel", "parallel", "arbitrary"),
        name="gqa_attention",
    )(proj, proj, proj, cos, sin, cos, sin, q_g.reshape(1, LANES), k_g.reshape(1, LANES))


def _hgrn2_intra_streams(streams):
    C = HG_CHUNK
    nb = C // HG_BLOCK
    ns = range(len(streams))
    rev = [st[5] for st in streams]
    r_i = lax.broadcasted_iota(jnp.int32, (C, C), 0)
    c_i = lax.broadcasted_iota(jnp.int32, (C, C), 1)

    kk, lg = [], []
    for q, fz, v, vb, lb, _ in streams:
        f = lb + (1.0 - lb) * jax.nn.sigmoid(fz)
        kk.append(1.0 - f)
        lg.append(jnp.log(f) * LOG2E)

    tri_f = jnp.where(r_i >= c_i, 1.0, 0.0).astype(BF16)
    tri_b = jnp.where(r_i <= c_i, 1.0, 0.0).astype(BF16)
    b = []
    for i in ns:
        hi = lg[i].astype(BF16)
        r1 = lg[i] - hi.astype(F32)
        mid = r1.astype(BF16)
        lo = (r1 - mid.astype(F32)).astype(BF16)
        tri = tri_b if rev[i] else tri_f
        b.append(_dot(tri, hi) + _dot(tri, mid) + _dot(tri, lo))
    b_end = [b[i][0:1, :] if rev[i] else b[i][C - 1:C, :] for i in ns]
    qe = [(streams[i][0] * jnp.exp2(b[i])).astype(BF16) for i in ns]
    u = [_dot_tn(streams[i][3], (kk[i] * jnp.exp2(b_end[i] - b[i])).astype(BF16)) for i in ns]

    pos = lax.broadcasted_iota(jnp.int32, (nb, HG_BLOCK, LANES), 1)
    b3 = [b[i].reshape(nb, HG_BLOCK, LANES) for i in ns]
    q3 = [streams[i][0].reshape(nb, HG_BLOCK, LANES) for i in ns]
    seg_pos = lax.broadcasted_iota(jnp.int32, (C, HG_BLOCK * LANES), 1) // LANES
    row_pos = lax.broadcasted_iota(jnp.int32, (C, HG_BLOCK * LANES), 0) % HG_BLOCK
    kcat = []
    for i in ns:
        kb = kk[i].astype(BF16)
        kcat.append(jnp.where(seg_pos == row_pos, jnp.concatenate([kb] * HG_BLOCK, axis=1),
                              jnp.zeros((), BF16)))
    ps = [[] for i in ns]
    for s in range(HG_BLOCK):
        for i in ns:
            valid = (pos <= s) if rev[i] else (pos >= s)
            d = jnp.where(valid, b3[i] - b3[i][:, s:s + 1, :], -jnp.inf)
            ps[i].append((q3[i] * jnp.exp2(d)).reshape(C, LANES).astype(BF16))
    same_blk = (r_i // HG_BLOCK) == (c_i // HG_BLOCK)
    adiag = [_dot_nt(jnp.concatenate(ps[i], axis=1), kcat[i]) for i in ns]

    zrow = jnp.zeros((1, 1, LANES), F32)
    qt = []
    for i in ns:
        if rev[i]:
            rblk3 = jnp.concatenate([b3[i][1:, 0:1, :], zrow], axis=0)
        else:
            rblk3 = jnp.concatenate([zrow, b3[i][:nb - 1, HG_BLOCK - 1:HG_BLOCK, :]], axis=0)
        qt.append((q3[i] * jnp.exp2(b3[i] - rblk3)).reshape(C, LANES))
    rows = [[] for i in ns]
    for blk in range(nb):
        for i in ns:
            if rev[i]:
                edge = (blk + 1) * HG_BLOCK
                if edge == C:
                    rows[i].append(jnp.zeros((HG_BLOCK, C), F32))
                    continue
                kt = kk[i][edge:, :] * jnp.exp2(b[i][edge:edge + 1, :] - b[i][edge:, :])
                kfull = jnp.concatenate([jnp.zeros((edge, LANES), F32), kt], axis=0)
            else:
                edge = blk * HG_BLOCK
                if edge == 0:
                    rows[i].append(jnp.zeros((HG_BLOCK, C), F32))
                    continue
                kt = kk[i][:edge, :] * jnp.exp2(b[i][edge - 1:edge, :] - b[i][:edge, :])
                kfull = jnp.concatenate([kt, jnp.zeros((C - edge, LANES), F32)], axis=0)
            rows[i].append(_dot_nt(qt[i][blk * HG_BLOCK:(blk + 1) * HG_BLOCK, :].astype(BF16),
                                   kfull.astype(BF16)))
    outs = []
    for i in ns:
        att = jnp.where(same_blk, adiag[i], jnp.concatenate(rows[i], axis=0))
        o = _dot(att.astype(BF16), streams[i][3])
        outs.append((o, qe[i], u[i], jnp.exp2(b_end[i])))
    return outs


def _hgrn2_kernel(q_ref, ff_ref, fb_ref, i_ref, g_ref, lbf_ref, lbb_ref, ng_ref, o_ref, of_ref, ob_ref,
                  sf_ref, sb_ref):
    L = q_ref.shape[1]
    C = HG_CHUNK
    n = L // C
    gsize = math.gcd(HG_INTRA_CHUNKS, n)
    qscale = HG_DK ** -0.5

    def rows(c):
        return pl.ds(pl.multiple_of(c * C, C), C)

    sf_ref[...] = jnp.zeros_like(sf_ref)
    sb_ref[...] = jnp.zeros_like(sb_ref)

    def group(it, carry):
        streams = []
        dests = []
        for k in range(gsize):
            c = it * gsize + k
            for fz_ref, lb_ref, o_s, s_s, cc, reverse in ((ff_ref, lbf_ref, of_ref, sf_ref, c, False),
                                                          (fb_ref, lbb_ref, ob_ref, sb_ref, n - 1 - c, True)):
                r = rows(cc)
                v = i_ref[0, r, :]
                streams.append((q_ref[0, r, :] * qscale, fz_ref[0, r, :], v, v.astype(BF16), lb_ref[0], reverse))
                dests.append((o_s, s_s, r))
        state = {id(sf_ref): sf_ref[...], id(sb_ref): sb_ref[...]}
        for (o, qe, u, e), (o_s, s_s, r) in zip(_hgrn2_intra_streams(streams), dests):
            st = state[id(s_s)]
            o_s[r, :] = o + _dot_nt(qe, st.astype(BF16))
            state[id(s_s)] = st * e + u
        sf_ref[...] = state[id(sf_ref)]
        sb_ref[...] = state[id(sb_ref)]
        return carry

    lax.fori_loop(0, n // gsize, group, 0)
    o = _rms(of_ref[...] + ob_ref[...], ng_ref[...])
    g = g_ref[0]
    o_ref[0] = (o * (g * jax.nn.sigmoid(g))).astype(o_ref.dtype)


def hgrn2_mixer(proj, lb_f, lb_b, norm_g):
    B, L, _ = proj.shape
    H = HG_HEADS

    def col(k):
        return pl.BlockSpec((1, L, LANES), lambda b, h: (b, 0, k * H + h))

    lbspec = pl.BlockSpec((1, 1, LANES), lambda b, h: (h, 0, 0))
    return pl.pallas_call(
        _hgrn2_kernel,
        grid=(B, H),
        in_specs=[col(0), col(1), col(2), col(3), col(4), lbspec, lbspec,
                  pl.BlockSpec((1, LANES), lambda b, h: (0, 0))],
        out_specs=pl.BlockSpec((1, L, LANES), lambda b, h: (b, 0, h)),
        out_shape=jax.ShapeDtypeStruct((B, L, HG_WIDTH), BF16),
        scratch_shapes=[pltpu.VMEM((L, LANES), F32), pltpu.VMEM((L, LANES), F32),
                        pltpu.VMEM((LANES, LANES), F32), pltpu.VMEM((LANES, LANES), F32)],
        compiler_params=_params("parallel", "arbitrary"),
        name="hgrn2",
    )(proj, proj, proj, proj, proj, lb_f.reshape(H, 1, LANES), lb_b.reshape(H, 1, LANES),
      norm_g.reshape(1, LANES))


def _hyena_filter_kernel(z_ref, t_ref, w1_ref, b1_ref, w2_ref, b2_ref, w3_ref, b3_ref, w4f_ref, w4b_ref,
                         fr_ref, ad_ref, gs_ref, gd_ref, kn_ref):
    def hdot(a, b):
        return jnp.dot(a, b, precision=HIGHEST, preferred_element_type=F32)

    fr = fr_ref[...]
    h = jnp.sin(fr * (hdot(z_ref[...], w1_ref[...]) + b1_ref[...]))
    h = jnp.sin(fr * (hdot(h, w2_ref[...]) + b2_ref[...]))
    h = jnp.sin(fr * (hdot(h, w3_ref[...]) + b3_ref[...]))
    window = jnp.exp(-t_ref[...] * ad_ref[...])
    hf = hdot(h, w4f_ref[...]) * window
    hb = hdot(h, w4b_ref[...]) * window
    row = lax.broadcasted_iota(jnp.int32, hb.shape, 0)
    hb = jnp.where(row == 0, 0.0, hb)
    gs_ref[...] = hf + hb
    gd_ref[...] = hb - hf
    sgn = jnp.where(row % 2 == 0, 1.0, -1.0)
    kn_ref[...] = jnp.sum((hf + hb) * sgn, axis=0, keepdims=True) * (0.5 / hb.shape[0])


def _split2(x):
    hi = x.astype(BF16)
    return hi, (x - hi.astype(F32)).astype(BF16)


def _hyena_spectrum_kernel(ah_ref, al_ref, bh_ref, bl_ref, gs_ref, gd_ref, kre_ref, kim_ref, sh_ref, sl_ref,
                           dh_ref, dl_ref, *, n_fft):
    @pl.when(pl.program_id(1) == 0)
    def _():
        sh_ref[...], sl_ref[...] = _split2(gs_ref[...])
        dh_ref[...], dl_ref[...] = _split2(gd_ref[...])

    tf = ah_ref.shape[0]
    f0 = pl.program_id(1) * tf
    row = lax.broadcasted_iota(jnp.int32, (tf, 1), 0) + f0
    wf = jnp.where(row == 0, 1.0 / n_fft, 2.0 / n_fft)
    ah = ah_ref[...]
    bh = bh_ref[...]
    kre_ref[...] = wf * (_dot(ah, sh_ref[...]) + _dot(ah, sl_ref[...]) + _dot(al_ref[...], sh_ref[...]))
    kim_ref[...] = wf * (_dot(bh, dh_ref[...]) + _dot(bh, dl_ref[...]) + _dot(bl_ref[...], dh_ref[...]))


def hyena_filter_spectrum(L, cos_hi, cos_lo, sin_hi, sin_lo, w1, b1, w2, b2, w3, b3, w4, sin_freq):
    C = HY_WIDTH
    t = jnp.linspace(0.0, 1.0, L, dtype=F32)[:, None]
    w = (2.0 * math.pi / L) * jnp.arange(L, dtype=F32)[:, None]
    f = jnp.linspace(1e-4, HY_BANDS - 1, HY_BANDS, dtype=F32)[None, :]
    z = jnp.concatenate([t, jnp.cos(f * w), -jnp.sin(f * w)], axis=-1)
    zp = jnp.pad(z, ((0, 0), (0, LANES - HY_EMB)))
    w1p = jnp.pad(w1, ((0, LANES - HY_EMB), (0, 0)))
    max_decay = math.log(HY_TARGET) / HY_FAST_PCT
    min_decay = math.log(HY_TARGET) / HY_SLOW_PCT
    absd = jnp.abs(jnp.linspace(min_decay, max_decay, C, dtype=F32))[None, :]
    tc = min(512, C)
    full = lambda shape: pl.BlockSpec(shape, lambda j: (0, 0))
    gs, gd, knyq = pl.pallas_call(
        _hyena_filter_kernel,
        grid=(C // tc,),
        in_specs=[full((L, LANES)), full((L, 1)), full((LANES, HY_FILT)), full((1, HY_FILT)),
                  full((HY_FILT, HY_FILT)), full((1, HY_FILT)), full((HY_FILT, HY_FILT)), full((1, HY_FILT)),
                  pl.BlockSpec((HY_FILT, tc), lambda j: (0, j)),
                  pl.BlockSpec((HY_FILT, tc), lambda j: (0, C // tc + j)),
                  full((1, HY_FILT)),
                  pl.BlockSpec((1, tc), lambda j: (0, j))],
        out_specs=[pl.BlockSpec((L, tc), lambda j: (0, j)), pl.BlockSpec((L, tc), lambda j: (0, j)),
                   pl.BlockSpec((1, tc), lambda j: (0, j))],
        out_shape=[jax.ShapeDtypeStruct((L, C), F32), jax.ShapeDtypeStruct((L, C), F32),
                   jax.ShapeDtypeStruct((1, C), F32)],
        compiler_params=_params("arbitrary"),
        name="hyena_filter",
    )(zp, t, w1p, b1.reshape(1, -1), w2, b2.reshape(1, -1), w3, b3.reshape(1, -1), w4, w4,
      sin_freq.reshape(1, -1), absd)

    tf = min(256, L)
    tbl = pl.BlockSpec((tf, L), lambda j, i: (i, 0))
    kre, kim = pl.pallas_call(
        functools.partial(_hyena_spectrum_kernel, n_fft=2 * L),
        grid=(C // tc, L // tf),
        in_specs=[tbl, tbl, tbl, tbl,
                  pl.BlockSpec((L, tc), lambda j, i: (0, j)),
                  pl.BlockSpec((L, tc), lambda j, i: (0, j))],
        out_specs=[pl.BlockSpec((tf, tc), lambda j, i: (i, j)), pl.BlockSpec((tf, tc), lambda j, i: (i, j))],
        out_shape=[jax.ShapeDtypeStruct((L, C), F32), jax.ShapeDtypeStruct((L, C), F32)],
        scratch_shapes=[pltpu.VMEM((L, tc), BF16)] * 4,
        compiler_params=_params("parallel", "arbitrary"),
        name="hyena_spectrum",
    )(cos_hi, cos_lo, sin_hi, sin_lo, gs, gd)
    return kre, kim, knyq


def _short_conv(u, w_ref, b_ref):
    L = u.shape[0]
    row = lax.broadcasted_iota(jnp.int32, (L, 1), 0)
    prev = jnp.where(row == 0, 0.0, pltpu.roll(u, 1, axis=0))
    nxt = jnp.where(row == L - 1, 0.0, pltpu.roll(u, L - 1, axis=0))
    return prev * w_ref[0:1, :] + u * w_ref[1:2, :] + nxt * w_ref[2:3, :] + b_ref[...]


def _hyena_conv_kernel(x1_ref, x2_ref, v_ref, cw1_ref, cw2_ref, cw3_ref, cb1_ref, cb2_ref, cb3_ref,
                       a_ref, b_ref, kre_ref, kim_ref, knyq_ref, bias_ref, o_ref, r1_ref, r2_ref, *, tf):
    L = v_ref.shape[1]
    x2 = _short_conv(x2_ref[0], cw2_ref, cb2_ref)
    vv = _short_conv(v_ref[0], cw3_ref, cb3_ref)
    vx = vv * x2
    vxb = vx.astype(BF16)
    sgn = jnp.where(lax.broadcasted_iota(jnp.int32, (L, 1), 0) % 2 == 0, 1.0, -1.0)
    def forward(fb):
        fs = slice(fb * tf, (fb + 1) * tf)
        return _dot(a_ref[fs, :], vxb), _dot(b_ref[fs, :], vxb)

    def product(fb, pq):
        p, q = pq
        fs = slice(fb * tf, (fb + 1) * tf)
        kre = kre_ref[fs, :]
        kim = kim_ref[fs, :]
        r1_ref[fs, :] = (p * kre + q * kim).astype(BF16)
        r2_ref[fs, :] = (q * kre - p * kim).astype(BF16)

    _pipeline3(L // tf, forward, product, lambda fb, _: None)

    v_nyq = jnp.sum(vx * sgn, axis=0, keepdims=True)
    y0 = vx * bias_ref[...] + sgn * (v_nyq * knyq_ref[...])
    x1 = _short_conv(x1_ref[0], cw1_ref, cb1_ref)
    y = _dot(a_ref[...], r1_ref[...]) + _dot(b_ref[...], r2_ref[...])
    o_ref[0] = ((y0 + y) * x1).astype(o_ref.dtype)


def hyena_conv(proj, conv_w, conv_b, cos_b, sin_b, kre, kim, knyq, bias, *, tc):
    B, L, _ = proj.shape
    C = HY_WIDTH
    tc = min(tc, C)
    nc = C // tc
    tf = min(512, L)

    def grp(k):
        return pl.BlockSpec((1, L, tc), lambda c, b: (b, 0, k * nc + c))

    def cw(k):
        return pl.BlockSpec((HY_SHORT, tc), lambda c, b: (0, k * nc + c))

    def cb(k):
        return pl.BlockSpec((1, tc), lambda c, b: (0, k * nc + c))

    tbl = pl.BlockSpec((L, L), lambda c, b: (0, 0), pipeline_mode=pl.Buffered(1))
    chan = pl.BlockSpec((L, tc), lambda c, b: (0, c), pipeline_mode=pl.Buffered(1))
    vec = pl.BlockSpec((1, tc), lambda c, b: (0, c))
    return pl.pallas_call(
        functools.partial(_hyena_conv_kernel, tf=tf),
        grid=(nc, B),
        in_specs=[grp(0), grp(1), grp(2), cw(0), cw(1), cw(2), cb(0), cb(1), cb(2),
                  tbl, tbl, chan, chan, vec, vec],
        out_specs=pl.BlockSpec((1, L, tc), lambda c, b: (b, 0, c)),
        out_shape=jax.ShapeDtypeStruct((B, L, C), BF16),
        scratch_shapes=[pltpu.VMEM((L, tc), BF16), pltpu.VMEM((L, tc), BF16)],
        compiler_params=_params("parallel", "arbitrary"),
        name="hyena_conv",
    )(proj, proj, proj, conv_w, conv_w, conv_w, conv_b.reshape(1, -1), conv_b.reshape(1, -1),
      conv_b.reshape(1, -1), cos_b, sin_b, kre, kim, knyq, bias.reshape(1, -1))


def _dft_tables(L):
    n = 2 * L
    r = math.gcd(GRID_W, L)
    j = jnp.arange(L, dtype=jnp.int32)[None, :]

    def cs(f):
        ang = ((f[:, None] * j) % n).astype(F32) * (2.0 * math.pi / n)
        return jnp.cos(ang), jnp.sin(ang)

    c0, s0 = cs(jnp.arange(r, dtype=jnp.int32))
    c1, s1 = cs(jnp.arange(L // r, dtype=jnp.int32) * r)
    cos_m = (c1[:, None, :] * c0[None, :, :] - s1[:, None, :] * s0[None, :, :]).reshape(L, L)
    sin_m = (s1[:, None, :] * c0[None, :, :] + c1[:, None, :] * s0[None, :, :]).reshape(L, L)
    cos_hi = cos_m.astype(BF16)
    sin_hi = sin_m.astype(BF16)
    return (cos_hi, (cos_m - cos_hi.astype(F32)).astype(BF16),
            sin_hi, (sin_m - sin_hi.astype(F32)).astype(BF16))


def _axial_rope_tables(L):
    rows = L // GRID_W
    r, c = jnp.meshgrid(jnp.arange(rows, dtype=F32), jnp.arange(GRID_W, dtype=F32), indexing='ij')
    r = r.reshape(-1)
    c = c.reshape(-1)
    half = GQ_HEAD_DIM // 2
    inv = ROPE_THETA ** (-jnp.arange(0, half, 2, dtype=F32) / half)
    ang_r = r[:, None] * inv[None, :]
    ang_c = c[:, None] * inv[None, :]
    ang = jnp.concatenate([ang_r, ang_r, ang_c, ang_c], axis=-1)
    return jnp.cos(ang), jnp.sin(ang)


def _even_mixer(x, layer, norm_g, w_in, w_out, conv_w, conv_b, fw1, fb1, fw2, fb2, fw3, fb3, fw4, sin_freq,
                hy_bias, q_g, k_g, lam_q1, lam_k1, lam_q2, lam_k2, subln_g):
    B, L, D = x.shape
    x2 = x.reshape(B * L, D)
    proj = norm_matmul(x2, norm_g, w_in.astype(BF16), tm=1024, tn=1536).reshape(B, L, -1)
    cos_hi, cos_lo, sin_hi, sin_lo = _dft_tables(L)
    kre, kim, knyq = hyena_filter_spectrum(L, cos_hi, cos_lo, sin_hi, sin_lo, fw1, fb1, fw2, fb2, fw3, fb3, fw4,
                                           sin_freq)
    o_a = hyena_conv(proj, conv_w, conv_b, cos_hi, sin_hi, kre, kim, knyq, hy_bias,
                     tc=256)
    lam_init = 0.8 - 0.6 * math.exp(-0.3 * layer)
    lam = (jnp.exp(jnp.sum(lam_q1.astype(F32) * lam_k1.astype(F32)))
           - jnp.exp(jnp.sum(lam_q2.astype(F32) * lam_k2.astype(F32))) + lam_init)
    slopes = jnp.asarray(np.array([2.0 ** (-8.0 * (h + 1) / DF_HEADS) for h in range(DF_HEADS)],
                                  dtype=np.float32))
    scal = jnp.concatenate([lam.reshape(1), slopes]).astype(F32)
    nb_hy = 3 * HY_WIDTH // LANES
    nb_qk = DF_QK_WIDTH // LANES
    o_b = diff_attention(proj, scal, q_g, k_g, subln_g, q_blk0=nb_hy, k_blk0=nb_hy + nb_qk,
                         v_blk0=nb_hy + 2 * nb_qk, out_scale=1.0 - lam_init, tq=2048)
    out = proj2_residual(o_a.reshape(B * L, -1), o_b.reshape(B * L, -1), w_out.astype(BF16), x2,
                         tm=1024, tn=1024)
    return out.reshape(B, L, D)


def _odd_mixer(x, norm_g, lb_f, lb_b, w_in, w_out, hg_norm_g, q_g, k_g):
    B, L, D = x.shape
    x2 = x.reshape(B * L, D)
    proj = norm_matmul(x2, norm_g, w_in.astype(BF16), tm=1024, tn=1664).reshape(B, L, -1)
    o_c = hgrn2_mixer(proj, lb_f, lb_b, hg_norm_g)
    cos, sin = _axial_rope_tables(L)
    gw = GQ_GROUP * GQ_HEAD_DIM
    q0 = 5 * HG_WIDTH
    k0 = q0 + GQ_WIDTH
    v0 = k0 + GQ_KV_HEADS * GQ_HEAD_DIM
    o_d = gqa_attention(proj, cos, sin, q_g, k_g, q_blk0=q0 // gw, k_blk0=k0 // LANES, v_blk0=v0 // LANES,
                        tq=1024)
    out = proj2_residual(o_c.reshape(B * L, -1), o_d.reshape(B * L, -1), w_out.astype(BF16), x2,
                         tm=1024, tn=1024)
    return out.reshape(B, L, D)


def kernel(x, mem, norm_mix_g, norm_mem_q_g, norm_mem_kv_g, norm_ffn_g, ev_w_in, ev_w_out, hy_conv_w, hy_conv_b, hy_fw1, hy_fb1, hy_fw2, hy_fb2, hy_fw3, hy_fb3, hy_fw4, hy_sin_freq, hy_bias, df_q_g, df_k_g, df_lam_q1, df_lam_k1, df_lam_q2, df_lam_k2, df_subln_g, od_w_in, od_w_out, hg_lb_logits, hg_norm_g, gq_q_g, gq_k_g, ca_w_q, ca_w_kv, ca_w_out, ca_q_g, ca_k_g, mlp_w1, mlp_w2):
    B, L, D = x.shape
    M = mem.shape[1]
    depth = norm_mix_g.shape[0]
    lb_soft = jax.nn.softmax(hg_lb_logits.astype(F32), axis=1)
    lbs = jnp.cumsum(lb_soft, axis=1)
    lbs = lbs - lbs[:, :1]
    mem2 = mem.reshape(B * M, D)
    for layer in range(depth):
        j = layer // 2
        if layer % 2 == 0:
            x = _even_mixer(x, layer, norm_mix_g[layer], ev_w_in[j], ev_w_out[j], hy_conv_w[j], hy_conv_b[j],
                            hy_fw1[j], hy_fb1[j], hy_fw2[j], hy_fb2[j], hy_fw3[j], hy_fb3[j], hy_fw4[j],
                            hy_sin_freq[j], hy_bias[j], df_q_g[j], df_k_g[j], df_lam_q1[j], df_lam_k1[j],
                            df_lam_q2[j], df_lam_k2[j], df_subln_g[j])
        else:
            x = _odd_mixer(x, norm_mix_g[layer], lbs[0, layer], lbs[1, layer], od_w_in[j], od_w_out[j],
                           hg_norm_g[j], gq_q_g[j], gq_k_g[j])
        kv = norm_matmul(mem2, norm_mem_kv_g[layer], ca_w_kv[layer].astype(BF16), tm=1024, tn=1024)
        x = cross_attention(x, kv.reshape(B, M, -1), norm_mem_q_g[layer], ca_w_q[layer].astype(BF16),
                            ca_q_g[layer], ca_k_g[layer], ca_w_out[layer].astype(BF16), tq=512)
        x = mlp_residual(x.reshape(B * L, D), norm_ffn_g[layer], layer_weight_bf16(mlp_w1, layer),
                         layer_weight_bf16(mlp_w2, layer), tm=512, th=1024).reshape(B, L, D)
    return x
```

```python
import functools
import math

import numpy as np
import jax
import jax.numpy as jnp
from jax import lax
from jax.experimental import pallas as pl
from jax.experimental.pallas import tpu as pltpu

F32 = jnp.float32
BF16 = jnp.bfloat16
HIGHEST = lax.Precision.HIGHEST

D_MODEL = 2048
NORM_EPS = 1e-6
GRID_W = 64
HY_WIDTH = D_MODEL // 2
HY_EMB = 33
HY_BANDS = (HY_EMB - 1) // 2
HY_FILT = 64
HY_SHORT = 3
HY_TARGET = 1e-2
HY_FAST_PCT = 0.3
HY_SLOW_PCT = 1.5
DF_HEADS = 8
DF_HEAD_DIM = 64
DF_V_DIM = 128
DF_QK_WIDTH = DF_HEADS * 2 * DF_HEAD_DIM
HG_WIDTH = D_MODEL // 2
HG_HEADS = 8
HG_DK = HG_WIDTH // HG_HEADS
GQ_WIDTH = D_MODEL - HG_WIDTH
GQ_HEADS = 8
GQ_KV_HEADS = 2
GQ_HEAD_DIM = GQ_WIDTH // GQ_HEADS
GQ_GROUP = GQ_HEADS // GQ_KV_HEADS
ROPE_THETA = 10000.0
CA_HEADS = 4
CA_HEAD_DIM = 128
CA_WIDTH = CA_HEADS * CA_HEAD_DIM

LANES = 128
SUBLANES = 8
VMEM_LIMIT_BYTES = 56 * 1024 * 1024

HG_CHUNK = 64
HG_BLOCK = SUBLANES
ATTN_STRIP = 128
HG_INTRA_CHUNKS = 16
LOG2E = 1.4426950408889634


def _params(*sem):
    return pltpu.CompilerParams(dimension_semantics=sem, vmem_limit_bytes=VMEM_LIMIT_BYTES)


def _rms(x, g):
    return x * lax.rsqrt(jnp.mean(x * x, axis=-1, keepdims=True) + NORM_EPS) * g


def _dot(a, b):
    return jnp.dot(a, b, preferred_element_type=F32)


def _dot_nt(a, b):
    return lax.dot_general(a, b, (((1,), (1,)), ((), ())), preferred_element_type=F32)


def _dot_tn(a, b):
    return lax.dot_general(a, b, (((0,), (0,)), ((), ())), preferred_element_type=F32)


def _pipeline3(n, stage_a, stage_b, stage_c):
    a = {}
    b = {}
    for t in range(n + 2):
        if t < n:
            a[t] = stage_a(t)
        if 0 <= t - 1 < n:
            b[t - 1] = stage_b(t - 1, a.pop(t - 1))
        if 0 <= t - 2 < n:
            stage_c(t - 2, b.pop(t - 2))


def _cast_kernel(w_ref, o_ref):
    o_ref[...] = w_ref[0].astype(o_ref.dtype)


def layer_weight_bf16(w, layer, *, block_bytes=8 * 1024 * 1024):
    _, K, N = w.shape
    tr = min(K, block_bytes // (4 * N))
    return pl.pallas_call(
        _cast_kernel,
        grid=(K // tr,),
        in_specs=[pl.BlockSpec((1, tr, N), lambda i: (layer, i, 0))],
        out_specs=pl.BlockSpec((tr, N), lambda i: (i, 0)),
        out_shape=jax.ShapeDtypeStruct((K, N), BF16),
        compiler_params=_params("parallel"),
        name="weight_bf16",
    )(w)


def _norm_matmul_kernel(x_ref, g_ref, w_ref, o_ref, hn_ref):
    @pl.when(pl.program_id(1) == 0)
    def _():
        hn_ref[...] = _rms(x_ref[...], g_ref[...]).astype(BF16)

    o_ref[...] = _dot(hn_ref[...], w_ref[...]).astype(o_ref.dtype)


def norm_matmul(x, g, w, *, tm, tn, out_dtype=F32):
    M, D = x.shape
    N = w.shape[1]
    tm = min(tm, M)
    tn = min(tn, N)
    return pl.pallas_call(
        _norm_matmul_kernel,
        grid=(M // tm, N // tn),
        in_specs=[pl.BlockSpec((tm, D), lambda i, j: (i, 0)),
                  pl.BlockSpec((1, D), lambda i, j: (0, 0)),
                  pl.BlockSpec((D, tn), lambda i, j: (0, j))],
        out_specs=pl.BlockSpec((tm, tn), lambda i, j: (i, j)),
        out_shape=jax.ShapeDtypeStruct((M, N), out_dtype),
        scratch_shapes=[pltpu.VMEM((tm, D), BF16)],
        compiler_params=_params("parallel", "arbitrary"),
        name="norm_matmul",
    )(x, g.reshape(1, D), w)


def _proj2_res_kernel(a1_ref, a2_ref, w1_ref, w2_ref, r_ref, o_ref):
    acc = _dot(a1_ref[...], w1_ref[...]) + _dot(a2_ref[...], w2_ref[...])
    o_ref[...] = r_ref[...] + acc


def proj2_residual(a1, a2, w, res, *, tm, tn):
    M, K1 = a1.shape
    K2 = a2.shape[1]
    assert K1 == K2
    N = w.shape[1]
    tm = min(tm, M)
    tn = min(tn, N)
    return pl.pallas_call(
        _proj2_res_kernel,
        grid=(M // tm, N // tn),
        in_specs=[pl.BlockSpec((tm, K1), lambda i, j: (i, 0)),
                  pl.BlockSpec((tm, K2), lambda i, j: (i, 0)),
                  pl.BlockSpec((K1, tn), lambda i, j: (0, j)),
                  pl.BlockSpec((K2, tn), lambda i, j: (1, j)),
                  pl.BlockSpec((tm, tn), lambda i, j: (i, j))],
        out_specs=pl.BlockSpec((tm, tn), lambda i, j: (i, j)),
        out_shape=jax.ShapeDtypeStruct((M, N), F32),
        compiler_params=_params("parallel", "arbitrary"),
        name="proj2_residual",
    )(a1, a2, w, w, res)


def _mlp_kernel(x_ref, g_ref, w1_ref, w2_ref, o_ref, hn_ref, acc_ref):
    j = pl.program_id(1)

    last = pl.num_programs(1) - 1

    @pl.when(j == 0)
    def _():
        x = x_ref[...]
        hn_ref[...] = _rms(x, g_ref[...]).astype(BF16)
        acc_ref[...] = x

    def accumulated():
        h1 = jnp.maximum(_dot(hn_ref[...], w1_ref[...]), 0.0)
        return acc_ref[...] + _dot((h1 * h1).astype(BF16), w2_ref[...])

    @pl.when(j != last)
    def _():
        acc_ref[...] = accumulated()

    @pl.when(j == last)
    def _():
        o_ref[...] = accumulated()


def mlp_residual(x, g, w1, w2, *, tm, th):
    M, D = x.shape
    H = w1.shape[1]
    tm = min(tm, M)
    th = min(th, H)
    return pl.pallas_call(
        _mlp_kernel,
        grid=(M // tm, H // th),
        in_specs=[pl.BlockSpec((tm, D), lambda i, j: (i, 0)),
                  pl.BlockSpec((1, D), lambda i, j: (0, 0)),
                  pl.BlockSpec((D, th), lambda i, j: (0, j)),
                  pl.BlockSpec((th, D), lambda i, j: (j, 0))],
        out_specs=pl.BlockSpec((tm, D), lambda i, j: (i, 0)),
        out_shape=jax.ShapeDtypeStruct((M, D), F32),
        scratch_shapes=[pltpu.VMEM((tm, D), BF16), pltpu.VMEM((tm, D), F32)],
        compiler_params=_params("parallel", "arbitrary"),
        name="mlp_residual",
    )(x, g.reshape(1, D), w1, w2)


def _cross_attn_kernel(x_ref, g_ref, wq_ref, kv_ref, qg_ref, kg_ref, wo_ref, o_ref):
    x = x_ref[0]
    hn = _rms(x, g_ref[...]).astype(BF16)
    q = _dot(hn, wq_ref[...])
    kv = kv_ref[0]
    scale = CA_HEAD_DIM ** -0.5 * LOG2E
    ones = jnp.ones((kv.shape[0], CA_HEAD_DIM), BF16)
    outs = []
    for h in range(CA_HEADS):
        sl = slice(h * CA_HEAD_DIM, (h + 1) * CA_HEAD_DIM)
        qh = (_rms(q[:, sl], qg_ref[...]) * scale).astype(BF16)
        kh = _rms(kv[:, sl], kg_ref[...]).astype(BF16)
        vh = kv[:, CA_WIDTH + h * CA_HEAD_DIM:CA_WIDTH + (h + 1) * CA_HEAD_DIM].astype(BF16)
        s = _dot_nt(qh, kh)
        e = jnp.exp2(s - jnp.max(s, axis=-1, keepdims=True)).astype(BF16)
        res = _dot(e, jnp.concatenate([vh, ones], axis=1))
        outs.append(res[:, :CA_HEAD_DIM] * (1.0 / res[:, CA_HEAD_DIM:CA_HEAD_DIM + 1]))
    o = jnp.concatenate(outs, axis=-1).astype(BF16)
    o_ref[0] = x + _dot(o, wo_ref[...])


def cross_attention(x, kv, g, wq, qg, kg, wo, *, tq):
    B, L, D = x.shape
    M = kv.shape[1]
    tq = min(tq, L)
    return pl.pallas_call(
        _cross_attn_kernel,
        grid=(B, L // tq),
        in_specs=[pl.BlockSpec((1, tq, D), lambda b, i: (b, i, 0)),
                  pl.BlockSpec((1, D), lambda b, i: (0, 0)),
                  pl.BlockSpec((D, CA_WIDTH), lambda b, i: (0, 0)),
                  pl.BlockSpec((1, M, 2 * CA_WIDTH), lambda b, i: (b, 0, 0)),
                  pl.BlockSpec((1, CA_HEAD_DIM), lambda b, i: (0, 0)),
                  pl.BlockSpec((1, CA_HEAD_DIM), lambda b, i: (0, 0)),
                  pl.BlockSpec((CA_WIDTH, D), lambda b, i: (0, 0))],
        out_specs=pl.BlockSpec((1, tq, D), lambda b, i: (b, i, 0)),
        out_shape=jax.ShapeDtypeStruct((B, L, D), F32),
        compiler_params=_params("parallel", "arbitrary"),
        name="cross_attention",
    )(x, g.reshape(1, D), wq, kv, qg.reshape(1, -1), kg.reshape(1, -1), wo)


def _half_rms(x, g2, lo_mask):
    sq = x * x
    ms_lo = jnp.sum(jnp.where(lo_mask, sq, 0.0), axis=-1, keepdims=True)
    ms_hi = jnp.sum(jnp.where(lo_mask, 0.0, sq), axis=-1, keepdims=True)
    inv = jnp.where(lo_mask, lax.rsqrt(ms_lo * (1.0 / DF_HEAD_DIM) + NORM_EPS),
                    lax.rsqrt(ms_hi * (1.0 / DF_HEAD_DIM) + NORM_EPS))
    return x * inv * g2


def _diff_attn_kernel(sc_ref, q_ref, k_ref, v_ref, qg_ref, kg_ref, sg_ref, o_ref, kn_ref, vb_ref, *,
                      out_scale):
    h = pl.program_id(1)
    qi = pl.program_id(2)
    tq = q_ref.shape[1]
    L = k_ref.shape[1]
    lane = lax.broadcasted_iota(jnp.int32, (1, LANES), 1)
    lo_mask = lane < DF_HEAD_DIM

    def prepare_kv():
        kn_ref[...] = _half_rms(k_ref[0], kg_ref[...], lo_mask).astype(BF16)
        vb_ref[:, :LANES] = v_ref[0].astype(BF16)
        vb_ref[:, LANES:] = jnp.ones((L, LANES), BF16)

    if tq == L:
        prepare_kv()
    else:
        pl.when(qi == 0)(prepare_kv)

    lam = sc_ref[0]
    slope = sc_ref[1 + h] * LOG2E
    ts = min(ATTN_STRIP, tq)
    col = lax.broadcasted_iota(jnp.int32, (1, L), 1).astype(F32)

    def scores(i):
        r0 = i * ts
        qn = _half_rms(q_ref[0, r0:r0 + ts, :], qg_ref[...], lo_mask) * (DF_HEAD_DIM ** -0.5 * LOG2E)
        row = (lax.broadcasted_iota(jnp.int32, (ts, 1), 0) + (qi * tq + r0)).astype(F32)
        bias = jnp.abs(row - col) * slope
        kn = kn_ref[...]
        return (_dot_nt(jnp.where(lo_mask, qn, 0.0).astype(BF16), kn) - bias,
                _dot_nt(jnp.where(lo_mask, 0.0, qn).astype(BF16), kn) - bias)

    def exps(i, ss):
        return tuple(jnp.exp2(s - jnp.max(s, axis=-1, keepdims=True)).astype(BF16) for s in ss)

    def values(i, es):
        r1, r2 = (_dot(e, vb_ref[...]) for e in es)
        o = (r1[:, :LANES] * (1.0 / r1[:, LANES:LANES + 1])
             - r2[:, :LANES] * (lam / r2[:, LANES:LANES + 1]))
        o_ref[0, i * ts:(i + 1) * ts, :] = (_rms(o, sg_ref[...]) * out_scale).astype(o_ref.dtype)

    _pipeline3(tq // ts, scores, exps, values)


def diff_attention(proj, scal, q_g, k_g, subln_g, *, q_blk0, k_blk0, v_blk0, out_scale, tq):
    B, L, _ = proj.shape
    tq = min(tq, L)
    qg2 = jnp.concatenate([q_g, q_g]).reshape(1, LANES)
    kg2 = jnp.concatenate([k_g, k_g]).reshape(1, LANES)
    return pl.pallas_call(
        functools.partial(_diff_attn_kernel, out_scale=out_scale),
        grid=(B, DF_HEADS, L // tq),
        in_specs=[pl.BlockSpec(memory_space=pltpu.SMEM),
                  pl.BlockSpec((1, tq, LANES), lambda b, h, i: (b, i, q_blk0 + h)),
                  pl.BlockSpec((1, L, LANES), lambda b, h, i: (b, 0, k_blk0 + h)),
                  pl.BlockSpec((1, L, LANES), lambda b, h, i: (b, 0, v_blk0 + h)),
                  pl.BlockSpec((1, LANES), lambda b, h, i: (0, 0)),
                  pl.BlockSpec((1, LANES), lambda b, h, i: (0, 0)),
                  pl.BlockSpec((1, LANES), lambda b, h, i: (0, 0))],
        out_specs=pl.BlockSpec((1, tq, LANES), lambda b, h, i: (b, i, h)),
        out_shape=jax.ShapeDtypeStruct((B, L, DF_HEADS * DF_V_DIM), BF16),
        scratch_shapes=[pltpu.VMEM((L, LANES), BF16), pltpu.VMEM((L, 2 * LANES), BF16)],
        compiler_params=_params("parallel", "parallel", "arbitrary"),
        name="diff_attention",
    )(scal, proj, proj, proj, qg2, kg2, subln_g.reshape(1, LANES))


def _rope(x, cos, sin, first_mask):
    q = GQ_HEAD_DIM // 4
    rot = jnp.where(first_mask, -pltpu.roll(x, LANES - q, axis=1), pltpu.roll(x, q, axis=1))
    return x * cos + rot * sin


def _gqa_kernel(q_ref, k_ref, v_ref, cq_ref, sq_ref, ck_ref, sk_ref, qg_ref, kg_ref, o_ref, kn_ref, vb_ref):
    qi = pl.program_id(2)
    lane = lax.broadcasted_iota(jnp.int32, (1, LANES), 1)
    first_mask = (lane % (GQ_HEAD_DIM // 2)) < (GQ_HEAD_DIM // 4)

    @pl.when(qi == 0)
    def _():
        kn = _rms(k_ref[0], kg_ref[...])
        kn_ref[...] = _rope(kn, ck_ref[...], sk_ref[...], first_mask).astype(BF16)
        vb_ref[:, :LANES] = v_ref[0].astype(BF16)
        vb_ref[:, LANES:] = jnp.ones((k_ref.shape[1], LANES), BF16)

    scale = GQ_HEAD_DIM ** -0.5 * LOG2E
    tq = q_ref.shape[1]
    ts = min(ATTN_STRIP, tq)
    nr = tq // ts

    def where(i):
        r0 = (i % nr) * ts
        return slice(r0, r0 + ts), slice((i // nr) * GQ_HEAD_DIM, (i // nr + 1) * GQ_HEAD_DIM)

    def scores(i):
        rs, cs = where(i)
        qn = _rope(_rms(q_ref[0, rs, cs], qg_ref[...]), cq_ref[rs, :], sq_ref[rs, :], first_mask) * scale
        return _dot_nt(qn.astype(BF16), kn_ref[...])

    def exps(i, s):
        return jnp.exp2(s - jnp.max(s, axis=-1, keepdims=True)).astype(BF16)

    def values(i, e):
        rs, cs = where(i)
        res = _dot(e, vb_ref[...])
        o_ref[0, rs, cs] = (res[:, :LANES] * (1.0 / res[:, LANES:LANES + 1])).astype(o_ref.dtype)

    _pipeline3(GQ_GROUP * nr, scores, exps, values)


def gqa_attention(proj, cos, sin, q_g, k_g, *, q_blk0, k_blk0, v_blk0, tq):
    B, L, _ = proj.shape
    tq = min(tq, L)
    gw = GQ_GROUP * GQ_HEAD_DIM
    return pl.pallas_call(
        _gqa_kernel,
        grid=(B, GQ_KV_HEADS, L // tq),
        in_specs=[pl.BlockSpec((1, tq, gw), lambda b, g, i: (b, i, q_blk0 + g)),
                  pl.BlockSpec((1, L, LANES), lambda b, g, i: (b, 0, k_blk0 + g)),
                  pl.BlockSpec((1, L, LANES), lambda b, g, i: (b, 0, v_blk0 + g)),
                  pl.BlockSpec((tq, LANES), lambda b, g, i: (i, 0)),
                  pl.BlockSpec((tq, LANES), lambda b, g, i: (i, 0)),
                  pl.BlockSpec((L, LANES), lambda b, g, i: (0, 0)),
                  pl.BlockSpec((L, LANES), lambda b, g, i: (0, 0)),
                  pl.BlockSpec((1, LANES), lambda b, g, i: (0, 0)),
                  pl.BlockSpec((1, LANES), lambda b, g, i: (0, 0))],
        out_specs=pl.BlockSpec((1, tq, gw), lambda b, g, i: (b, i, g)),
        out_shape=jax.ShapeDtypeStruct((B, L, GQ_WIDTH), BF16),
        scratch_shapes=[pltpu.VMEM((L, LANES), BF16), pltpu.VMEM((L, 2 * LANES), BF16)],
        compiler_params=_params("parallel", "parallel", "arbitrary"),
        name="gqa_attention",
    )(proj, proj, proj, cos, sin, cos, sin, q_g.reshape(1, LANES), k_g.reshape(1, LANES))


def _hgrn2_pair_masks():
    pos = lax.broadcasted_iota(jnp.int32, (1, HG_BLOCK, LANES), 1)
    return {rev: [jnp.where((pos <= s) if rev else (pos >= s), 0.0, -jnp.inf).astype(F32)
                  for s in range(HG_BLOCK)] for rev in (False, True)}


def _hgrn2_intra_streams(streams, b_scr, pair_mask):
    C = HG_CHUNK
    nb = C // HG_BLOCK
    ns = range(len(streams))
    rev = [st[5] for st in streams]
    r_i = lax.broadcasted_iota(jnp.int32, (C, C), 0)
    c_i = lax.broadcasted_iota(jnp.int32, (C, C), 1)

    kk, lg = [], []
    for q, fz, v, vb, lb, _ in streams:
        f = lb + (1.0 - lb) * jax.nn.sigmoid(fz)
        kk.append(1.0 - f)
        lg.append(jnp.log(f) * LOG2E)

    tri_f = jnp.where(r_i >= c_i, 1.0, 0.0).astype(BF16)
    tri_b = jnp.where(r_i <= c_i, 1.0, 0.0).astype(BF16)
    b = []
    for i in ns:
        hi = lg[i].astype(BF16)
        r1 = lg[i] - hi.astype(F32)
        mid = r1.astype(BF16)
        lo = (r1 - mid.astype(F32)).astype(BF16)
        tri = tri_b if rev[i] else tri_f
        b.append(_dot(tri, hi) + _dot(tri, mid) + _dot(tri, lo))
    b_end = [b[i][0:1, :] if rev[i] else b[i][C - 1:C, :] for i in ns]
    qe = [(streams[i][0] * jnp.exp2(b[i])).astype(BF16) for i in ns]
    u = [_dot_tn(streams[i][3], (kk[i] * jnp.exp2(b_end[i] - b[i])).astype(BF16)) for i in ns]

    b3 = [b[i].reshape(nb, HG_BLOCK, LANES) for i in ns]
    q3 = [streams[i][0].reshape(nb, HG_BLOCK, LANES) for i in ns]
    seg_pos = lax.broadcasted_iota(jnp.int32, (C, HG_BLOCK * LANES), 1) // LANES
    row_pos = lax.broadcasted_iota(jnp.int32, (C, HG_BLOCK * LANES), 0) % HG_BLOCK
    kcat = []
    for i in ns:
        kb = kk[i].astype(BF16)
        kcat.append(jnp.where(seg_pos == row_pos, jnp.concatenate([kb] * HG_BLOCK, axis=1),
                              jnp.zeros((), BF16)))
    for i in ns:
        b_scr[i] = b[i]
    ps = [[] for i in ns]
    for s in range(HG_BLOCK):
        for i in ns:
            key_b = jnp.concatenate(
                [jnp.broadcast_to(b_scr[i, blk * HG_BLOCK + s:blk * HG_BLOCK + s + 1, :], (HG_BLOCK, LANES))
                 for blk in range(nb)], axis=0).reshape(nb, HG_BLOCK, LANES)
            d = (b3[i] - key_b) + pair_mask[rev[i]][s]
            ps[i].append((q3[i] * jnp.exp2(d)).reshape(C, LANES).astype(BF16))
    same_blk = (r_i // HG_BLOCK) == (c_i // HG_BLOCK)
    adiag = [_dot_nt(jnp.concatenate(ps[i], axis=1), kcat[i]) for i in ns]

    zrow = jnp.zeros((1, 1, LANES), F32)
    qt = []
    for i in ns:
        if rev[i]:
            rblk3 = jnp.concatenate([b3[i][1:, 0:1, :], zrow], axis=0)
        else:
            rblk3 = jnp.concatenate([zrow, b3[i][:nb - 1, HG_BLOCK - 1:HG_BLOCK, :]], axis=0)
        qt.append((q3[i] * jnp.exp2(b3[i] - rblk3)).reshape(C, LANES))
    rows = [[] for i in ns]
    for blk in range(nb):
        for i in ns:
            if rev[i]:
                edge = (blk + 1) * HG_BLOCK
                if edge == C:
                    rows[i].append(jnp.zeros((HG_BLOCK, C), F32))
                    continue
                kt = kk[i][edge:, :] * jnp.exp2(b[i][edge:edge + 1, :] - b[i][edge:, :])
                kfull = jnp.concatenate([jnp.zeros((edge, LANES), F32), kt], axis=0)
            else:
                edge = blk * HG_BLOCK
                if edge == 0:
                    rows[i].append(jnp.zeros((HG_BLOCK, C), F32))
                    continue
                kt = kk[i][:edge, :] * jnp.exp2(b[i][edge - 1:edge, :] - b[i][:edge, :])
                kfull = jnp.concatenate([kt, jnp.zeros((C - edge, LANES), F32)], axis=0)
            rows[i].append(_dot_nt(qt[i][blk * HG_BLOCK:(blk + 1) * HG_BLOCK, :].astype(BF16),
                                   kfull.astype(BF16)))
    outs = []
    for i in ns:
        att = jnp.where(same_blk, adiag[i], jnp.concatenate(rows[i], axis=0))
        o = _dot(att.astype(BF16), streams[i][3])
        outs.append((o, qe[i], u[i], jnp.exp2(b_end[i])))
    return outs


def _hgrn2_kernel(q_ref, ff_ref, fb_ref, i_ref, g_ref, lbf_ref, lbb_ref, ng_ref, o_ref, of_ref, ob_ref,
                  sf_ref, sb_ref, b_scr):
    L = q_ref.shape[1]
    C = HG_CHUNK
    n = L // C
    gsize = math.gcd(HG_INTRA_CHUNKS, n)
    qscale = HG_DK ** -0.5

    def rows(c):
        return pl.ds(pl.multiple_of(c * C, C), C)

    sf_ref[...] = jnp.zeros_like(sf_ref)
    sb_ref[...] = jnp.zeros_like(sb_ref)
    pair_mask = _hgrn2_pair_masks()

    def group(it, carry):
        streams = []
        dests = []
        for k in range(gsize):
            c = it * gsize + k
            for fz_ref, lb_ref, o_s, s_s, cc, reverse in ((ff_ref, lbf_ref, of_ref, sf_ref, c, False),
                                                          (fb_ref, lbb_ref, ob_ref, sb_ref, n - 1 - c, True)):
                r = rows(cc)
                v = i_ref[0, r, :]
                streams.append((q_ref[0, r, :] * qscale, fz_ref[0, r, :], v, v.astype(BF16), lb_ref[0], reverse))
                dests.append((o_s, s_s, r))
        state = {id(sf_ref): sf_ref[...], id(sb_ref): sb_ref[...]}
        for (o, qe, u, e), (o_s, s_s, r) in zip(_hgrn2_intra_streams(streams, b_scr, pair_mask), dests):
            st = state[id(s_s)]
            o_s[r, :] = o + _dot_nt(qe, st.astype(BF16))
            state[id(s_s)] = st * e + u
        sf_ref[...] = state[id(sf_ref)]
        sb_ref[...] = state[id(sb_ref)]
        return carry

    lax.fori_loop(0, n // gsize, group, 0)
    o = _rms(of_ref[...] + ob_ref[...], ng_ref[...])
    g = g_ref[0]
    o_ref[0] = (o * (g * jax.nn.sigmoid(g))).astype(o_ref.dtype)


def hgrn2_mixer(proj, lb_f, lb_b, norm_g):
    B, L, _ = proj.shape
    H = HG_HEADS

    def col(k):
        return pl.BlockSpec((1, L, LANES), lambda b, h: (b, 0, k * H + h))

    lbspec = pl.BlockSpec((1, 1, LANES), lambda b, h: (h, 0, 0))
    return pl.pallas_call(
        _hgrn2_kernel,
        grid=(B, H),
        in_specs=[col(0), col(1), col(2), col(3), col(4), lbspec, lbspec,
                  pl.BlockSpec((1, LANES), lambda b, h: (0, 0))],
        out_specs=pl.BlockSpec((1, L, LANES), lambda b, h: (b, 0, h)),
        out_shape=jax.ShapeDtypeStruct((B, L, HG_WIDTH), BF16),
        scratch_shapes=[pltpu.VMEM((L, LANES), F32), pltpu.VMEM((L, LANES), F32),
                        pltpu.VMEM((LANES, LANES), F32), pltpu.VMEM((LANES, LANES), F32),
                        pltpu.VMEM((2 * math.gcd(HG_INTRA_CHUNKS, L // HG_CHUNK), HG_CHUNK, LANES), F32)],
        compiler_params=_params("parallel", "arbitrary"),
        name="hgrn2",
    )(proj, proj, proj, proj, proj, lb_f.reshape(H, 1, LANES), lb_b.reshape(H, 1, LANES),
      norm_g.reshape(1, LANES))


def _hyena_filter_kernel(z_ref, t_ref, w1_ref, b1_ref, w2_ref, b2_ref, w3_ref, b3_ref, w4f_ref, w4b_ref,
                         fr_ref, ad_ref, gs_ref, gd_ref, kn_ref):
    def hdot(a, b):
        return jnp.dot(a, b, precision=HIGHEST, preferred_element_type=F32)

    fr = fr_ref[...]
    h = jnp.sin(fr * (hdot(z_ref[...], w1_ref[...]) + b1_ref[...]))
    h = jnp.sin(fr * (hdot(h, w2_ref[...]) + b2_ref[...]))
    h = jnp.sin(fr * (hdot(h, w3_ref[...]) + b3_ref[...]))
    window = jnp.exp(-t_ref[...] * ad_ref[...])
    hf = hdot(h, w4f_ref[...]) * window
    hb = hdot(h, w4b_ref[...]) * window
    row = lax.broadcasted_iota(jnp.int32, hb.shape, 0)
    hb = jnp.where(row == 0, 0.0, hb)
    gs_ref[...] = hf + hb
    gd_ref[...] = hb - hf
    sgn = jnp.where(row % 2 == 0, 1.0, -1.0)
    kn_ref[...] = jnp.sum((hf + hb) * sgn, axis=0, keepdims=True) * (0.5 / hb.shape[0])


def _split2(x):
    hi = x.astype(BF16)
    return hi, (x - hi.astype(F32)).astype(BF16)


def _hyena_spectrum_kernel(ah_ref, al_ref, bh_ref, bl_ref, gs_ref, gd_ref, kre_ref, kim_ref, sh_ref, sl_ref,
                           dh_ref, dl_ref, *, n_fft):
    @pl.when(pl.program_id(1) == 0)
    def _():
        sh_ref[...], sl_ref[...] = _split2(gs_ref[...])
        dh_ref[...], dl_ref[...] = _split2(gd_ref[...])

    tf = ah_ref.shape[0]
    f0 = pl.program_id(1) * tf
    row = lax.broadcasted_iota(jnp.int32, (tf, 1), 0) + f0
    wf = jnp.where(row == 0, 1.0 / n_fft, 2.0 / n_fft)
    ah = ah_ref[...]
    bh = bh_ref[...]
    kre_ref[...] = wf * (_dot(ah, sh_ref[...]) + _dot(ah, sl_ref[...]) + _dot(al_ref[...], sh_ref[...]))
    kim_ref[...] = wf * (_dot(bh, dh_ref[...]) + _dot(bh, dl_ref[...]) + _dot(bl_ref[...], dh_ref[...]))


def hyena_filter_spectrum(L, cos_hi, cos_lo, sin_hi, sin_lo, w1, b1, w2, b2, w3, b3, w4, sin_freq):
    C = HY_WIDTH
    t = jnp.linspace(0.0, 1.0, L, dtype=F32)[:, None]
    w = (2.0 * math.pi / L) * jnp.arange(L, dtype=F32)[:, None]
    f = jnp.linspace(1e-4, HY_BANDS - 1, HY_BANDS, dtype=F32)[None, :]
    z = jnp.concatenate([t, jnp.cos(f * w), -jnp.sin(f * w)], axis=-1)
    zp = jnp.pad(z, ((0, 0), (0, LANES - HY_EMB)))
    w1p = jnp.pad(w1, ((0, LANES - HY_EMB), (0, 0)))
    max_decay = math.log(HY_TARGET) / HY_FAST_PCT
    min_decay = math.log(HY_TARGET) / HY_SLOW_PCT
    absd = jnp.abs(jnp.linspace(min_decay, max_decay, C, dtype=F32))[None, :]
    tc = min(512, C)
    full = lambda shape: pl.BlockSpec(shape, lambda j: (0, 0))
    gs, gd, knyq = pl.pallas_call(
        _hyena_filter_kernel,
        grid=(C // tc,),
        in_specs=[full((L, LANES)), full((L, 1)), full((LANES, HY_FILT)), full((1, HY_FILT)),
                  full((HY_FILT, HY_FILT)), full((1, HY_FILT)), full((HY_FILT, HY_FILT)), full((1, HY_FILT)),
                  pl.BlockSpec((HY_FILT, tc), lambda j: (0, j)),
                  pl.BlockSpec((HY_FILT, tc), lambda j: (0, C // tc + j)),
                  full((1, HY_FILT)),
                  pl.BlockSpec((1, tc), lambda j: (0, j))],
        out_specs=[pl.BlockSpec((L, tc), lambda j: (0, j)), pl.BlockSpec((L, tc), lambda j: (0, j)),
                   pl.BlockSpec((1, tc), lambda j: (0, j))],
        out_shape=[jax.ShapeDtypeStruct((L, C), F32), jax.ShapeDtypeStruct((L, C), F32),
                   jax.ShapeDtypeStruct((1, C), F32)],
        compiler_params=_params("arbitrary"),
        name="hyena_filter",
    )(zp, t, w1p, b1.reshape(1, -1), w2, b2.reshape(1, -1), w3, b3.reshape(1, -1), w4, w4,
      sin_freq.reshape(1, -1), absd)

    tf = min(256, L)
    tbl = pl.BlockSpec((tf, L), lambda j, i: (i, 0))
    kre, kim = pl.pallas_call(
        functools.partial(_hyena_spectrum_kernel, n_fft=2 * L),
        grid=(C // tc, L // tf),
        in_specs=[tbl, tbl, tbl, tbl,
                  pl.BlockSpec((L, tc), lambda j, i: (0, j)),
                  pl.BlockSpec((L, tc), lambda j, i: (0, j))],
        out_specs=[pl.BlockSpec((tf, tc), lambda j, i: (i, j)), pl.BlockSpec((tf, tc), lambda j, i: (i, j))],
        out_shape=[jax.ShapeDtypeStruct((L, C), F32), jax.ShapeDtypeStruct((L, C), F32)],
        scratch_shapes=[pltpu.VMEM((L, tc), BF16)] * 4,
        compiler_params=_params("parallel", "arbitrary"),
        name="hyena_spectrum",
    )(cos_hi, cos_lo, sin_hi, sin_lo, gs, gd)
    return kre, kim, knyq


def _short_conv(u, w_ref, b_ref):
    L = u.shape[0]
    row = lax.broadcasted_iota(jnp.int32, (L, 1), 0)
    prev = jnp.where(row == 0, 0.0, pltpu.roll(u, 1, axis=0))
    nxt = jnp.where(row == L - 1, 0.0, pltpu.roll(u, L - 1, axis=0))
    return prev * w_ref[0:1, :] + u * w_ref[1:2, :] + nxt * w_ref[2:3, :] + b_ref[...]


def _hyena_conv_kernel(x1_ref, x2_ref, v_ref, cw1_ref, cw2_ref, cw3_ref, cb1_ref, cb2_ref, cb3_ref,
                       a_ref, b_ref, kre_ref, kim_ref, knyq_ref, bias_ref, o_ref, r1_ref, r2_ref, *, tf):
    L = v_ref.shape[1]
    x2 = _short_conv(x2_ref[0], cw2_ref, cb2_ref)
    vv = _short_conv(v_ref[0], cw3_ref, cb3_ref)
    vx = vv * x2
    vxb = vx.astype(BF16)
    sgn = jnp.where(lax.broadcasted_iota(jnp.int32, (L, 1), 0) % 2 == 0, 1.0, -1.0)
    def forward(fb):
        fs = slice(fb * tf, (fb + 1) * tf)
        return _dot(a_ref[fs, :], vxb), _dot(b_ref[fs, :], vxb)

    def product(fb, pq):
        p, q = pq
        fs = slice(fb * tf, (fb + 1) * tf)
        kre = kre_ref[fs, :]
        kim = kim_ref[fs, :]
        r1_ref[fs, :] = (p * kre + q * kim).astype(BF16)
        r2_ref[fs, :] = (q * kre - p * kim).astype(BF16)

    _pipeline3(L // tf, forward, product, lambda fb, _: None)

    v_nyq = jnp.sum(vx * sgn, axis=0, keepdims=True)
    y0 = vx * bias_ref[...] + sgn * (v_nyq * knyq_ref[...])
    x1 = _short_conv(x1_ref[0], cw1_ref, cb1_ref)
    y = _dot(a_ref[...], r1_ref[...]) + _dot(b_ref[...], r2_ref[...])
    o_ref[0] = ((y0 + y) * x1).astype(o_ref.dtype)


def hyena_conv(proj, conv_w, conv_b, cos_b, sin_b, kre, kim, knyq, bias, *, tc):
    B, L, _ = proj.shape
    C = HY_WIDTH
    tc = min(tc, C)
    nc = C // tc
    tf = min(512, L)

    def grp(k):
        return pl.BlockSpec((1, L, tc), lambda c, b: (b, 0, k * nc + c))

    def cw(k):
        return pl.BlockSpec((HY_SHORT, tc), lambda c, b: (0, k * nc + c))

    def cb(k):
        return pl.BlockSpec((1, tc), lambda c, b: (0, k * nc + c))

    tbl = pl.BlockSpec((L, L), lambda c, b: (0, 0), pipeline_mode=pl.Buffered(1))
    chan = pl.BlockSpec((L, tc), lambda c, b: (0, c), pipeline_mode=pl.Buffered(1))
    vec = pl.BlockSpec((1, tc), lambda c, b: (0, c))
    return pl.pallas_call(
        functools.partial(_hyena_conv_kernel, tf=tf),
        grid=(nc, B),
        in_specs=[grp(0), grp(1), grp(2), cw(0), cw(1), cw(2), cb(0), cb(1), cb(2),
                  tbl, tbl, chan, chan, vec, vec],
        out_specs=pl.BlockSpec((1, L, tc), lambda c, b: (b, 0, c)),
        out_shape=jax.ShapeDtypeStruct((B, L, C), BF16),
        scratch_shapes=[pltpu.VMEM((L, tc), BF16), pltpu.VMEM((L, tc), BF16)],
        compiler_params=_params("parallel", "arbitrary"),
        name="hyena_conv",
    )(proj, proj, proj, conv_w, conv_w, conv_w, conv_b.reshape(1, -1), conv_b.reshape(1, -1),
      conv_b.reshape(1, -1), cos_b, sin_b, kre, kim, knyq, bias.reshape(1, -1))


def _dft_tables(L):
    n = 2 * L
    r = math.gcd(GRID_W, L)
    j = jnp.arange(L, dtype=jnp.int32)[None, :]

    def cs(f):
        ang = ((f[:, None] * j) % n).astype(F32) * (2.0 * math.pi / n)
        return jnp.cos(ang), jnp.sin(ang)

    c0, s0 = cs(jnp.arange(r, dtype=jnp.int32))
    c1, s1 = cs(jnp.arange(L // r, dtype=jnp.int32) * r)
    cos_m = (c1[:, None, :] * c0[None, :, :] - s1[:, None, :] * s0[None, :, :]).reshape(L, L)
    sin_m = (s1[:, None, :] * c0[None, :, :] + c1[:, None, :] * s0[None, :, :]).reshape(L, L)
    cos_hi = cos_m.astype(BF16)
    sin_hi = sin_m.astype(BF16)
    return (cos_hi, (cos_m - cos_hi.astype(F32)).astype(BF16),
            sin_hi, (sin_m - sin_hi.astype(F32)).astype(BF16))


def _axial_rope_tables(L):
    rows = L // GRID_W
    r, c = jnp.meshgrid(jnp.arange(rows, dtype=F32), jnp.arange(GRID_W, dtype=F32), indexing='ij')
    r = r.reshape(-1)
    c = c.reshape(-1)
    half = GQ_HEAD_DIM // 2
    inv = ROPE_THETA ** (-jnp.arange(0, half, 2, dtype=F32) / half)
    ang_r = r[:, None] * inv[None, :]
    ang_c = c[:, None] * inv[None, :]
    ang = jnp.concatenate([ang_r, ang_r, ang_c, ang_c], axis=-1)
    return jnp.cos(ang), jnp.sin(ang)


def _even_mixer(x, layer, norm_g, w_in, w_out, conv_w, conv_b, fw1, fb1, fw2, fb2, fw3, fb3, fw4, sin_freq,
                hy_bias, q_g, k_g, lam_q1, lam_k1, lam_q2, lam_k2, subln_g):
    B, L, D = x.shape
    x2 = x.reshape(B * L, D)
    proj = norm_matmul(x2, norm_g, w_in.astype(BF16), tm=1024, tn=1536).reshape(B, L, -1)
    cos_hi, cos_lo, sin_hi, sin_lo = _dft_tables(L)
    kre, kim, knyq = hyena_filter_spectrum(L, cos_hi, cos_lo, sin_hi, sin_lo, fw1, fb1, fw2, fb2, fw3, fb3, fw4,
                                           sin_freq)
    o_a = hyena_conv(proj, conv_w, conv_b, cos_hi, sin_hi, kre, kim, knyq, hy_bias,
                     tc=256)
    lam_init = 0.8 - 0.6 * math.exp(-0.3 * layer)
    lam = (jnp.exp(jnp.sum(lam_q1.astype(F32) * lam_k1.astype(F32)))
           - jnp.exp(jnp.sum(lam_q2.astype(F32) * lam_k2.astype(F32))) + lam_init)
    slopes = jnp.asarray(np.array([2.0 ** (-8.0 * (h + 1) / DF_HEADS) for h in range(DF_HEADS)],
                                  dtype=np.float32))
    scal = jnp.concatenate([lam.reshape(1), slopes]).astype(F32)
    nb_hy = 3 * HY_WIDTH // LANES
    nb_qk = DF_QK_WIDTH // LANES
    o_b = diff_attention(proj, scal, q_g, k_g, subln_g, q_blk0=nb_hy, k_blk0=nb_hy + nb_qk,
                         v_blk0=nb_hy + 2 * nb_qk, out_scale=1.0 - lam_init, tq=2048)
    out = proj2_residual(o_a.reshape(B * L, -1), o_b.reshape(B * L, -1), w_out.astype(BF16), x2,
                         tm=1024, tn=1024)
    return out.reshape(B, L, D)


def _odd_mixer(x, norm_g, lb_f, lb_b, w_in, w_out, hg_norm_g, q_g, k_g):
    B, L, D = x.shape
    x2 = x.reshape(B * L, D)
    proj = norm_matmul(x2, norm_g, w_in.astype(BF16), tm=1024, tn=1664).reshape(B, L, -1)
    o_c = hgrn2_mixer(proj, lb_f, lb_b, hg_norm_g)
    cos, sin = _axial_rope_tables(L)
    gw = GQ_GROUP * GQ_HEAD_DIM
    q0 = 5 * HG_WIDTH
    k0 = q0 + GQ_WIDTH
    v0 = k0 + GQ_KV_HEADS * GQ_HEAD_DIM
    o_d = gqa_attention(proj, cos, sin, q_g, k_g, q_blk0=q0 // gw, k_blk0=k0 // LANES, v_blk0=v0 // LANES,
                        tq=1024)
    out = proj2_residual(o_c.reshape(B * L, -1), o_d.reshape(B * L, -1), w_out.astype(BF16), x2,
                         tm=1024, tn=1024)
    return out.reshape(B, L, D)


def kernel(x, mem, norm_mix_g, norm_mem_q_g, norm_mem_kv_g, norm_ffn_g, ev_w_in, ev_w_out, hy_conv_w, hy_conv_b, hy_fw1, hy_fb1, hy_fw2, hy_fb2, hy_fw3, hy_fb3, hy_fw4, hy_sin_freq, hy_bias, df_q_g, df_k_g, df_lam_q1, df_lam_k1, df_lam_q2, df_lam_k2, df_subln_g, od_w_in, od_w_out, hg_lb_logits, hg_norm_g, gq_q_g, gq_k_g, ca_w_q, ca_w_kv, ca_w_out, ca_q_g, ca_k_g, mlp_w1, mlp_w2):
    B, L, D = x.shape
    M = mem.shape[1]
    depth = norm_mix_g.shape[0]
    lb_soft = jax.nn.softmax(hg_lb_logits.astype(F32), axis=1)
    lbs = jnp.cumsum(lb_soft, axis=1)
    lbs = lbs - lbs[:, :1]
    mem2 = mem.reshape(B * M, D)
    for layer in range(depth):
        j = layer // 2
        if layer % 2 == 0:
            x = _even_mixer(x, layer, norm_mix_g[layer], ev_w_in[j], ev_w_out[j], hy_conv_w[j], hy_conv_b[j],
                            hy_fw1[j], hy_fb1[j], hy_fw2[j], hy_fb2[j], hy_fw3[j], hy_fb3[j], hy_fw4[j],
                            hy_sin_freq[j], hy_bias[j], df_q_g[j], df_k_g[j], df_lam_q1[j], df_lam_k1[j],
                            df_lam_q2[j], df_lam_k2[j], df_subln_g[j])
        else:
            x = _odd_mixer(x, norm_mix_g[layer], lbs[0, layer], lbs[1, layer], od_w_in[j], od_w_out[j],
                           hg_norm_g[j], gq_q_g[j], gq_k_g[j])
        kv = norm_matmul(mem2, norm_mem_kv_g[layer], ca_w_kv[layer].astype(BF16), tm=1024, tn=1024)
        x = cross_attention(x, kv.reshape(B, M, -1), norm_mem_q_g[layer], ca_w_q[layer].astype(BF16),
                            ca_q_g[layer], ca_k_g[layer], ca_w_out[layer].astype(BF16), tq=512)
        x = mlp_residual(x.reshape(B * L, D), norm_ffn_g[layer], layer_weight_bf16(mlp_w1, layer),
                         layer_weight_bf16(mlp_w2, layer), tm=512, th=1024).reshape(B, L, D)
    return x
```

```python
import functools
import math

import numpy as np
import jax
import jax.numpy as jnp
from jax import lax
from jax.experimental import pallas as pl
from jax.experimental.pallas import tpu as pltpu

F32 = jnp.float32
BF16 = jnp.bfloat16
HIGHEST = lax.Precision.HIGHEST

D_MODEL = 2048
NORM_EPS = 1e-6
GRID_W = 64
HY_WIDTH = D_MODEL // 2
HY_EMB = 33
HY_BANDS = (HY_EMB - 1) // 2
HY_FILT = 64
HY_SHORT = 3
HY_TARGET = 1e-2
HY_FAST_PCT = 0.3
HY_SLOW_PCT = 1.5
DF_HEADS = 8
DF_HEAD_DIM = 64
DF_V_DIM = 128
DF_QK_WIDTH = DF_HEADS * 2 * DF_HEAD_DIM
HG_WIDTH = D_MODEL // 2
HG_HEADS = 8
HG_DK = HG_WIDTH // HG_HEADS
GQ_WIDTH = D_MODEL - HG_WIDTH
GQ_HEADS = 8
GQ_KV_HEADS = 2
GQ_HEAD_DIM = GQ_WIDTH // GQ_HEADS
GQ_GROUP = GQ_HEADS // GQ_KV_HEADS
ROPE_THETA = 10000.0
CA_HEADS = 4
CA_HEAD_DIM = 128
CA_WIDTH = CA_HEADS * CA_HEAD_DIM

LANES = 128
SUBLANES = 8
VMEM_LIMIT_BYTES = 56 * 1024 * 1024

HG_CHUNK = 64
HG_BLOCK = SUBLANES
ATTN_STRIP = 128
HG_INTRA_CHUNKS = 16
LOG2E = 1.4426950408889634


def _params(*sem):
    return pltpu.CompilerParams(dimension_semantics=sem, vmem_limit_bytes=VMEM_LIMIT_BYTES)


def _rms(x, g):
    return x * lax.rsqrt(jnp.mean(x * x, axis=-1, keepdims=True) + NORM_EPS) * g


def _dot(a, b):
    return jnp.dot(a, b, preferred_element_type=F32)


def _dot_nt(a, b):
    return lax.dot_general(a, b, (((1,), (1,)), ((), ())), preferred_element_type=F32)


def _dot_tn(a, b):
    return lax.dot_general(a, b, (((0,), (0,)), ((), ())), preferred_element_type=F32)


def _pipeline3(n, stage_a, stage_b, stage_c):
    a = {}
    b = {}
    for t in range(n + 2):
        if t < n:
            a[t] = stage_a(t)
        if 0 <= t - 1 < n:
            b[t - 1] = stage_b(t - 1, a.pop(t - 1))
        if 0 <= t - 2 < n:
            stage_c(t - 2, b.pop(t - 2))


def _cast_kernel(w_ref, o_ref):
    o_ref[...] = w_ref[0].astype(o_ref.dtype)


def layer_weight_bf16(w, layer, *, block_bytes=8 * 1024 * 1024):
    _, K, N = w.shape
    tr = min(K, block_bytes // (4 * N))
    return pl.pallas_call(
        _cast_kernel,
        grid=(K // tr,),
        in_specs=[pl.BlockSpec((1, tr, N), lambda i: (layer, i, 0))],
        out_specs=pl.BlockSpec((tr, N), lambda i: (i, 0)),
        out_shape=jax.ShapeDtypeStruct((K, N), BF16),
        compiler_params=_params("parallel"),
        name="weight_bf16",
    )(w)


def _norm_matmul_kernel(x_ref, g_ref, w_ref, o_ref, hn_ref):
    @pl.when(pl.program_id(1) == 0)
    def _():
        hn_ref[...] = _rms(x_ref[...], g_ref[...]).astype(BF16)

    o_ref[...] = _dot(hn_ref[...], w_ref[...]).astype(o_ref.dtype)


def norm_matmul(x, g, w, *, tm, tn, out_dtype=F32):
    M, D = x.shape
    N = w.shape[1]
    tm = min(tm, M)
    tn = min(tn, N)
    return pl.pallas_call(
        _norm_matmul_kernel,
        grid=(M // tm, N // tn),
        in_specs=[pl.BlockSpec((tm, D), lambda i, j: (i, 0)),
                  pl.BlockSpec((1, D), lambda i, j: (0, 0)),
                  pl.BlockSpec((D, tn), lambda i, j: (0, j))],
        out_specs=pl.BlockSpec((tm, tn), lambda i, j: (i, j)),
        out_shape=jax.ShapeDtypeStruct((M, N), out_dtype),
        scratch_shapes=[pltpu.VMEM((tm, D), BF16)],
        compiler_params=_params("parallel", "arbitrary"),
        name="norm_matmul",
    )(x, g.reshape(1, D), w)


def _mlp_kernel(x_ref, g_ref, w1_ref, w2_ref, o_ref, hn_ref, acc_ref):
    j = pl.program_id(1)

    last = pl.num_programs(1) - 1

    @pl.when(j == 0)
    def _():
        x = x_ref[...]
        hn_ref[...] = _rms(x, g_ref[...]).astype(BF16)
        acc_ref[...] = x

    def accumulated():
        h1 = jnp.maximum(_dot(hn_ref[...], w1_ref[...]), 0.0)
        return acc_ref[...] + _dot((h1 * h1).astype(BF16), w2_ref[...])

    @pl.when(j != last)
    def _():
        acc_ref[...] = accumulated()

    @pl.when(j == last)
    def _():
        o_ref[...] = accumulated()


def mlp_residual(x, g, w1, w2, *, tm, th):
    M, D = x.shape
    H = w1.shape[1]
    tm = min(tm, M)
    th = min(th, H)
    return pl.pallas_call(
        _mlp_kernel,
        grid=(M // tm, H // th),
        in_specs=[pl.BlockSpec((tm, D), lambda i, j: (i, 0)),
                  pl.BlockSpec((1, D), lambda i, j: (0, 0)),
                  pl.BlockSpec((D, th), lambda i, j: (0, j)),
                  pl.BlockSpec((th, D), lambda i, j: (j, 0))],
        out_specs=pl.BlockSpec((tm, D), lambda i, j: (i, 0)),
        out_shape=jax.ShapeDtypeStruct((M, D), F32),
        scratch_shapes=[pltpu.VMEM((tm, D), BF16), pltpu.VMEM((tm, D), F32)],
        compiler_params=_params("parallel", "arbitrary"),
        name="mlp_residual",
    )(x, g.reshape(1, D), w1, w2)


def _cross_attn_kernel(a1_ref, a2_ref, w1_ref, w2_ref, x_ref, g_ref, wq_ref, kv_ref, qg_ref, kg_ref, wo_ref,
                       o_ref):
    x = x_ref[0] + _dot(a1_ref[0], w1_ref[...]) + _dot(a2_ref[0], w2_ref[...])
    hn = _rms(x, g_ref[...]).astype(BF16)
    q = _dot(hn, wq_ref[...])
    kv = kv_ref[0]
    scale = CA_HEAD_DIM ** -0.5 * LOG2E
    ones = jnp.ones((kv.shape[0], CA_HEAD_DIM), BF16)
    outs = []
    for h in range(CA_HEADS):
        sl = slice(h * CA_HEAD_DIM, (h + 1) * CA_HEAD_DIM)
        qh = (_rms(q[:, sl], qg_ref[...]) * scale).astype(BF16)
        kh = _rms(kv[:, sl], kg_ref[...]).astype(BF16)
        vh = kv[:, CA_WIDTH + h * CA_HEAD_DIM:CA_WIDTH + (h + 1) * CA_HEAD_DIM].astype(BF16)
        s = _dot_nt(qh, kh)
        e = jnp.exp2(s - jnp.max(s, axis=-1, keepdims=True)).astype(BF16)
        res = _dot(e, jnp.concatenate([vh, ones], axis=1))
        outs.append(res[:, :CA_HEAD_DIM] * (1.0 / res[:, CA_HEAD_DIM:CA_HEAD_DIM + 1]))
    o = jnp.concatenate(outs, axis=-1).astype(BF16)
    o_ref[0] = x + _dot(o, wo_ref[...])


def mix_proj_cross_attention(a1, a2, w_mix, x, kv, g, wq, qg, kg, wo, *, tq):
    B, L, D = x.shape
    M = kv.shape[1]
    K1 = a1.shape[2]
    assert a2.shape[2] == K1 and w_mix.shape[0] == 2 * K1
    tq = min(tq, L)
    return pl.pallas_call(
        _cross_attn_kernel,
        grid=(B, L // tq),
        in_specs=[pl.BlockSpec((1, tq, K1), lambda b, i: (b, i, 0)),
                  pl.BlockSpec((1, tq, K1), lambda b, i: (b, i, 0)),
                  pl.BlockSpec((K1, D), lambda b, i: (0, 0), pipeline_mode=pl.Buffered(1)),
                  pl.BlockSpec((K1, D), lambda b, i: (1, 0), pipeline_mode=pl.Buffered(1)),
                  pl.BlockSpec((1, tq, D), lambda b, i: (b, i, 0)),
                  pl.BlockSpec((1, D), lambda b, i: (0, 0)),
                  pl.BlockSpec((D, CA_WIDTH), lambda b, i: (0, 0)),
                  pl.BlockSpec((1, M, 2 * CA_WIDTH), lambda b, i: (b, 0, 0)),
                  pl.BlockSpec((1, CA_HEAD_DIM), lambda b, i: (0, 0)),
                  pl.BlockSpec((1, CA_HEAD_DIM), lambda b, i: (0, 0)),
                  pl.BlockSpec((CA_WIDTH, D), lambda b, i: (0, 0))],
        out_specs=pl.BlockSpec((1, tq, D), lambda b, i: (b, i, 0)),
        out_shape=jax.ShapeDtypeStruct((B, L, D), F32),
        compiler_params=_params("parallel", "arbitrary"),
        name="cross_attention",
    )(a1, a2, w_mix, w_mix, x, g.reshape(1, D), wq, kv, qg.reshape(1, -1), kg.reshape(1, -1), wo)


def _half_rms(x, g2, lo_mask):
    sq = x * x
    ms_lo = jnp.sum(jnp.where(lo_mask, sq, 0.0), axis=-1, keepdims=True)
    ms_hi = jnp.sum(jnp.where(lo_mask, 0.0, sq), axis=-1, keepdims=True)
    inv = jnp.where(lo_mask, lax.rsqrt(ms_lo * (1.0 / DF_HEAD_DIM) + NORM_EPS),
                    lax.rsqrt(ms_hi * (1.0 / DF_HEAD_DIM) + NORM_EPS))
    return x * inv * g2


def _diff_attn_kernel(sc_ref, q_ref, k_ref, v_ref, qg_ref, kg_ref, sg_ref, o_ref, kn_ref, vb_ref, *,
                      out_scale):
    h = pl.program_id(1)
    qi = pl.program_id(2)
    tq = q_ref.shape[1]
    L = k_ref.shape[1]
    lane = lax.broadcasted_iota(jnp.int32, (1, LANES), 1)
    lo_mask = lane < DF_HEAD_DIM

    def prepare_kv():
        kn_ref[...] = _half_rms(k_ref[0], kg_ref[...], lo_mask).astype(BF16)
        vb_ref[:, :LANES] = v_ref[0].astype(BF16)
        vb_ref[:, LANES:] = jnp.ones((L, LANES), BF16)

    if tq == L:
        prepare_kv()
    else:
        pl.when(qi == 0)(prepare_kv)

    lam = sc_ref[0]
    slope = sc_ref[1 + h] * LOG2E
    ts = min(ATTN_STRIP, tq)
    col = lax.broadcasted_iota(jnp.int32, (1, L), 1).astype(F32)

    def scores(i):
        r0 = i * ts
        qn = _half_rms(q_ref[0, r0:r0 + ts, :], qg_ref[...], lo_mask) * (DF_HEAD_DIM ** -0.5 * LOG2E)
        row = (lax.broadcasted_iota(jnp.int32, (ts, 1), 0) + (qi * tq + r0)).astype(F32)
        bias = jnp.abs(row - col) * slope
        kn = kn_ref[...]
        return (_dot_nt(jnp.where(lo_mask, qn, 0.0).astype(BF16), kn) - bias,
                _dot_nt(jnp.where(lo_mask, 0.0, qn).astype(BF16), kn) - bias)

    def exps(i, ss):
        return tuple(jnp.exp2(s - jnp.max(s, axis=-1, keepdims=True)).astype(BF16) for s in ss)

    def values(i, es):
        r1, r2 = (_dot(e, vb_ref[...]) for e in es)
        o = (r1[:, :LANES] * (1.0 / r1[:, LANES:LANES + 1])
             - r2[:, :LANES] * (lam / r2[:, LANES:LANES + 1]))
        o_ref[0, i * ts:(i + 1) * ts, :] = (_rms(o, sg_ref[...]) * out_scale).astype(o_ref.dtype)

    _pipeline3(tq // ts, scores, exps, values)


def diff_attention(proj, scal, q_g, k_g, subln_g, *, q_blk0, k_blk0, v_blk0, out_scale, tq):
    B, L, _ = proj.shape
    tq = min(tq, L)
    qg2 = jnp.concatenate([q_g, q_g]).reshape(1, LANES)
    kg2 = jnp.concatenate([k_g, k_g]).reshape(1, LANES)
    return pl.pallas_call(
        functools.partial(_diff_attn_kernel, out_scale=out_scale),
        grid=(B, DF_HEADS, L // tq),
        in_specs=[pl.BlockSpec(memory_space=pltpu.SMEM),
                  pl.BlockSpec((1, tq, LANES), lambda b, h, i: (b, i, q_blk0 + h)),
                  pl.BlockSpec((1, L, LANES), lambda b, h, i: (b, 0, k_blk0 + h)),
                  pl.BlockSpec((1, L, LANES), lambda b, h, i: (b, 0, v_blk0 + h)),
                  pl.BlockSpec((1, LANES), lambda b, h, i: (0, 0)),
                  pl.BlockSpec((1, LANES), lambda b, h, i: (0, 0)),
                  pl.BlockSpec((1, LANES), lambda b, h, i: (0, 0))],
        out_specs=pl.BlockSpec((1, tq, LANES), lambda b, h, i: (b, i, h)),
        out_shape=jax.ShapeDtypeStruct((B, L, DF_HEADS * DF_V_DIM), BF16),
        scratch_shapes=[pltpu.VMEM((L, LANES), BF16), pltpu.VMEM((L, 2 * LANES), BF16)],
        compiler_params=_params("parallel", "parallel", "arbitrary"),
        name="diff_attention",
    )(scal, proj, proj, proj, qg2, kg2, subln_g.reshape(1, LANES))


def _rope(x, cos, sin, first_mask):
    q = GQ_HEAD_DIM // 4
    rot = jnp.where(first_mask, -pltpu.roll(x, LANES - q, axis=1), pltpu.roll(x, q, axis=1))
    return x * cos + rot * sin


def _gqa_kernel(q_ref, k_ref, v_ref, cq_ref, sq_ref, ck_ref, sk_ref, qg_ref, kg_ref, o_ref, kn_ref, vb_ref):
    qi = pl.program_id(2)
    lane = lax.broadcasted_iota(jnp.int32, (1, LANES), 1)
    first_mask = (lane % (GQ_HEAD_DIM // 2)) < (GQ_HEAD_DIM // 4)

    @pl.when(qi == 0)
    def _():
        kn = _rms(k_ref[0], kg_ref[...])
        kn_ref[...] = _rope(kn, ck_ref[...], sk_ref[...], first_mask).astype(BF16)
        vb_ref[:, :LANES] = v_ref[0].astype(BF16)
        vb_ref[:, LANES:] = jnp.ones((k_ref.shape[1], LANES), BF16)

    scale = GQ_HEAD_DIM ** -0.5 * LOG2E
    tq = q_ref.shape[1]
    ts = min(ATTN_STRIP, tq)
    nr = tq // ts

    def where(i):
        r0 = (i % nr) * ts
        return slice(r0, r0 + ts), slice((i // nr) * GQ_HEAD_DIM, (i // nr + 1) * GQ_HEAD_DIM)

    def scores(i):
        rs, cs = where(i)
        qn = _rope(_rms(q_ref[0, rs, cs], qg_ref[...]), cq_ref[rs, :], sq_ref[rs, :], first_mask) * scale
        return _dot_nt(qn.astype(BF16), kn_ref[...])

    def exps(i, s):
        return jnp.exp2(s - jnp.max(s, axis=-1, keepdims=True)).astype(BF16)

    def values(i, e):
        rs, cs = where(i)
        res = _dot(e, vb_ref[...])
        o_ref[0, rs, cs] = (res[:, :LANES] * (1.0 / res[:, LANES:LANES + 1])).astype(o_ref.dtype)

    _pipeline3(GQ_GROUP * nr, scores, exps, values)


def gqa_attention(proj, cos, sin, q_g, k_g, *, q_blk0, k_blk0, v_blk0, tq):
    B, L, _ = proj.shape
    tq = min(tq, L)
    gw = GQ_GROUP * GQ_HEAD_DIM
    return pl.pallas_call(
        _gqa_kernel,
        grid=(B, GQ_KV_HEADS, L // tq),
        in_specs=[pl.BlockSpec((1, tq, gw), lambda b, g, i: (b, i, q_blk0 + g)),
                  pl.BlockSpec((1, L, LANES), lambda b, g, i: (b, 0, k_blk0 + g)),
                  pl.BlockSpec((1, L, LANES), lambda b, g, i: (b, 0, v_blk0 + g)),
                  pl.BlockSpec((tq, LANES), lambda b, g, i: (i, 0)),
                  pl.BlockSpec((tq, LANES), lambda b, g, i: (i, 0)),
                  pl.BlockSpec((L, LANES), lambda b, g, i: (0, 0)),
                  pl.BlockSpec((L, LANES), lambda b, g, i: (0, 0)),
                  pl.BlockSpec((1, LANES), lambda b, g, i: (0, 0)),
                  pl.BlockSpec((1, LANES), lambda b, g, i: (0, 0))],
        out_specs=pl.BlockSpec((1, tq, gw), lambda b, g, i: (b, i, g)),
        out_shape=jax.ShapeDtypeStruct((B, L, GQ_WIDTH), BF16),
        scratch_shapes=[pltpu.VMEM((L, LANES), BF16), pltpu.VMEM((L, 2 * LANES), BF16)],
        compiler_params=_params("parallel", "parallel", "arbitrary"),
        name="gqa_attention",
    )(proj, proj, proj, cos, sin, cos, sin, q_g.reshape(1, LANES), k_g.reshape(1, LANES))


def _hgrn2_pair_masks():
    pos = lax.broadcasted_iota(jnp.int32, (1, HG_BLOCK, LANES), 1)
    return {rev: [jnp.where((pos <= s) if rev else (pos >= s), 0.0, -jnp.inf).astype(F32)
                  for s in range(HG_BLOCK)] for rev in (False, True)}


def _hgrn2_intra_streams(streams, b_scr, pair_mask):
    C = HG_CHUNK
    nb = C // HG_BLOCK
    ns = range(len(streams))
    rev = [st[5] for st in streams]
    r_i = lax.broadcasted_iota(jnp.int32, (C, C), 0)
    c_i = lax.broadcasted_iota(jnp.int32, (C, C), 1)

    kk, lg = [], []
    for q, fz, v, vb, lb, _ in streams:
        f = lb + (1.0 - lb) * jax.nn.sigmoid(fz)
        kk.append(1.0 - f)
        lg.append(jnp.log(f) * LOG2E)

    tri_f = jnp.where(r_i >= c_i, 1.0, 0.0).astype(BF16)
    tri_b = jnp.where(r_i <= c_i, 1.0, 0.0).astype(BF16)
    b = []
    for i in ns:
        hi = lg[i].astype(BF16)
        r1 = lg[i] - hi.astype(F32)
        mid = r1.astype(BF16)
        lo = (r1 - mid.astype(F32)).astype(BF16)
        tri = tri_b if rev[i] else tri_f
        b.append(_dot(tri, hi) + _dot(tri, mid) + _dot(tri, lo))
    b_end = [b[i][0:1, :] if rev[i] else b[i][C - 1:C, :] for i in ns]
    qe = [(streams[i][0] * jnp.exp2(b[i])).astype(BF16) for i in ns]
    u = [_dot_tn(streams[i][3], (kk[i] * jnp.exp2(b_end[i] - b[i])).astype(BF16)) for i in ns]

    b3 = [b[i].reshape(nb, HG_BLOCK, LANES) for i in ns]
    q3 = [streams[i][0].reshape(nb, HG_BLOCK, LANES) for i in ns]
    seg_pos = lax.broadcasted_iota(jnp.int32, (C, HG_BLOCK * LANES), 1) // LANES
    row_pos = lax.broadcasted_iota(jnp.int32, (C, HG_BLOCK * LANES), 0) % HG_BLOCK
    kcat = []
    for i in ns:
        kb = kk[i].astype(BF16)
        kcat.append(jnp.where(seg_pos == row_pos, jnp.concatenate([kb] * HG_BLOCK, axis=1),
                              jnp.zeros((), BF16)))
    for i in ns:
        b_scr[i] = b[i]
    ps = [[] for i in ns]
    for s in range(HG_BLOCK):
        for i in ns:
            key_b = jnp.concatenate(
                [jnp.broadcast_to(b_scr[i, blk * HG_BLOCK + s:blk * HG_BLOCK + s + 1, :], (HG_BLOCK, LANES))
                 for blk in range(nb)], axis=0).reshape(nb, HG_BLOCK, LANES)
            d = (b3[i] - key_b) + pair_mask[rev[i]][s]
            ps[i].append((q3[i] * jnp.exp2(d)).reshape(C, LANES).astype(BF16))
    same_blk = (r_i // HG_BLOCK) == (c_i // HG_BLOCK)
    adiag = [_dot_nt(jnp.concatenate(ps[i], axis=1), kcat[i]) for i in ns]

    zrow = jnp.zeros((1, 1, LANES), F32)
    qt = []
    for i in ns:
        if rev[i]:
            rblk3 = jnp.concatenate([b3[i][1:, 0:1, :], zrow], axis=0)
        else:
            rblk3 = jnp.concatenate([zrow, b3[i][:nb - 1, HG_BLOCK - 1:HG_BLOCK, :]], axis=0)
        qt.append((q3[i] * jnp.exp2(b3[i] - rblk3)).reshape(C, LANES))
    rows = [[] for i in ns]
    for blk in range(nb):
        for i in ns:
            if rev[i]:
                edge = (blk + 1) * HG_BLOCK
                if edge == C:
                    rows[i].append(jnp.zeros((HG_BLOCK, C), F32))
                    continue
                kt = kk[i][edge:, :] * jnp.exp2(b[i][edge:edge + 1, :] - b[i][edge:, :])
                kfull = jnp.concatenate([jnp.zeros((edge, LANES), F32), kt], axis=0)
            else:
                edge = blk * HG_BLOCK
                if edge == 0:
                    rows[i].append(jnp.zeros((HG_BLOCK, C), F32))
                    continue
                kt = kk[i][:edge, :] * jnp.exp2(b[i][edge - 1:edge, :] - b[i][:edge, :])
                kfull = jnp.concatenate([kt, jnp.zeros((C - edge, LANES), F32)], axis=0)
            rows[i].append(_dot_nt(qt[i][blk * HG_BLOCK:(blk + 1) * HG_BLOCK, :].astype(BF16),
                                   kfull.astype(BF16)))
    outs = []
    for i in ns:
        att = jnp.where(same_blk, adiag[i], jnp.concatenate(rows[i], axis=0))
        o = _dot(att.astype(BF16), streams[i][3])
        outs.append((o, qe[i], u[i], jnp.exp2(b_end[i])))
    return outs


def _hgrn2_kernel(q_ref, ff_ref, fb_ref, i_ref, g_ref, lbf_ref, lbb_ref, ng_ref, o_ref, of_ref, ob_ref,
                  sf_ref, sb_ref, b_scr):
    L = q_ref.shape[1]
    C = HG_CHUNK
    n = L // C
    gsize = math.gcd(HG_INTRA_CHUNKS, n)
    qscale = HG_DK ** -0.5

    def rows(c):
        return pl.ds(pl.multiple_of(c * C, C), C)

    sf_ref[...] = jnp.zeros_like(sf_ref)
    sb_ref[...] = jnp.zeros_like(sb_ref)
    pair_mask = _hgrn2_pair_masks()

    def group(it, carry):
        streams = []
        dests = []
        for k in range(gsize):
            c = it * gsize + k
            for fz_ref, lb_ref, o_s, s_s, cc, reverse in ((ff_ref, lbf_ref, of_ref, sf_ref, c, False),
                                                          (fb_ref, lbb_ref, ob_ref, sb_ref, n - 1 - c, True)):
                r = rows(cc)
                v = i_ref[0, r, :]
                streams.append((q_ref[0, r, :] * qscale, fz_ref[0, r, :], v, v.astype(BF16), lb_ref[0], reverse))
                dests.append((o_s, s_s, r))
        state = {id(sf_ref): sf_ref[...], id(sb_ref): sb_ref[...]}
        for (o, qe, u, e), (o_s, s_s, r) in zip(_hgrn2_intra_streams(streams, b_scr, pair_mask), dests):
            st = state[id(s_s)]
            o_s[r, :] = o + _dot_nt(qe, st.astype(BF16))
            state[id(s_s)] = st * e + u
        sf_ref[...] = state[id(sf_ref)]
        sb_ref[...] = state[id(sb_ref)]
        return carry

    lax.fori_loop(0, n // gsize, group, 0)
    o = _rms(of_ref[...] + ob_ref[...], ng_ref[...])
    g = g_ref[0]
    o_ref[0] = (o * (g * jax.nn.sigmoid(g))).astype(o_ref.dtype)


def hgrn2_mixer(proj, lb_f, lb_b, norm_g):
    B, L, _ = proj.shape
    H = HG_HEADS

    def col(k):
        return pl.BlockSpec((1, L, LANES), lambda b, h: (b, 0, k * H + h))

    lbspec = pl.BlockSpec((1, 1, LANES), lambda b, h: (h, 0, 0))
    return pl.pallas_call(
        _hgrn2_kernel,
        grid=(B, H),
        in_specs=[col(0), col(1), col(2), col(3), col(4), lbspec, lbspec,
                  pl.BlockSpec((1, LANES), lambda b, h: (0, 0))],
        out_specs=pl.BlockSpec((1, L, LANES), lambda b, h: (b, 0, h)),
        out_shape=jax.ShapeDtypeStruct((B, L, HG_WIDTH), BF16),
        scratch_shapes=[pltpu.VMEM((L, LANES), F32), pltpu.VMEM((L, LANES), F32),
                        pltpu.VMEM((LANES, LANES), F32), pltpu.VMEM((LANES, LANES), F32),
                        pltpu.VMEM((2 * math.gcd(HG_INTRA_CHUNKS, L // HG_CHUNK), HG_CHUNK, LANES), F32)],
        compiler_params=_params("parallel", "arbitrary"),
        name="hgrn2",
    )(proj, proj, proj, proj, proj, lb_f.reshape(H, 1, LANES), lb_b.reshape(H, 1, LANES),
      norm_g.reshape(1, LANES))


def _hyena_filter_kernel(z_ref, t_ref, w1_ref, b1_ref, w2_ref, b2_ref, w3_ref, b3_ref, w4f_ref, w4b_ref,
                         fr_ref, ad_ref, gs_ref, gd_ref, kn_ref):
    def hdot(a, b):
        return jnp.dot(a, b, precision=HIGHEST, preferred_element_type=F32)

    fr = fr_ref[...]
    h = jnp.sin(fr * (hdot(z_ref[...], w1_ref[...]) + b1_ref[...]))
    h = jnp.sin(fr * (hdot(h, w2_ref[...]) + b2_ref[...]))
    h = jnp.sin(fr * (hdot(h, w3_ref[...]) + b3_ref[...]))
    window = jnp.exp(-t_ref[...] * ad_ref[...])
    hf = hdot(h, w4f_ref[...]) * window
    hb = hdot(h, w4b_ref[...]) * window
    row = lax.broadcasted_iota(jnp.int32, hb.shape, 0)
    hb = jnp.where(row == 0, 0.0, hb)
    gs_ref[...] = hf + hb
    gd_ref[...] = hb - hf
    sgn = jnp.where(row % 2 == 0, 1.0, -1.0)
    kn_ref[...] = jnp.sum((hf + hb) * sgn, axis=0, keepdims=True) * (0.5 / hb.shape[0])


def _split2(x):
    hi = x.astype(BF16)
    return hi, (x - hi.astype(F32)).astype(BF16)


def _hyena_spectrum_kernel(ah_ref, al_ref, bh_ref, bl_ref, gs_ref, gd_ref, kre_ref, kim_ref, sh_ref, sl_ref,
                           dh_ref, dl_ref, *, n_fft):
    @pl.when(pl.program_id(1) == 0)
    def _():
        sh_ref[...], sl_ref[...] = _split2(gs_ref[...])
        dh_ref[...], dl_ref[...] = _split2(gd_ref[...])

    tf = ah_ref.shape[0]
    f0 = pl.program_id(1) * tf
    row = lax.broadcasted_iota(jnp.int32, (tf, 1), 0) + f0
    wf = jnp.where(row == 0, 1.0 / n_fft, 2.0 / n_fft)
    ah = ah_ref[...]
    bh = bh_ref[...]
    kre_ref[...] = wf * (_dot(ah, sh_ref[...]) + _dot(ah, sl_ref[...]) + _dot(al_ref[...], sh_ref[...]))
    kim_ref[...] = wf * (_dot(bh, dh_ref[...]) + _dot(bh, dl_ref[...]) + _dot(bl_ref[...], dh_ref[...]))


def hyena_filter_spectrum(L, cos_hi, cos_lo, sin_hi, sin_lo, w1, b1, w2, b2, w3, b3, w4, sin_freq):
    C = HY_WIDTH
    t = jnp.linspace(0.0, 1.0, L, dtype=F32)[:, None]
    w = (2.0 * math.pi / L) * jnp.arange(L, dtype=F32)[:, None]
    f = jnp.linspace(1e-4, HY_BANDS - 1, HY_BANDS, dtype=F32)[None, :]
    z = jnp.concatenate([t, jnp.cos(f * w), -jnp.sin(f * w)], axis=-1)
    zp = jnp.pad(z, ((0, 0), (0, LANES - HY_EMB)))
    w1p = jnp.pad(w1, ((0, LANES - HY_EMB), (0, 0)))
    max_decay = math.log(HY_TARGET) / HY_FAST_PCT
    min_decay = math.log(HY_TARGET) / HY_SLOW_PCT
    absd = jnp.abs(jnp.linspace(min_decay, max_decay, C, dtype=F32))[None, :]
    tc = min(512, C)
    full = lambda shape: pl.BlockSpec(shape, lambda j: (0, 0))
    gs, gd, knyq = pl.pallas_call(
        _hyena_filter_kernel,
        grid=(C // tc,),
        in_specs=[full((L, LANES)), full((L, 1)), full((LANES, HY_FILT)), full((1, HY_FILT)),
                  full((HY_FILT, HY_FILT)), full((1, HY_FILT)), full((HY_FILT, HY_FILT)), full((1, HY_FILT)),
                  pl.BlockSpec((HY_FILT, tc), lambda j: (0, j)),
                  pl.BlockSpec((HY_FILT, tc), lambda j: (0, C // tc + j)),
                  full((1, HY_FILT)),
                  pl.BlockSpec((1, tc), lambda j: (0, j))],
        out_specs=[pl.BlockSpec((L, tc), lambda j: (0, j)), pl.BlockSpec((L, tc), lambda j: (0, j)),
                   pl.BlockSpec((1, tc), lambda j: (0, j))],
        out_shape=[jax.ShapeDtypeStruct((L, C), F32), jax.ShapeDtypeStruct((L, C), F32),
                   jax.ShapeDtypeStruct((1, C), F32)],
        compiler_params=_params("arbitrary"),
        name="hyena_filter",
    )(zp, t, w1p, b1.reshape(1, -1), w2, b2.reshape(1, -1), w3, b3.reshape(1, -1), w4, w4,
      sin_freq.reshape(1, -1), absd)

    tf = min(256, L)
    tbl = pl.BlockSpec((tf, L), lambda j, i: (i, 0))
    kre, kim = pl.pallas_call(
        functools.partial(_hyena_spectrum_kernel, n_fft=2 * L),
        grid=(C // tc, L // tf),
        in_specs=[tbl, tbl, tbl, tbl,
                  pl.BlockSpec((L, tc), lambda j, i: (0, j)),
                  pl.BlockSpec((L, tc), lambda j, i: (0, j))],
        out_specs=[pl.BlockSpec((tf, tc), lambda j, i: (i, j)), pl.BlockSpec((tf, tc), lambda j, i: (i, j))],
        out_shape=[jax.ShapeDtypeStruct((L, C), F32), jax.ShapeDtypeStruct((L, C), F32)],
        scratch_shapes=[pltpu.VMEM((L, tc), BF16)] * 4,
        compiler_params=_params("parallel", "arbitrary"),
        name="hyena_spectrum",
    )(cos_hi, cos_lo, sin_hi, sin_lo, gs, gd)
    return kre, kim, knyq


def _short_conv(u, w_ref, b_ref):
    L = u.shape[0]
    row = lax.broadcasted_iota(jnp.int32, (L, 1), 0)
    prev = jnp.where(row == 0, 0.0, pltpu.roll(u, 1, axis=0))
    nxt = jnp.where(row == L - 1, 0.0, pltpu.roll(u, L - 1, axis=0))
    return prev * w_ref[0:1, :] + u * w_ref[1:2, :] + nxt * w_ref[2:3, :] + b_ref[...]


def _hyena_conv_kernel(x1_ref, x2_ref, v_ref, cw1_ref, cw2_ref, cw3_ref, cb1_ref, cb2_ref, cb3_ref,
                       a_ref, b_ref, kre_ref, kim_ref, knyq_ref, bias_ref, o_ref, r1_ref, r2_ref, *, tf):
    L = v_ref.shape[1]
    x2 = _short_conv(x2_ref[0], cw2_ref, cb2_ref)
    vv = _short_conv(v_ref[0], cw3_ref, cb3_ref)
    vx = vv * x2
    vxb = vx.astype(BF16)
    sgn = jnp.where(lax.broadcasted_iota(jnp.int32, (L, 1), 0) % 2 == 0, 1.0, -1.0)
    def forward(fb):
        fs = slice(fb * tf, (fb + 1) * tf)
        return _dot(a_ref[fs, :], vxb), _dot(b_ref[fs, :], vxb)

    def product(fb, pq):
        p, q = pq
        fs = slice(fb * tf, (fb + 1) * tf)
        kre = kre_ref[fs, :]
        kim = kim_ref[fs, :]
        r1_ref[fs, :] = (p * kre + q * kim).astype(BF16)
        r2_ref[fs, :] = (q * kre - p * kim).astype(BF16)

    _pipeline3(L // tf, forward, product, lambda fb, _: None)

    v_nyq = jnp.sum(vx * sgn, axis=0, keepdims=True)
    y0 = vx * bias_ref[...] + sgn * (v_nyq * knyq_ref[...])
    x1 = _short_conv(x1_ref[0], cw1_ref, cb1_ref)
    y = _dot(a_ref[...], r1_ref[...]) + _dot(b_ref[...], r2_ref[...])
    o_ref[0] = ((y0 + y) * x1).astype(o_ref.dtype)


def hyena_conv(proj, conv_w, conv_b, cos_b, sin_b, kre, kim, knyq, bias, *, tc):
    B, L, _ = proj.shape
    C = HY_WIDTH
    tc = min(tc, C)
    nc = C // tc
    tf = min(512, L)

    def grp(k):
        return pl.BlockSpec((1, L, tc), lambda c, b: (b, 0, k * nc + c))

    def cw(k):
        return pl.BlockSpec((HY_SHORT, tc), lambda c, b: (0, k * nc + c))

    def cb(k):
        return pl.BlockSpec((1, tc), lambda c, b: (0, k * nc + c))

    tbl = pl.BlockSpec((L, L), lambda c, b: (0, 0), pipeline_mode=pl.Buffered(1))
    chan = pl.BlockSpec((L, tc), lambda c, b: (0, c), pipeline_mode=pl.Buffered(1))
    vec = pl.BlockSpec((1, tc), lambda c, b: (0, c))
    return pl.pallas_call(
        functools.partial(_hyena_conv_kernel, tf=tf),
        grid=(nc, B),
        in_specs=[grp(0), grp(1), grp(2), cw(0), cw(1), cw(2), cb(0), cb(1), cb(2),
                  tbl, tbl, chan, chan, vec, vec],
        out_specs=pl.BlockSpec((1, L, tc), lambda c, b: (b, 0, c)),
        out_shape=jax.ShapeDtypeStruct((B, L, C), BF16),
        scratch_shapes=[pltpu.VMEM((L, tc), BF16), pltpu.VMEM((L, tc), BF16)],
        compiler_params=_params("parallel", "arbitrary"),
        name="hyena_conv",
    )(proj, proj, proj, conv_w, conv_w, conv_w, conv_b.reshape(1, -1), conv_b.reshape(1, -1),
      conv_b.reshape(1, -1), cos_b, sin_b, kre, kim, knyq, bias.reshape(1, -1))


def _dft_tables(L):
    n = 2 * L
    r = math.gcd(GRID_W, L)
    j = jnp.arange(L, dtype=jnp.int32)[None, :]

    def cs(f):
        ang = ((f[:, None] * j) % n).astype(F32) * (2.0 * math.pi / n)
        return jnp.cos(ang), jnp.sin(ang)

    c0, s0 = cs(jnp.arange(r, dtype=jnp.int32))
    c1, s1 = cs(jnp.arange(L // r, dtype=jnp.int32) * r)
    cos_m = (c1[:, None, :] * c0[None, :, :] - s1[:, None, :] * s0[None, :, :]).reshape(L, L)
    sin_m = (s1[:, None, :] * c0[None, :, :] + c1[:, None, :] * s0[None, :, :]).reshape(L, L)
    cos_hi = cos_m.astype(BF16)
    sin_hi = sin_m.astype(BF16)
    return (cos_hi, (cos_m - cos_hi.astype(F32)).astype(BF16),
            sin_hi, (sin_m - sin_hi.astype(F32)).astype(BF16))


def _axial_rope_tables(L):
    rows = L // GRID_W
    r, c = jnp.meshgrid(jnp.arange(rows, dtype=F32), jnp.arange(GRID_W, dtype=F32), indexing='ij')
    r = r.reshape(-1)
    c = c.reshape(-1)
    half = GQ_HEAD_DIM // 2
    inv = ROPE_THETA ** (-jnp.arange(0, half, 2, dtype=F32) / half)
    ang_r = r[:, None] * inv[None, :]
    ang_c = c[:, None] * inv[None, :]
    ang = jnp.concatenate([ang_r, ang_r, ang_c, ang_c], axis=-1)
    return jnp.cos(ang), jnp.sin(ang)


def _even_mixer(x, layer, norm_g, w_in, conv_w, conv_b, fw1, fb1, fw2, fb2, fw3, fb3, fw4, sin_freq,
                hy_bias, q_g, k_g, lam_q1, lam_k1, lam_q2, lam_k2, subln_g):
    B, L, D = x.shape
    x2 = x.reshape(B * L, D)
    proj = norm_matmul(x2, norm_g, w_in.astype(BF16), tm=1024, tn=1536).reshape(B, L, -1)
    cos_hi, cos_lo, sin_hi, sin_lo = _dft_tables(L)
    kre, kim, knyq = hyena_filter_spectrum(L, cos_hi, cos_lo, sin_hi, sin_lo, fw1, fb1, fw2, fb2, fw3, fb3, fw4,
                                           sin_freq)
    o_a = hyena_conv(proj, conv_w, conv_b, cos_hi, sin_hi, kre, kim, knyq, hy_bias,
                     tc=256)
    lam_init = 0.8 - 0.6 * math.exp(-0.3 * layer)
    lam = (jnp.exp(jnp.sum(lam_q1.astype(F32) * lam_k1.astype(F32)))
           - jnp.exp(jnp.sum(lam_q2.astype(F32) * lam_k2.astype(F32))) + lam_init)
    slopes = jnp.asarray(np.array([2.0 ** (-8.0 * (h + 1) / DF_HEADS) for h in range(DF_HEADS)],
                                  dtype=np.float32))
    scal = jnp.concatenate([lam.reshape(1), slopes]).astype(F32)
    nb_hy = 3 * HY_WIDTH // LANES
    nb_qk = DF_QK_WIDTH // LANES
    o_b = diff_attention(proj, scal, q_g, k_g, subln_g, q_blk0=nb_hy, k_blk0=nb_hy + nb_qk,
                         v_blk0=nb_hy + 2 * nb_qk, out_scale=1.0 - lam_init, tq=2048)
    return o_a, o_b


def _odd_mixer(x, norm_g, lb_f, lb_b, w_in, hg_norm_g, q_g, k_g):
    B, L, D = x.shape
    x2 = x.reshape(B * L, D)
    proj = norm_matmul(x2, norm_g, w_in.astype(BF16), tm=1024, tn=1664).reshape(B, L, -1)
    o_c = hgrn2_mixer(proj, lb_f, lb_b, hg_norm_g)
    cos, sin = _axial_rope_tables(L)
    gw = GQ_GROUP * GQ_HEAD_DIM
    q0 = 5 * HG_WIDTH
    k0 = q0 + GQ_WIDTH
    v0 = k0 + GQ_KV_HEADS * GQ_HEAD_DIM
    o_d = gqa_attention(proj, cos, sin, q_g, k_g, q_blk0=q0 // gw, k_blk0=k0 // LANES, v_blk0=v0 // LANES,
                        tq=1024)
    return o_c, o_d


def kernel(x, mem, norm_mix_g, norm_mem_q_g, norm_mem_kv_g, norm_ffn_g, ev_w_in, ev_w_out, hy_conv_w, hy_conv_b, hy_fw1, hy_fb1, hy_fw2, hy_fb2, hy_fw3, hy_fb3, hy_fw4, hy_sin_freq, hy_bias, df_q_g, df_k_g, df_lam_q1, df_lam_k1, df_lam_q2, df_lam_k2, df_subln_g, od_w_in, od_w_out, hg_lb_logits, hg_norm_g, gq_q_g, gq_k_g, ca_w_q, ca_w_kv, ca_w_out, ca_q_g, ca_k_g, mlp_w1, mlp_w2):
    B, L, D = x.shape
    M = mem.shape[1]
    depth = norm_mix_g.shape[0]
    lb_soft = jax.nn.softmax(hg_lb_logits.astype(F32), axis=1)
    lbs = jnp.cumsum(lb_soft, axis=1)
    lbs = lbs - lbs[:, :1]
    mem2 = mem.reshape(B * M, D)
    for layer in range(depth):
        j = layer // 2
        if layer % 2 == 0:
            a1, a2 = _even_mixer(x, layer, norm_mix_g[layer], ev_w_in[j], hy_conv_w[j], hy_conv_b[j],
                                 hy_fw1[j], hy_fb1[j], hy_fw2[j], hy_fb2[j], hy_fw3[j], hy_fb3[j], hy_fw4[j],
                                 hy_sin_freq[j], hy_bias[j], df_q_g[j], df_k_g[j], df_lam_q1[j], df_lam_k1[j],
                                 df_lam_q2[j], df_lam_k2[j], df_subln_g[j])
            w_mix = ev_w_out[j]
        else:
            a1, a2 = _odd_mixer(x, norm_mix_g[layer], lbs[0, layer], lbs[1, layer], od_w_in[j],
                                hg_norm_g[j], gq_q_g[j], gq_k_g[j])
            w_mix = od_w_out[j]
        kv = norm_matmul(mem2, norm_mem_kv_g[layer], ca_w_kv[layer].astype(BF16), tm=1024, tn=1024)
        x = mix_proj_cross_attention(a1, a2, w_mix.astype(BF16), x, kv.reshape(B, M, -1), norm_mem_q_g[layer],
                                     ca_w_q[layer].astype(BF16), ca_q_g[layer], ca_k_g[layer],
                                     ca_w_out[layer].astype(BF16), tq=512)
        x = mlp_residual(x.reshape(B * L, D), norm_ffn_g[layer], layer_weight_bf16(mlp_w1, layer),
                         layer_weight_bf16(mlp_w2, layer), tm=512, th=1024).reshape(B, L, D)
    return x
```

```python
import functools
import math

import numpy as np
import jax
import jax.numpy as jnp
from jax import lax
from jax.experimental import pallas as pl
from jax.experimental.pallas import tpu as pltpu

F32 = jnp.float32
BF16 = jnp.bfloat16
HIGHEST = lax.Precision.HIGHEST

D_MODEL = 2048
NORM_EPS = 1e-6
GRID_W = 64
HY_WIDTH = D_MODEL // 2
HY_EMB = 33
HY_BANDS = (HY_EMB - 1) // 2
HY_FILT = 64
HY_SHORT = 3
HY_TARGET = 1e-2
HY_FAST_PCT = 0.3
HY_SLOW_PCT = 1.5
DF_HEADS = 8
DF_HEAD_DIM = 64
DF_V_DIM = 128
DF_QK_WIDTH = DF_HEADS * 2 * DF_HEAD_DIM
HG_WIDTH = D_MODEL // 2
HG_HEADS = 8
HG_DK = HG_WIDTH // HG_HEADS
GQ_WIDTH = D_MODEL - HG_WIDTH
GQ_HEADS = 8
GQ_KV_HEADS = 2
GQ_HEAD_DIM = GQ_WIDTH // GQ_HEADS
GQ_GROUP = GQ_HEADS // GQ_KV_HEADS
ROPE_THETA = 10000.0
CA_HEADS = 4
CA_HEAD_DIM = 128
CA_WIDTH = CA_HEADS * CA_HEAD_DIM

LANES = 128
SUBLANES = 8
VMEM_LIMIT_BYTES = 56 * 1024 * 1024

HG_CHUNK = 64
HG_BLOCK = SUBLANES
ATTN_STRIP = 128
HG_INTRA_CHUNKS = 16
LOG2E = 1.4426950408889634


def _params(*sem):
    return pltpu.CompilerParams(dimension_semantics=sem, vmem_limit_bytes=VMEM_LIMIT_BYTES)


def _rms(x, g):
    return x * lax.rsqrt(jnp.mean(x * x, axis=-1, keepdims=True) + NORM_EPS) * g


def _dot(a, b):
    return jnp.dot(a, b, preferred_element_type=F32)


def _dot_nt(a, b):
    return lax.dot_general(a, b, (((1,), (1,)), ((), ())), preferred_element_type=F32)


def _dot_tn(a, b):
    return lax.dot_general(a, b, (((0,), (0,)), ((), ())), preferred_element_type=F32)


def _pipeline3(n, stage_a, stage_b, stage_c):
    a = {}
    b = {}
    for t in range(n + 2):
        if t < n:
            a[t] = stage_a(t)
        if 0 <= t - 1 < n:
            b[t - 1] = stage_b(t - 1, a.pop(t - 1))
        if 0 <= t - 2 < n:
            stage_c(t - 2, b.pop(t - 2))


def _cast_kernel(w_ref, o_ref):
    o_ref[...] = w_ref[0].astype(o_ref.dtype)


def layer_weight_bf16(w, layer, *, block_bytes=8 * 1024 * 1024):
    _, K, N = w.shape
    tr = min(K, block_bytes // (4 * N))
    return pl.pallas_call(
        _cast_kernel,
        grid=(K // tr,),
        in_specs=[pl.BlockSpec((1, tr, N), lambda i: (layer, i, 0))],
        out_specs=pl.BlockSpec((tr, N), lambda i: (i, 0)),
        out_shape=jax.ShapeDtypeStruct((K, N), BF16),
        compiler_params=_params("parallel"),
        name="weight_bf16",
    )(w)


def _norm_matmul_kernel(x_ref, g_ref, w_ref, o_ref, hn_ref):
    @pl.when(pl.program_id(1) == 0)
    def _():
        hn_ref[...] = _rms(x_ref[...], g_ref[...]).astype(BF16)

    o_ref[...] = _dot(hn_ref[...], w_ref[...]).astype(o_ref.dtype)


def norm_matmul(x, g, w, *, tm, tn, out_dtype=F32):
    M, D = x.shape
    N = w.shape[1]
    tm = min(tm, M)
    tn = min(tn, N)
    return pl.pallas_call(
        _norm_matmul_kernel,
        grid=(M // tm, N // tn),
        in_specs=[pl.BlockSpec((tm, D), lambda i, j: (i, 0)),
                  pl.BlockSpec((1, D), lambda i, j: (0, 0)),
                  pl.BlockSpec((D, tn), lambda i, j: (0, j))],
        out_specs=pl.BlockSpec((tm, tn), lambda i, j: (i, j)),
        out_shape=jax.ShapeDtypeStruct((M, N), out_dtype),
        scratch_shapes=[pltpu.VMEM((tm, D), BF16)],
        compiler_params=_params("parallel", "arbitrary"),
        name="norm_matmul",
    )(x, g.reshape(1, D), w)


def _mlp_kernel(x_ref, g_ref, w1_ref, w2_ref, o_ref, hn_ref, acc_ref):
    j = pl.program_id(1)

    last = pl.num_programs(1) - 1

    @pl.when(j == 0)
    def _():
        x = x_ref[...]
        hn_ref[...] = _rms(x, g_ref[...]).astype(BF16)
        acc_ref[...] = x

    def accumulated():
        h1 = jnp.maximum(_dot(hn_ref[...], w1_ref[...]), 0.0)
        return acc_ref[...] + _dot((h1 * h1).astype(BF16), w2_ref[...])

    @pl.when(j != last)
    def _():
        acc_ref[...] = accumulated()

    @pl.when(j == last)
    def _():
        o_ref[...] = accumulated()


def mlp_residual(x, g, w1, w2, *, tm, th):
    M, D = x.shape
    H = w1.shape[1]
    tm = min(tm, M)
    th = min(th, H)
    return pl.pallas_call(
        _mlp_kernel,
        grid=(M // tm, H // th),
        in_specs=[pl.BlockSpec((tm, D), lambda i, j: (i, 0)),
                  pl.BlockSpec((1, D), lambda i, j: (0, 0)),
                  pl.BlockSpec((D, th), lambda i, j: (0, j)),
                  pl.BlockSpec((th, D), lambda i, j: (j, 0))],
        out_specs=pl.BlockSpec((tm, D), lambda i, j: (i, 0), pipeline_mode=pl.Buffered(1)),
        out_shape=jax.ShapeDtypeStruct((M, D), F32),
        scratch_shapes=[pltpu.VMEM((tm, D), BF16), pltpu.VMEM((tm, D), F32)],
        compiler_params=_params("parallel", "arbitrary"),
        name="mlp_residual",
    )(x, g.reshape(1, D), w1, w2)


def _cross_attn_kernel(a1_ref, a2_ref, w1_ref, w2_ref, x_ref, g_ref, wq_ref, kv_ref, qg_ref, kg_ref, wo_ref,
                       o_ref):
    x = x_ref[0] + _dot(a1_ref[0], w1_ref[...]) + _dot(a2_ref[0], w2_ref[...])
    hn = _rms(x, g_ref[...]).astype(BF16)
    q = _dot(hn, wq_ref[...])
    kv = kv_ref[0]
    scale = CA_HEAD_DIM ** -0.5 * LOG2E
    ones = jnp.ones((kv.shape[0], CA_HEAD_DIM), BF16)
    outs = []
    for h in range(CA_HEADS):
        sl = slice(h * CA_HEAD_DIM, (h + 1) * CA_HEAD_DIM)
        qh = (_rms(q[:, sl], qg_ref[...]) * scale).astype(BF16)
        kh = _rms(kv[:, sl], kg_ref[...]).astype(BF16)
        vh = kv[:, CA_WIDTH + h * CA_HEAD_DIM:CA_WIDTH + (h + 1) * CA_HEAD_DIM].astype(BF16)
        s = _dot_nt(qh, kh)
        e = jnp.exp2(s - jnp.max(s, axis=-1, keepdims=True)).astype(BF16)
        res = _dot(e, jnp.concatenate([vh, ones], axis=1))
        outs.append(res[:, :CA_HEAD_DIM] * (1.0 / res[:, CA_HEAD_DIM:CA_HEAD_DIM + 1]))
    o = jnp.concatenate(outs, axis=-1).astype(BF16)
    o_ref[0] = x + _dot(o, wo_ref[...])


def mix_proj_cross_attention(a1, a2, w_mix, x, kv, g, wq, qg, kg, wo, *, tq):
    B, L, D = x.shape
    M = kv.shape[1]
    K1 = a1.shape[2]
    assert a2.shape[2] == K1 and w_mix.shape[0] == 2 * K1
    tq = min(tq, L)
    return pl.pallas_call(
        _cross_attn_kernel,
        grid=(B, L // tq),
        in_specs=[pl.BlockSpec((1, tq, K1), lambda b, i: (b, i, 0)),
                  pl.BlockSpec((1, tq, K1), lambda b, i: (b, i, 0)),
                  pl.BlockSpec((K1, D), lambda b, i: (0, 0), pipeline_mode=pl.Buffered(1)),
                  pl.BlockSpec((K1, D), lambda b, i: (1, 0), pipeline_mode=pl.Buffered(1)),
                  pl.BlockSpec((1, tq, D), lambda b, i: (b, i, 0)),
                  pl.BlockSpec((1, D), lambda b, i: (0, 0)),
                  pl.BlockSpec((D, CA_WIDTH), lambda b, i: (0, 0)),
                  pl.BlockSpec((1, M, 2 * CA_WIDTH), lambda b, i: (b, 0, 0)),
                  pl.BlockSpec((1, CA_HEAD_DIM), lambda b, i: (0, 0)),
                  pl.BlockSpec((1, CA_HEAD_DIM), lambda b, i: (0, 0)),
                  pl.BlockSpec((CA_WIDTH, D), lambda b, i: (0, 0))],
        out_specs=pl.BlockSpec((1, tq, D), lambda b, i: (b, i, 0)),
        out_shape=jax.ShapeDtypeStruct((B, L, D), F32),
        compiler_params=_params("parallel", "arbitrary"),
        name="cross_attention",
    )(a1, a2, w_mix, w_mix, x, g.reshape(1, D), wq, kv, qg.reshape(1, -1), kg.reshape(1, -1), wo)


def _half_rms(x, g2, lo_mask):
    sq = x * x
    ms_lo = jnp.sum(jnp.where(lo_mask, sq, 0.0), axis=-1, keepdims=True)
    ms_hi = jnp.sum(jnp.where(lo_mask, 0.0, sq), axis=-1, keepdims=True)
    inv = jnp.where(lo_mask, lax.rsqrt(ms_lo * (1.0 / DF_HEAD_DIM) + NORM_EPS),
                    lax.rsqrt(ms_hi * (1.0 / DF_HEAD_DIM) + NORM_EPS))
    return x * inv * g2


def _diff_attn_kernel(sc_ref, q_ref, k_ref, v_ref, qg_ref, kg_ref, sg_ref, o_ref, kn_ref, vb_ref, *,
                      out_scale):
    h = pl.program_id(1)
    qi = pl.program_id(2)
    tq = q_ref.shape[1]
    L = k_ref.shape[1]
    lane = lax.broadcasted_iota(jnp.int32, (1, LANES), 1)
    lo_mask = lane < DF_HEAD_DIM

    def prepare_kv():
        kn_ref[...] = _half_rms(k_ref[0], kg_ref[...], lo_mask).astype(BF16)
        vb_ref[:, :LANES] = v_ref[0].astype(BF16)
        vb_ref[:, LANES:] = jnp.ones((L, LANES), BF16)

    if tq == L:
        prepare_kv()
    else:
        pl.when(qi == 0)(prepare_kv)

    lam = sc_ref[0]
    slope = sc_ref[1 + h] * LOG2E
    ts = min(ATTN_STRIP, tq)
    col = lax.broadcasted_iota(jnp.int32, (1, L), 1).astype(F32)

    def scores(i):
        r0 = i * ts
        qn = _half_rms(q_ref[0, r0:r0 + ts, :], qg_ref[...], lo_mask) * (DF_HEAD_DIM ** -0.5 * LOG2E)
        row = (lax.broadcasted_iota(jnp.int32, (ts, 1), 0) + (qi * tq + r0)).astype(F32)
        bias = jnp.abs(row - col) * slope
        kn = kn_ref[...]
        return (_dot_nt(jnp.where(lo_mask, qn, 0.0).astype(BF16), kn) - bias,
                _dot_nt(jnp.where(lo_mask, 0.0, qn).astype(BF16), kn) - bias)

    def exps(i, ss):
        return tuple(jnp.exp2(s - jnp.max(s, axis=-1, keepdims=True)).astype(BF16) for s in ss)

    def values(i, es):
        r1, r2 = (_dot(e, vb_ref[...]) for e in es)
        o = (r1[:, :LANES] * (1.0 / r1[:, LANES:LANES + 1])
             - r2[:, :LANES] * (lam / r2[:, LANES:LANES + 1]))
        o_ref[0, i * ts:(i + 1) * ts, :] = (_rms(o, sg_ref[...]) * out_scale).astype(o_ref.dtype)

    _pipeline3(tq // ts, scores, exps, values)


def diff_attention(proj, scal, q_g, k_g, subln_g, *, q_blk0, k_blk0, v_blk0, out_scale, tq):
    B, L, _ = proj.shape
    tq = min(tq, L)
    qg2 = jnp.concatenate([q_g, q_g]).reshape(1, LANES)
    kg2 = jnp.concatenate([k_g, k_g]).reshape(1, LANES)
    return pl.pallas_call(
        functools.partial(_diff_attn_kernel, out_scale=out_scale),
        grid=(B, DF_HEADS, L // tq),
        in_specs=[pl.BlockSpec(memory_space=pltpu.SMEM),
                  pl.BlockSpec((1, tq, LANES), lambda b, h, i: (b, i, q_blk0 + h)),
                  pl.BlockSpec((1, L, LANES), lambda b, h, i: (b, 0, k_blk0 + h)),
                  pl.BlockSpec((1, L, LANES), lambda b, h, i: (b, 0, v_blk0 + h)),
                  pl.BlockSpec((1, LANES), lambda b, h, i: (0, 0)),
                  pl.BlockSpec((1, LANES), lambda b, h, i: (0, 0)),
                  pl.BlockSpec((1, LANES), lambda b, h, i: (0, 0))],
        out_specs=pl.BlockSpec((1, tq, LANES), lambda b, h, i: (b, i, h)),
        out_shape=jax.ShapeDtypeStruct((B, L, DF_HEADS * DF_V_DIM), BF16),
        scratch_shapes=[pltpu.VMEM((L, LANES), BF16), pltpu.VMEM((L, 2 * LANES), BF16)],
        compiler_params=_params("parallel", "parallel", "arbitrary"),
        name="diff_attention",
    )(scal, proj, proj, proj, qg2, kg2, subln_g.reshape(1, LANES))


def _rope(x, cos, sin, first_mask):
    q = GQ_HEAD_DIM // 4
    rot = jnp.where(first_mask, -pltpu.roll(x, LANES - q, axis=1), pltpu.roll(x, q, axis=1))
    return x * cos + rot * sin


def _gqa_kernel(q_ref, k_ref, v_ref, cq_ref, sq_ref, ck_ref, sk_ref, qg_ref, kg_ref, o_ref, kn_ref, vb_ref):
    qi = pl.program_id(2)
    lane = lax.broadcasted_iota(jnp.int32, (1, LANES), 1)
    first_mask = (lane % (GQ_HEAD_DIM // 2)) < (GQ_HEAD_DIM // 4)

    @pl.when(qi == 0)
    def _():
        kn = _rms(k_ref[0], kg_ref[...])
        kn_ref[...] = _rope(kn, ck_ref[...], sk_ref[...], first_mask).astype(BF16)
        vb_ref[:, :LANES] = v_ref[0].astype(BF16)
        vb_ref[:, LANES:] = jnp.ones((k_ref.shape[1], LANES), BF16)

    scale = GQ_HEAD_DIM ** -0.5 * LOG2E
    tq = q_ref.shape[1]
    ts = min(ATTN_STRIP, tq)
    nr = tq // ts

    def where(i):
        r0 = (i % nr) * ts
        return slice(r0, r0 + ts), slice((i // nr) * GQ_HEAD_DIM, (i // nr + 1) * GQ_HEAD_DIM)

    def scores(i):
        rs, cs = where(i)
        qn = _rope(_rms(q_ref[0, rs, cs], qg_ref[...]), cq_ref[rs, :], sq_ref[rs, :], first_mask) * scale
        return _dot_nt(qn.astype(BF16), kn_ref[...])

    def exps(i, s):
        return jnp.exp2(s - jnp.max(s, axis=-1, keepdims=True)).astype(BF16)

    def values(i, e):
        rs, cs = where(i)
        res = _dot(e, vb_ref[...])
        o_ref[0, rs, cs] = (res[:, :LANES] * (1.0 / res[:, LANES:LANES + 1])).astype(o_ref.dtype)

    _pipeline3(GQ_GROUP * nr, scores, exps, values)


def gqa_attention(proj, cos, sin, q_g, k_g, *, q_blk0, k_blk0, v_blk0, tq):
    B, L, _ = proj.shape
    tq = min(tq, L)
    gw = GQ_GROUP * GQ_HEAD_DIM
    return pl.pallas_call(
        _gqa_kernel,
        grid=(B, GQ_KV_HEADS, L // tq),
        in_specs=[pl.BlockSpec((1, tq, gw), lambda b, g, i: (b, i, q_blk0 + g)),
                  pl.BlockSpec((1, L, LANES), lambda b, g, i: (b, 0, k_blk0 + g)),
                  pl.BlockSpec((1, L, LANES), lambda b, g, i: (b, 0, v_blk0 + g)),
                  pl.BlockSpec((tq, LANES), lambda b, g, i: (i, 0)),
                  pl.BlockSpec((tq, LANES), lambda b, g, i: (i, 0)),
                  pl.BlockSpec((L, LANES), lambda b, g, i: (0, 0)),
                  pl.BlockSpec((L, LANES), lambda b, g, i: (0, 0)),
                  pl.BlockSpec((1, LANES), lambda b, g, i: (0, 0)),
                  pl.BlockSpec((1, LANES), lambda b, g, i: (0, 0))],
        out_specs=pl.BlockSpec((1, tq, gw), lambda b, g, i: (b, i, g)),
        out_shape=jax.ShapeDtypeStruct((B, L, GQ_WIDTH), BF16),
        scratch_shapes=[pltpu.VMEM((L, LANES), BF16), pltpu.VMEM((L, 2 * LANES), BF16)],
        compiler_params=_params("parallel", "parallel", "arbitrary"),
        name="gqa_attention",
    )(proj, proj, proj, cos, sin, cos, sin, q_g.reshape(1, LANES), k_g.reshape(1, LANES))


def _hgrn2_pair_masks():
    pos = lax.broadcasted_iota(jnp.int32, (1, HG_BLOCK, LANES), 1)
    return {rev: [jnp.where((pos <= s) if rev else (pos >= s), 0.0, -jnp.inf).astype(F32)
                  for s in range(HG_BLOCK)] for rev in (False, True)}


def _hgrn2_intra_streams(streams, b_scr, pair_mask):
    C = HG_CHUNK
    nb = C // HG_BLOCK
    ns = range(len(streams))
    rev = [st[5] for st in streams]
    r_i = lax.broadcasted_iota(jnp.int32, (C, C), 0)
    c_i = lax.broadcasted_iota(jnp.int32, (C, C), 1)

    kk, lg = [], []
    for q, fz, v, vb, lb, _ in streams:
        f = lb + (1.0 - lb) * jax.nn.sigmoid(fz)
        kk.append(1.0 - f)
        lg.append(jnp.log(f) * LOG2E)

    tri_f = jnp.where(r_i >= c_i, 1.0, 0.0).astype(BF16)
    tri_b = jnp.where(r_i <= c_i, 1.0, 0.0).astype(BF16)
    b = []
    for i in ns:
        hi = lg[i].astype(BF16)
        r1 = lg[i] - hi.astype(F32)
        mid = r1.astype(BF16)
        lo = (r1 - mid.astype(F32)).astype(BF16)
        tri = tri_b if rev[i] else tri_f
        b.append(_dot(tri, hi) + _dot(tri, mid) + _dot(tri, lo))
    b_end = [b[i][0:1, :] if rev[i] else b[i][C - 1:C, :] for i in ns]
    qe = [(streams[i][0] * jnp.exp2(b[i])).astype(BF16) for i in ns]
    u = [_dot_tn(streams[i][3], (kk[i] * jnp.exp2(b_end[i] - b[i])).astype(BF16)) for i in ns]

    b3 = [b[i].reshape(nb, HG_BLOCK, LANES) for i in ns]
    q3 = [streams[i][0].reshape(nb, HG_BLOCK, LANES) for i in ns]
    seg_pos = lax.broadcasted_iota(jnp.int32, (C, HG_BLOCK * LANES), 1) // LANES
    row_pos = lax.broadcasted_iota(jnp.int32, (C, HG_BLOCK * LANES), 0) % HG_BLOCK
    kcat = []
    for i in ns:
        kb = kk[i].astype(BF16)
        kcat.append(jnp.where(seg_pos == row_pos, jnp.concatenate([kb] * HG_BLOCK, axis=1),
                              jnp.zeros((), BF16)))
    for i in ns:
        b_scr[i] = b[i]
    ps = [[] for i in ns]
    for s in range(HG_BLOCK):
        for i in ns:
            key_b = jnp.concatenate(
                [jnp.broadcast_to(b_scr[i, blk * HG_BLOCK + s:blk * HG_BLOCK + s + 1, :], (HG_BLOCK, LANES))
                 for blk in range(nb)], axis=0).reshape(nb, HG_BLOCK, LANES)
            d = (b3[i] - key_b) + pair_mask[rev[i]][s]
            ps[i].append((q3[i] * jnp.exp2(d)).reshape(C, LANES).astype(BF16))
    same_blk = (r_i // HG_BLOCK) == (c_i // HG_BLOCK)
    adiag = [_dot_nt(jnp.concatenate(ps[i], axis=1), kcat[i]) for i in ns]

    zrow = jnp.zeros((1, 1, LANES), F32)
    qt = []
    for i in ns:
        if rev[i]:
            rblk3 = jnp.concatenate([b3[i][1:, 0:1, :], zrow], axis=0)
        else:
            rblk3 = jnp.concatenate([zrow, b3[i][:nb - 1, HG_BLOCK - 1:HG_BLOCK, :]], axis=0)
        qt.append((q3[i] * jnp.exp2(b3[i] - rblk3)).reshape(C, LANES))
    rows = [[] for i in ns]
    for blk in range(nb):
        for i in ns:
            if rev[i]:
                edge = (blk + 1) * HG_BLOCK
                if edge == C:
                    rows[i].append(jnp.zeros((HG_BLOCK, C), F32))
                    continue
                kt = kk[i][edge:, :] * jnp.exp2(b[i][edge:edge + 1, :] - b[i][edge:, :])
                kfull = jnp.concatenate([jnp.zeros((edge, LANES), F32), kt], axis=0)
            else:
                edge = blk * HG_BLOCK
                if edge == 0:
                    rows[i].append(jnp.zeros((HG_BLOCK, C), F32))
                    continue
                kt = kk[i][:edge, :] * jnp.exp2(b[i][edge - 1:edge, :] - b[i][:edge, :])
                kfull = jnp.concatenate([kt, jnp.zeros((C - edge, LANES), F32)], axis=0)
            rows[i].append(_dot_nt(qt[i][blk * HG_BLOCK:(blk + 1) * HG_BLOCK, :].astype(BF16),
                                   kfull.astype(BF16)))
    outs = []
    for i in ns:
        att = jnp.where(same_blk, adiag[i], jnp.concatenate(rows[i], axis=0))
        o = _dot(att.astype(BF16), streams[i][3])
        outs.append((o, qe[i], u[i], jnp.exp2(b_end[i])))
    return outs


def _hgrn2_kernel(q_ref, ff_ref, fb_ref, i_ref, g_ref, lbf_ref, lbb_ref, ng_ref, o_ref, of_ref, ob_ref,
                  sf_ref, sb_ref, b_scr):
    L = q_ref.shape[1]
    C = HG_CHUNK
    n = L // C
    gsize = math.gcd(HG_INTRA_CHUNKS, n)
    qscale = HG_DK ** -0.5

    def rows(c):
        return pl.ds(pl.multiple_of(c * C, C), C)

    sf_ref[...] = jnp.zeros_like(sf_ref)
    sb_ref[...] = jnp.zeros_like(sb_ref)
    pair_mask = _hgrn2_pair_masks()

    def group(it, carry):
        streams = []
        dests = []
        for k in range(gsize):
            c = it * gsize + k
            for fz_ref, lb_ref, o_s, s_s, cc, reverse in ((ff_ref, lbf_ref, of_ref, sf_ref, c, False),
                                                          (fb_ref, lbb_ref, ob_ref, sb_ref, n - 1 - c, True)):
                r = rows(cc)
                v = i_ref[0, r, :]
                streams.append((q_ref[0, r, :] * qscale, fz_ref[0, r, :], v, v.astype(BF16), lb_ref[0], reverse))
                dests.append((o_s, s_s, r))
        state = {id(sf_ref): sf_ref[...], id(sb_ref): sb_ref[...]}
        for (o, qe, u, e), (o_s, s_s, r) in zip(_hgrn2_intra_streams(streams, b_scr, pair_mask), dests):
            st = state[id(s_s)]
            o_s[r, :] = o + _dot_nt(qe, st.astype(BF16))
            state[id(s_s)] = st * e + u
        sf_ref[...] = state[id(sf_ref)]
        sb_ref[...] = state[id(sb_ref)]
        return carry

    lax.fori_loop(0, n // gsize, group, 0)
    o = _rms(of_ref[...] + ob_ref[...], ng_ref[...])
    g = g_ref[0]
    o_ref[0] = (o * (g * jax.nn.sigmoid(g))).astype(o_ref.dtype)


def hgrn2_mixer(proj, lb_f, lb_b, norm_g):
    B, L, _ = proj.shape
    H = HG_HEADS

    def col(k):
        return pl.BlockSpec((1, L, LANES), lambda b, h: (b, 0, k * H + h))

    lbspec = pl.BlockSpec((1, 1, LANES), lambda b, h: (h, 0, 0))
    return pl.pallas_call(
        _hgrn2_kernel,
        grid=(B, H),
        in_specs=[col(0), col(1), col(2), col(3), col(4), lbspec, lbspec,
                  pl.BlockSpec((1, LANES), lambda b, h: (0, 0))],
        out_specs=pl.BlockSpec((1, L, LANES), lambda b, h: (b, 0, h)),
        out_shape=jax.ShapeDtypeStruct((B, L, HG_WIDTH), BF16),
        scratch_shapes=[pltpu.VMEM((L, LANES), F32), pltpu.VMEM((L, LANES), F32),
                        pltpu.VMEM((LANES, LANES), F32), pltpu.VMEM((LANES, LANES), F32),
                        pltpu.VMEM((2 * math.gcd(HG_INTRA_CHUNKS, L // HG_CHUNK), HG_CHUNK, LANES), F32)],
        compiler_params=_params("parallel", "arbitrary"),
        name="hgrn2",
    )(proj, proj, proj, proj, proj, lb_f.reshape(H, 1, LANES), lb_b.reshape(H, 1, LANES),
      norm_g.reshape(1, LANES))


def _hyena_filter_kernel(z_ref, t_ref, w1_ref, b1_ref, w2_ref, b2_ref, w3_ref, b3_ref, w4f_ref, w4b_ref,
                         fr_ref, ad_ref, gs_ref, gd_ref, kn_ref):
    def hdot(a, b):
        return jnp.dot(a, b, precision=HIGHEST, preferred_element_type=F32)

    fr = fr_ref[...]
    h = jnp.sin(fr * (hdot(z_ref[...], w1_ref[...]) + b1_ref[...]))
    h = jnp.sin(fr * (hdot(h, w2_ref[...]) + b2_ref[...]))
    h = jnp.sin(fr * (hdot(h, w3_ref[...]) + b3_ref[...]))
    window = jnp.exp(-t_ref[...] * ad_ref[...])
    hf = hdot(h, w4f_ref[...]) * window
    hb = hdot(h, w4b_ref[...]) * window
    row = lax.broadcasted_iota(jnp.int32, hb.shape, 0)
    hb = jnp.where(row == 0, 0.0, hb)
    gs_ref[...] = hf + hb
    gd_ref[...] = hb - hf
    sgn = jnp.where(row % 2 == 0, 1.0, -1.0)
    kn_ref[...] = jnp.sum((hf + hb) * sgn, axis=0, keepdims=True) * (0.5 / hb.shape[0])


def _split2(x):
    hi = x.astype(BF16)
    return hi, (x - hi.astype(F32)).astype(BF16)


def _hyena_spectrum_kernel(ah_ref, al_ref, bh_ref, bl_ref, gs_ref, gd_ref, kre_ref, kim_ref, sh_ref, sl_ref,
                           dh_ref, dl_ref, *, n_fft):
    @pl.when(pl.program_id(1) == 0)
    def _():
        sh_ref[...], sl_ref[...] = _split2(gs_ref[...])
        dh_ref[...], dl_ref[...] = _split2(gd_ref[...])

    tf = ah_ref.shape[0]
    f0 = pl.program_id(1) * tf
    row = lax.broadcasted_iota(jnp.int32, (tf, 1), 0) + f0
    wf = jnp.where(row == 0, 1.0 / n_fft, 2.0 / n_fft)
    ah = ah_ref[...]
    bh = bh_ref[...]
    kre_ref[...] = wf * (_dot(ah, sh_ref[...]) + _dot(ah, sl_ref[...]) + _dot(al_ref[...], sh_ref[...]))
    kim_ref[...] = wf * (_dot(bh, dh_ref[...]) + _dot(bh, dl_ref[...]) + _dot(bl_ref[...], dh_ref[...]))


def hyena_filter_spectrum(L, cos_hi, cos_lo, sin_hi, sin_lo, w1, b1, w2, b2, w3, b3, w4, sin_freq):
    C = HY_WIDTH
    t = jnp.linspace(0.0, 1.0, L, dtype=F32)[:, None]
    w = (2.0 * math.pi / L) * jnp.arange(L, dtype=F32)[:, None]
    f = jnp.linspace(1e-4, HY_BANDS - 1, HY_BANDS, dtype=F32)[None, :]
    z = jnp.concatenate([t, jnp.cos(f * w), -jnp.sin(f * w)], axis=-1)
    zp = jnp.pad(z, ((0, 0), (0, LANES - HY_EMB)))
    w1p = jnp.pad(w1, ((0, LANES - HY_EMB), (0, 0)))
    max_decay = math.log(HY_TARGET) / HY_FAST_PCT
    min_decay = math.log(HY_TARGET) / HY_SLOW_PCT
    absd = jnp.abs(jnp.linspace(min_decay, max_decay, C, dtype=F32))[None, :]
    tc = min(512, C)
    full = lambda shape: pl.BlockSpec(shape, lambda j: (0, 0))
    gs, gd, knyq = pl.pallas_call(
        _hyena_filter_kernel,
        grid=(C // tc,),
        in_specs=[full((L, LANES)), full((L, 1)), full((LANES, HY_FILT)), full((1, HY_FILT)),
                  full((HY_FILT, HY_FILT)), full((1, HY_FILT)), full((HY_FILT, HY_FILT)), full((1, HY_FILT)),
                  pl.BlockSpec((HY_FILT, tc), lambda j: (0, j)),
                  pl.BlockSpec((HY_FILT, tc), lambda j: (0, C // tc + j)),
                  full((1, HY_FILT)),
                  pl.BlockSpec((1, tc), lambda j: (0, j))],
        out_specs=[pl.BlockSpec((L, tc), lambda j: (0, j)), pl.BlockSpec((L, tc), lambda j: (0, j)),
                   pl.BlockSpec((1, tc), lambda j: (0, j))],
        out_shape=[jax.ShapeDtypeStruct((L, C), F32), jax.ShapeDtypeStruct((L, C), F32),
                   jax.ShapeDtypeStruct((1, C), F32)],
        compiler_params=_params("arbitrary"),
        name="hyena_filter",
    )(zp, t, w1p, b1.reshape(1, -1), w2, b2.reshape(1, -1), w3, b3.reshape(1, -1), w4, w4,
      sin_freq.reshape(1, -1), absd)

    tf = min(256, L)
    tbl = pl.BlockSpec((tf, L), lambda j, i: (i, 0))
    kre, kim = pl.pallas_call(
        functools.partial(_hyena_spectrum_kernel, n_fft=2 * L),
        grid=(C // tc, L // tf),
        in_specs=[tbl, tbl, tbl, tbl,
                  pl.BlockSpec((L, tc), lambda j, i: (0, j)),
                  pl.BlockSpec((L, tc), lambda j, i: (0, j))],
        out_specs=[pl.BlockSpec((tf, tc), lambda j, i: (i, j)), pl.BlockSpec((tf, tc), lambda j, i: (i, j))],
        out_shape=[jax.ShapeDtypeStruct((L, C), F32), jax.ShapeDtypeStruct((L, C), F32)],
        scratch_shapes=[pltpu.VMEM((L, tc), BF16)] * 4,
        compiler_params=_params("parallel", "arbitrary"),
        name="hyena_spectrum",
    )(cos_hi, cos_lo, sin_hi, sin_lo, gs, gd)
    return kre, kim, knyq


def _short_conv(u, w_ref, b_ref):
    L = u.shape[0]
    row = lax.broadcasted_iota(jnp.int32, (L, 1), 0)
    prev = jnp.where(row == 0, 0.0, pltpu.roll(u, 1, axis=0))
    nxt = jnp.where(row == L - 1, 0.0, pltpu.roll(u, L - 1, axis=0))
    return prev * w_ref[0:1, :] + u * w_ref[1:2, :] + nxt * w_ref[2:3, :] + b_ref[...]


def _hyena_conv_kernel(x1_ref, x2_ref, v_ref, cw1_ref, cw2_ref, cw3_ref, cb1_ref, cb2_ref, cb3_ref,
                       a_ref, b_ref, kre_ref, kim_ref, knyq_ref, bias_ref, o_ref, r1_ref, r2_ref, *, tf):
    L = v_ref.shape[1]
    x2 = _short_conv(x2_ref[0], cw2_ref, cb2_ref)
    vv = _short_conv(v_ref[0], cw3_ref, cb3_ref)
    vx = vv * x2
    vxb = vx.astype(BF16)
    sgn = jnp.where(lax.broadcasted_iota(jnp.int32, (L, 1), 0) % 2 == 0, 1.0, -1.0)
    def forward(fb):
        fs = slice(fb * tf, (fb + 1) * tf)
        return _dot(a_ref[fs, :], vxb), _dot(b_ref[fs, :], vxb)

    def product(fb, pq):
        p, q = pq
        fs = slice(fb * tf, (fb + 1) * tf)
        kre = kre_ref[fs, :]
        kim = kim_ref[fs, :]
        r1_ref[fs, :] = (p * kre + q * kim).astype(BF16)
        r2_ref[fs, :] = (q * kre - p * kim).astype(BF16)

    _pipeline3(L // tf, forward, product, lambda fb, _: None)

    v_nyq = jnp.sum(vx * sgn, axis=0, keepdims=True)
    y0 = vx * bias_ref[...] + sgn * (v_nyq * knyq_ref[...])
    x1 = _short_conv(x1_ref[0], cw1_ref, cb1_ref)
    y = _dot(a_ref[...], r1_ref[...]) + _dot(b_ref[...], r2_ref[...])
    o_ref[0] = ((y0 + y) * x1).astype(o_ref.dtype)


def hyena_conv(proj, conv_w, conv_b, cos_b, sin_b, kre, kim, knyq, bias, *, tc):
    B, L, _ = proj.shape
    C = HY_WIDTH
    tc = min(tc, C)
    nc = C // tc
    tf = min(512, L)

    def grp(k):
        return pl.BlockSpec((1, L, tc), lambda c, b: (b, 0, k * nc + c))

    def cw(k):
        return pl.BlockSpec((HY_SHORT, tc), lambda c, b: (0, k * nc + c))

    def cb(k):
        return pl.BlockSpec((1, tc), lambda c, b: (0, k * nc + c))

    tbl = pl.BlockSpec((L, L), lambda c, b: (0, 0), pipeline_mode=pl.Buffered(1))
    chan = pl.BlockSpec((L, tc), lambda c, b: (0, c), pipeline_mode=pl.Buffered(1))
    vec = pl.BlockSpec((1, tc), lambda c, b: (0, c))
    return pl.pallas_call(
        functools.partial(_hyena_conv_kernel, tf=tf),
        grid=(nc, B),
        in_specs=[grp(0), grp(1), grp(2), cw(0), cw(1), cw(2), cb(0), cb(1), cb(2),
                  tbl, tbl, chan, chan, vec, vec],
        out_specs=pl.BlockSpec((1, L, tc), lambda c, b: (b, 0, c)),
        out_shape=jax.ShapeDtypeStruct((B, L, C), BF16),
        scratch_shapes=[pltpu.VMEM((L, tc), BF16), pltpu.VMEM((L, tc), BF16)],
        compiler_params=_params("parallel", "arbitrary"),
        name="hyena_conv",
    )(proj, proj, proj, conv_w, conv_w, conv_w, conv_b.reshape(1, -1), conv_b.reshape(1, -1),
      conv_b.reshape(1, -1), cos_b, sin_b, kre, kim, knyq, bias.reshape(1, -1))


def _dft_tables(L):
    n = 2 * L
    r = math.gcd(GRID_W, L)
    j = jnp.arange(L, dtype=jnp.int32)[None, :]

    def cs(f):
        ang = ((f[:, None] * j) % n).astype(F32) * (2.0 * math.pi / n)
        return jnp.cos(ang), jnp.sin(ang)

    c0, s0 = cs(jnp.arange(r, dtype=jnp.int32))
    c1, s1 = cs(jnp.arange(L // r, dtype=jnp.int32) * r)
    cos_m = (c1[:, None, :] * c0[None, :, :] - s1[:, None, :] * s0[None, :, :]).reshape(L, L)
    sin_m = (s1[:, None, :] * c0[None, :, :] + c1[:, None, :] * s0[None, :, :]).reshape(L, L)
    cos_hi = cos_m.astype(BF16)
    sin_hi = sin_m.astype(BF16)
    return (cos_hi, (cos_m - cos_hi.astype(F32)).astype(BF16),
            sin_hi, (sin_m - sin_hi.astype(F32)).astype(BF16))


def _axial_rope_tables(L):
    rows = L // GRID_W
    r, c = jnp.meshgrid(jnp.arange(rows, dtype=F32), jnp.arange(GRID_W, dtype=F32), indexing='ij')
    r = r.reshape(-1)
    c = c.reshape(-1)
    half = GQ_HEAD_DIM // 2
    inv = ROPE_THETA ** (-jnp.arange(0, half, 2, dtype=F32) / half)
    ang_r = r[:, None] * inv[None, :]
    ang_c = c[:, None] * inv[None, :]
    ang = jnp.concatenate([ang_r, ang_r, ang_c, ang_c], axis=-1)
    return jnp.cos(ang), jnp.sin(ang)


def _even_mixer(x, layer, norm_g, w_in, conv_w, conv_b, fw1, fb1, fw2, fb2, fw3, fb3, fw4, sin_freq,
                hy_bias, q_g, k_g, lam_q1, lam_k1, lam_q2, lam_k2, subln_g):
    B, L, D = x.shape
    x2 = x.reshape(B * L, D)
    proj = norm_matmul(x2, norm_g, w_in.astype(BF16), tm=1024, tn=1536).reshape(B, L, -1)
    cos_hi, cos_lo, sin_hi, sin_lo = _dft_tables(L)
    kre, kim, knyq = hyena_filter_spectrum(L, cos_hi, cos_lo, sin_hi, sin_lo, fw1, fb1, fw2, fb2, fw3, fb3, fw4,
                                           sin_freq)
    o_a = hyena_conv(proj, conv_w, conv_b, cos_hi, sin_hi, kre, kim, knyq, hy_bias,
                     tc=256)
    lam_init = 0.8 - 0.6 * math.exp(-0.3 * layer)
    lam = (jnp.exp(jnp.sum(lam_q1.astype(F32) * lam_k1.astype(F32)))
           - jnp.exp(jnp.sum(lam_q2.astype(F32) * lam_k2.astype(F32))) + lam_init)
    slopes = jnp.asarray(np.array([2.0 ** (-8.0 * (h + 1) / DF_HEADS) for h in range(DF_HEADS)],
                                  dtype=np.float32))
    scal = jnp.concatenate([lam.reshape(1), slopes]).astype(F32)
    nb_hy = 3 * HY_WIDTH // LANES
    nb_qk = DF_QK_WIDTH // LANES
    o_b = diff_attention(proj, scal, q_g, k_g, subln_g, q_blk0=nb_hy, k_blk0=nb_hy + nb_qk,
                         v_blk0=nb_hy + 2 * nb_qk, out_scale=1.0 - lam_init, tq=2048)
    return o_a, o_b


def _odd_mixer(x, norm_g, lb_f, lb_b, w_in, hg_norm_g, q_g, k_g):
    B, L, D = x.shape
    x2 = x.reshape(B * L, D)
    proj = norm_matmul(x2, norm_g, w_in.astype(BF16), tm=1024, tn=1664).reshape(B, L, -1)
    o_c = hgrn2_mixer(proj, lb_f, lb_b, hg_norm_g)
    cos, sin = _axial_rope_tables(L)
    gw = GQ_GROUP * GQ_HEAD_DIM
    q0 = 5 * HG_WIDTH
    k0 = q0 + GQ_WIDTH
    v0 = k0 + GQ_KV_HEADS * GQ_HEAD_DIM
    o_d = gqa_attention(proj, cos, sin, q_g, k_g, q_blk0=q0 // gw, k_blk0=k0 // LANES, v_blk0=v0 // LANES,
                        tq=1024)
    return o_c, o_d


def kernel(x, mem, norm_mix_g, norm_mem_q_g, norm_mem_kv_g, norm_ffn_g, ev_w_in, ev_w_out, hy_conv_w, hy_conv_b, hy_fw1, hy_fb1, hy_fw2, hy_fb2, hy_fw3, hy_fb3, hy_fw4, hy_sin_freq, hy_bias, df_q_g, df_k_g, df_lam_q1, df_lam_k1, df_lam_q2, df_lam_k2, df_subln_g, od_w_in, od_w_out, hg_lb_logits, hg_norm_g, gq_q_g, gq_k_g, ca_w_q, ca_w_kv, ca_w_out, ca_q_g, ca_k_g, mlp_w1, mlp_w2):
    B, L, D = x.shape
    M = mem.shape[1]
    depth = norm_mix_g.shape[0]
    lb_soft = jax.nn.softmax(hg_lb_logits.astype(F32), axis=1)
    lbs = jnp.cumsum(lb_soft, axis=1)
    lbs = lbs - lbs[:, :1]
    mem2 = mem.reshape(B * M, D)
    for layer in range(depth):
        j = layer // 2
        if layer % 2 == 0:
            a1, a2 = _even_mixer(x, layer, norm_mix_g[layer], ev_w_in[j], hy_conv_w[j], hy_conv_b[j],
                                 hy_fw1[j], hy_fb1[j], hy_fw2[j], hy_fb2[j], hy_fw3[j], hy_fb3[j], hy_fw4[j],
                                 hy_sin_freq[j], hy_bias[j], df_q_g[j], df_k_g[j], df_lam_q1[j], df_lam_k1[j],
                                 df_lam_q2[j], df_lam_k2[j], df_subln_g[j])
            w_mix = ev_w_out[j]
        else:
            a1, a2 = _odd_mixer(x, norm_mix_g[layer], lbs[0, layer], lbs[1, layer], od_w_in[j],
                                hg_norm_g[j], gq_q_g[j], gq_k_g[j])
            w_mix = od_w_out[j]
        kv = norm_matmul(mem2, norm_mem_kv_g[layer], ca_w_kv[layer].astype(BF16), tm=1024, tn=1024)
        x = mix_proj_cross_attention(a1, a2, w_mix.astype(BF16), x, kv.reshape(B, M, -1), norm_mem_q_g[layer],
                                     ca_w_q[layer].astype(BF16), ca_q_g[layer], ca_k_g[layer],
                                     ca_w_out[layer].astype(BF16), tq=512)
        x = mlp_residual(x.reshape(B * L, D), norm_ffn_g[layer], layer_weight_bf16(mlp_w1, layer),
                         layer_weight_bf16(mlp_w2, layer), tm=512, th=2048).reshape(B, L, D)
    return x
```

```python
import functools
import math

import numpy as np
import jax
import jax.numpy as jnp
from jax import lax
from jax.experimental import pallas as pl
from jax.experimental.pallas import tpu as pltpu

F32 = jnp.float32
BF16 = jnp.bfloat16
HIGHEST = lax.Precision.HIGHEST

D_MODEL = 2048
NORM_EPS = 1e-6
GRID_W = 64
HY_WIDTH = D_MODEL // 2
HY_EMB = 33
HY_BANDS = (HY_EMB - 1) // 2
HY_FILT = 64
HY_SHORT = 3
HY_TARGET = 1e-2
HY_FAST_PCT = 0.3
HY_SLOW_PCT = 1.5
DF_HEADS = 8
DF_HEAD_DIM = 64
DF_V_DIM = 128
DF_QK_WIDTH = DF_HEADS * 2 * DF_HEAD_DIM
HG_WIDTH = D_MODEL // 2
HG_HEADS = 8
HG_DK = HG_WIDTH // HG_HEADS
GQ_WIDTH = D_MODEL - HG_WIDTH
GQ_HEADS = 8
GQ_KV_HEADS = 2
GQ_HEAD_DIM = GQ_WIDTH // GQ_HEADS
GQ_GROUP = GQ_HEADS // GQ_KV_HEADS
ROPE_THETA = 10000.0
CA_HEADS = 4
CA_HEAD_DIM = 128
CA_WIDTH = CA_HEADS * CA_HEAD_DIM

LANES = 128
SUBLANES = 8
VMEM_LIMIT_BYTES = 56 * 1024 * 1024

HG_CHUNK = 64
HG_BLOCK = SUBLANES
ATTN_STRIP = 128
HG_INTRA_CHUNKS = 16
DFT_ROW_SPLIT = 64
LOG2E = 1.4426950408889634


def _params(*sem):
    return pltpu.CompilerParams(dimension_semantics=sem, vmem_limit_bytes=VMEM_LIMIT_BYTES)


def _rms(x, g):
    return x * lax.rsqrt(jnp.mean(x * x, axis=-1, keepdims=True) + NORM_EPS) * g


def _dot(a, b):
    return jnp.dot(a, b, preferred_element_type=F32)


def _dot_nt(a, b):
    return lax.dot_general(a, b, (((1,), (1,)), ((), ())), preferred_element_type=F32)


def _dot_tn(a, b):
    return lax.dot_general(a, b, (((0,), (0,)), ((), ())), preferred_element_type=F32)


def _pipeline3(n, stage_a, stage_b, stage_c):
    a = {}
    b = {}
    for t in range(n + 2):
        if t < n:
            a[t] = stage_a(t)
        if 0 <= t - 1 < n:
            b[t - 1] = stage_b(t - 1, a.pop(t - 1))
        if 0 <= t - 2 < n:
            stage_c(t - 2, b.pop(t - 2))


def _cast_kernel(w_ref, o_ref):
    o_ref[...] = w_ref[0].astype(o_ref.dtype)


def layer_weight_bf16(w, layer, *, block_bytes=8 * 1024 * 1024):
    _, K, N = w.shape
    tr = min(K, block_bytes // (4 * N))
    return pl.pallas_call(
        _cast_kernel,
        grid=(K // tr,),
        in_specs=[pl.BlockSpec((1, tr, N), lambda i: (layer, i, 0))],
        out_specs=pl.BlockSpec((tr, N), lambda i: (i, 0)),
        out_shape=jax.ShapeDtypeStruct((K, N), BF16),
        compiler_params=_params("parallel"),
        name="weight_bf16",
    )(w)


def _norm_matmul_kernel(x_ref, g_ref, w_ref, o_ref, hn_ref):
    @pl.when(pl.program_id(1) == 0)
    def _():
        hn_ref[...] = _rms(x_ref[...], g_ref[...]).astype(BF16)

    o_ref[...] = _dot(hn_ref[...], w_ref[...]).astype(o_ref.dtype)


def norm_matmul(x, g, w, *, tm, tn, out_dtype=F32):
    M, D = x.shape
    N = w.shape[1]
    tm = min(tm, M)
    tn = min(tn, N)
    return pl.pallas_call(
        _norm_matmul_kernel,
        grid=(M // tm, N // tn),
        in_specs=[pl.BlockSpec((tm, D), lambda i, j: (i, 0)),
                  pl.BlockSpec((1, D), lambda i, j: (0, 0)),
                  pl.BlockSpec((D, tn), lambda i, j: (0, j))],
        out_specs=pl.BlockSpec((tm, tn), lambda i, j: (i, j)),
        out_shape=jax.ShapeDtypeStruct((M, N), out_dtype),
        scratch_shapes=[pltpu.VMEM((tm, D), BF16)],
        compiler_params=_params("parallel", "arbitrary"),
        name="norm_matmul",
    )(x, g.reshape(1, D), w)


def _mlp_kernel(x_ref, g_ref, w1_ref, w2_ref, o_ref, hn_ref, acc_ref):
    j = pl.program_id(1)

    last = pl.num_programs(1) - 1

    @pl.when(j == 0)
    def _():
        x = x_ref[...]
        hn_ref[...] = _rms(x, g_ref[...]).astype(BF16)
        acc_ref[...] = x

    def accumulated():
        h1 = jnp.maximum(_dot(hn_ref[...], w1_ref[...]), 0.0)
        return acc_ref[...] + _dot((h1 * h1).astype(BF16), w2_ref[...])

    @pl.when(j != last)
    def _():
        acc_ref[...] = accumulated()

    @pl.when(j == last)
    def _():
        o_ref[...] = accumulated()


def mlp_residual(x, g, w1, w2, *, tm, th):
    M, D = x.shape
    H = w1.shape[1]
    tm = min(tm, M)
    th = min(th, H)
    return pl.pallas_call(
        _mlp_kernel,
        grid=(M // tm, H // th),
        in_specs=[pl.BlockSpec((tm, D), lambda i, j: (i, 0)),
                  pl.BlockSpec((1, D), lambda i, j: (0, 0)),
                  pl.BlockSpec((D, th), lambda i, j: (0, j)),
                  pl.BlockSpec((th, D), lambda i, j: (j, 0))],
        out_specs=pl.BlockSpec((tm, D), lambda i, j: (i, 0)),
        out_shape=jax.ShapeDtypeStruct((M, D), F32),
        scratch_shapes=[pltpu.VMEM((tm, D), BF16), pltpu.VMEM((tm, D), F32)],
        compiler_params=_params("parallel", "arbitrary"),
        name="mlp_residual",
    )(x, g.reshape(1, D), w1, w2)


def _cross_attn_kernel(a1_ref, a2_ref, w1_ref, w2_ref, x_ref, g_ref, wq_ref, kv_ref, qg_ref, kg_ref, wo_ref,
                       o_ref):
    x = x_ref[0] + _dot(a1_ref[0], w1_ref[...]) + _dot(a2_ref[0], w2_ref[...])
    hn = _rms(x, g_ref[...]).astype(BF16)
    q = _dot(hn, wq_ref[...])
    kv = kv_ref[0]
    scale = CA_HEAD_DIM ** -0.5 * LOG2E
    ones = jnp.ones((kv.shape[0], CA_HEAD_DIM), BF16)
    outs = []
    for h in range(CA_HEADS):
        sl = slice(h * CA_HEAD_DIM, (h + 1) * CA_HEAD_DIM)
        qh = (_rms(q[:, sl], qg_ref[...]) * scale).astype(BF16)
        kh = _rms(kv[:, sl], kg_ref[...]).astype(BF16)
        vh = kv[:, CA_WIDTH + h * CA_HEAD_DIM:CA_WIDTH + (h + 1) * CA_HEAD_DIM].astype(BF16)
        s = _dot_nt(qh, kh)
        e = jnp.exp2(s - jnp.max(s, axis=-1, keepdims=True)).astype(BF16)
        res = _dot(e, jnp.concatenate([vh, ones], axis=1))
        outs.append(res[:, :CA_HEAD_DIM] * (1.0 / res[:, CA_HEAD_DIM:CA_HEAD_DIM + 1]))
    o = jnp.concatenate(outs, axis=-1).astype(BF16)
    o_ref[0] = x + _dot(o, wo_ref[...])


def mix_proj_cross_attention(a1, a2, w_mix, x, kv, g, wq, qg, kg, wo, *, tq):
    B, L, D = x.shape
    M = kv.shape[1]
    K1 = a1.shape[2]
    assert a2.shape[2] == K1 and w_mix.shape[0] == 2 * K1
    tq = min(tq, L)
    return pl.pallas_call(
        _cross_attn_kernel,
        grid=(B, L // tq),
        in_specs=[pl.BlockSpec((1, tq, K1), lambda b, i: (b, i, 0)),
                  pl.BlockSpec((1, tq, K1), lambda b, i: (b, i, 0)),
                  pl.BlockSpec((K1, D), lambda b, i: (0, 0), pipeline_mode=pl.Buffered(1)),
                  pl.BlockSpec((K1, D), lambda b, i: (1, 0), pipeline_mode=pl.Buffered(1)),
                  pl.BlockSpec((1, tq, D), lambda b, i: (b, i, 0)),
                  pl.BlockSpec((1, D), lambda b, i: (0, 0)),
                  pl.BlockSpec((D, CA_WIDTH), lambda b, i: (0, 0)),
                  pl.BlockSpec((1, M, 2 * CA_WIDTH), lambda b, i: (b, 0, 0)),
                  pl.BlockSpec((1, CA_HEAD_DIM), lambda b, i: (0, 0)),
                  pl.BlockSpec((1, CA_HEAD_DIM), lambda b, i: (0, 0)),
                  pl.BlockSpec((CA_WIDTH, D), lambda b, i: (0, 0))],
        out_specs=pl.BlockSpec((1, tq, D), lambda b, i: (b, i, 0)),
        out_shape=jax.ShapeDtypeStruct((B, L, D), F32),
        compiler_params=_params("parallel", "arbitrary"),
        name="cross_attention",
    )(a1, a2, w_mix, w_mix, x, g.reshape(1, D), wq, kv, qg.reshape(1, -1), kg.reshape(1, -1), wo)


def _half_rms(x, g2, lo_mask):
    sq = x * x
    ms_lo = jnp.sum(jnp.where(lo_mask, sq, 0.0), axis=-1, keepdims=True)
    ms_hi = jnp.sum(jnp.where(lo_mask, 0.0, sq), axis=-1, keepdims=True)
    inv = jnp.where(lo_mask, lax.rsqrt(ms_lo * (1.0 / DF_HEAD_DIM) + NORM_EPS),
                    lax.rsqrt(ms_hi * (1.0 / DF_HEAD_DIM) + NORM_EPS))
    return x * inv * g2


def _diff_attn_kernel(sc_ref, q_ref, k_ref, v_ref, qg_ref, kg_ref, sg_ref, o_ref, kn_ref, vb_ref, *,
                      out_scale):
    h = pl.program_id(1)
    qi = pl.program_id(2)
    tq = q_ref.shape[1]
    L = k_ref.shape[1]
    lane = lax.broadcasted_iota(jnp.int32, (1, LANES), 1)
    lo_mask = lane < DF_HEAD_DIM

    def prepare_kv():
        kn_ref[...] = _half_rms(k_ref[0], kg_ref[...], lo_mask).astype(BF16)
        vb_ref[:, :LANES] = v_ref[0].astype(BF16)
        vb_ref[:, LANES:] = jnp.ones((L, LANES), BF16)

    if tq == L:
        prepare_kv()
    else:
        pl.when(qi == 0)(prepare_kv)

    lam = sc_ref[0]
    slope = sc_ref[1 + h] * LOG2E
    ts = min(ATTN_STRIP, tq)
    col = lax.broadcasted_iota(jnp.int32, (1, L), 1).astype(F32)

    def scores(i):
        r0 = i * ts
        qn = _half_rms(q_ref[0, r0:r0 + ts, :], qg_ref[...], lo_mask) * (DF_HEAD_DIM ** -0.5 * LOG2E)
        row = (lax.broadcasted_iota(jnp.int32, (ts, 1), 0) + (qi * tq + r0)).astype(F32)
        bias = jnp.abs(row - col) * slope
        kn = kn_ref[...]
        return (_dot_nt(jnp.where(lo_mask, qn, 0.0).astype(BF16), kn) - bias,
                _dot_nt(jnp.where(lo_mask, 0.0, qn).astype(BF16), kn) - bias)

    def exps(i, ss):
        return tuple(jnp.exp2(s - jnp.max(s, axis=-1, keepdims=True)).astype(BF16) for s in ss)

    def values(i, es):
        r1, r2 = (_dot(e, vb_ref[...]) for e in es)
        o = (r1[:, :LANES] * (1.0 / r1[:, LANES:LANES + 1])
             - r2[:, :LANES] * (lam / r2[:, LANES:LANES + 1]))
        o_ref[0, i * ts:(i + 1) * ts, :] = (_rms(o, sg_ref[...]) * out_scale).astype(o_ref.dtype)

    _pipeline3(tq // ts, scores, exps, values)


def diff_attention(proj, scal, q_g, k_g, subln_g, *, q_blk0, k_blk0, v_blk0, out_scale, tq):
    B, L, _ = proj.shape
    tq = min(tq, L)
    qg2 = jnp.concatenate([q_g, q_g]).reshape(1, LANES)
    kg2 = jnp.concatenate([k_g, k_g]).reshape(1, LANES)
    return pl.pallas_call(
        functools.partial(_diff_attn_kernel, out_scale=out_scale),
        grid=(B, DF_HEADS, L // tq),
        in_specs=[pl.BlockSpec(memory_space=pltpu.SMEM),
                  pl.BlockSpec((1, tq, LANES), lambda b, h, i: (b, i, q_blk0 + h)),
                  pl.BlockSpec((1, L, LANES), lambda b, h, i: (b, 0, k_blk0 + h)),
                  pl.BlockSpec((1, L, LANES), lambda b, h, i: (b, 0, v_blk0 + h)),
                  pl.BlockSpec((1, LANES), lambda b, h, i: (0, 0)),
                  pl.BlockSpec((1, LANES), lambda b, h, i: (0, 0)),
                  pl.BlockSpec((1, LANES), lambda b, h, i: (0, 0))],
        out_specs=pl.BlockSpec((1, tq, LANES), lambda b, h, i: (b, i, h)),
        out_shape=jax.ShapeDtypeStruct((B, L, DF_HEADS * DF_V_DIM), BF16),
        scratch_shapes=[pltpu.VMEM((L, LANES), BF16), pltpu.VMEM((L, 2 * LANES), BF16)],
        compiler_params=_params("parallel", "parallel", "arbitrary"),
        name="diff_attention",
    )(scal, proj, proj, proj, qg2, kg2, subln_g.reshape(1, LANES))


def _rope(x, cos, sin, first_mask):
    q = GQ_HEAD_DIM // 4
    rot = jnp.where(first_mask, -pltpu.roll(x, LANES - q, axis=1), pltpu.roll(x, q, axis=1))
    return x * cos + rot * sin


def _gqa_kernel(q_ref, k_ref, v_ref, cq_ref, sq_ref, ck_ref, sk_ref, qg_ref, kg_ref, o_ref, kn_ref, vb_ref):
    qi = pl.program_id(2)
    lane = lax.broadcasted_iota(jnp.int32, (1, LANES), 1)
    first_mask = (lane % (GQ_HEAD_DIM // 2)) < (GQ_HEAD_DIM // 4)

    @pl.when(qi == 0)
    def _():
        kn = _rms(k_ref[0], kg_ref[...])
        kn_ref[...] = _rope(kn, ck_ref[...], sk_ref[...], first_mask).astype(BF16)
        vb_ref[:, :LANES] = v_ref[0].astype(BF16)
        vb_ref[:, LANES:] = jnp.ones((k_ref.shape[1], LANES), BF16)

    scale = GQ_HEAD_DIM ** -0.5 * LOG2E
    tq = q_ref.shape[1]
    ts = min(ATTN_STRIP, tq)
    nr = tq // ts

    def where(i):
        r0 = (i % nr) * ts
        return slice(r0, r0 + ts), slice((i // nr) * GQ_HEAD_DIM, (i // nr + 1) * GQ_HEAD_DIM)

    def scores(i):
        rs, cs = where(i)
        qn = _rope(_rms(q_ref[0, rs, cs], qg_ref[...]), cq_ref[rs, :], sq_ref[rs, :], first_mask) * scale
        return _dot_nt(qn.astype(BF16), kn_ref[...])

    def exps(i, s):
        return jnp.exp2(s - jnp.max(s, axis=-1, keepdims=True)).astype(BF16)

    def values(i, e):
        rs, cs = where(i)
        res = _dot(e, vb_ref[...])
        o_ref[0, rs, cs] = (res[:, :LANES] * (1.0 / res[:, LANES:LANES + 1])).astype(o_ref.dtype)

    _pipeline3(GQ_GROUP * nr, scores, exps, values)


def gqa_attention(proj, cos, sin, q_g, k_g, *, q_blk0, k_blk0, v_blk0, tq):
    B, L, _ = proj.shape
    tq = min(tq, L)
    gw = GQ_GROUP * GQ_HEAD_DIM
    return pl.pallas_call(
        _gqa_kernel,
        grid=(B, GQ_KV_HEADS, L // tq),
        in_specs=[pl.BlockSpec((1, tq, gw), lambda b, g, i: (b, i, q_blk0 + g)),
                  pl.BlockSpec((1, L, LANES), lambda b, g, i: (b, 0, k_blk0 + g)),
                  pl.BlockSpec((1, L, LANES), lambda b, g, i: (b, 0, v_blk0 + g)),
                  pl.BlockSpec((tq, LANES), lambda b, g, i: (i, 0)),
                  pl.BlockSpec((tq, LANES), lambda b, g, i: (i, 0)),
                  pl.BlockSpec((L, LANES), lambda b, g, i: (0, 0)),
                  pl.BlockSpec((L, LANES), lambda b, g, i: (0, 0)),
                  pl.BlockSpec((1, LANES), lambda b, g, i: (0, 0)),
                  pl.BlockSpec((1, LANES), lambda b, g, i: (0, 0))],
        out_specs=pl.BlockSpec((1, tq, gw), lambda b, g, i: (b, i, g)),
        out_shape=jax.ShapeDtypeStruct((B, L, GQ_WIDTH), BF16),
        scratch_shapes=[pltpu.VMEM((L, LANES), BF16), pltpu.VMEM((L, 2 * LANES), BF16)],
        compiler_params=_params("parallel", "parallel", "arbitrary"),
        name="gqa_attention",
    )(proj, proj, proj, cos, sin, cos, sin, q_g.reshape(1, LANES), k_g.reshape(1, LANES))


def _hgrn2_pair_masks():
    pos = lax.broadcasted_iota(jnp.int32, (1, HG_BLOCK, LANES), 1)
    return {rev: [jnp.where((pos <= s) if rev else (pos >= s), 0.0, -jnp.inf).astype(F32)
                  for s in range(HG_BLOCK)] for rev in (False, True)}


def _hgrn2_intra_streams(streams, b_scr, pair_mask):
    C = HG_CHUNK
    nb = C // HG_BLOCK
    ns = range(len(streams))
    rev = [st[5] for st in streams]
    r_i = lax.broadcasted_iota(jnp.int32, (C, C), 0)
    c_i = lax.broadcasted_iota(jnp.int32, (C, C), 1)

    kk, lg = [], []
    for q, fz, v, vb, lb, _ in streams:
        f = lb + (1.0 - lb) * jax.nn.sigmoid(fz)
        kk.append(1.0 - f)
        lg.append(jnp.log(f) * LOG2E)

    tri_f = jnp.where(r_i >= c_i, 1.0, 0.0).astype(BF16)
    tri_b = jnp.where(r_i <= c_i, 1.0, 0.0).astype(BF16)
    b = []
    for i in ns:
        hi = lg[i].astype(BF16)
        r1 = lg[i] - hi.astype(F32)
        mid = r1.astype(BF16)
        lo = (r1 - mid.astype(F32)).astype(BF16)
        tri = tri_b if rev[i] else tri_f
        b.append(_dot(tri, hi) + _dot(tri, mid) + _dot(tri, lo))
    b_end = [b[i][0:1, :] if rev[i] else b[i][C - 1:C, :] for i in ns]
    qe = [(streams[i][0] * jnp.exp2(b[i])).astype(BF16) for i in ns]
    u = [_dot_tn(streams[i][3], (kk[i] * jnp.exp2(b_end[i] - b[i])).astype(BF16)) for i in ns]

    b3 = [b[i].reshape(nb, HG_BLOCK, LANES) for i in ns]
    q3 = [streams[i][0].reshape(nb, HG_BLOCK, LANES) for i in ns]
    seg_pos = lax.broadcasted_iota(jnp.int32, (C, HG_BLOCK * LANES), 1) // LANES
    row_pos = lax.broadcasted_iota(jnp.int32, (C, HG_BLOCK * LANES), 0) % HG_BLOCK
    kcat = []
    for i in ns:
        kb = kk[i].astype(BF16)
        kcat.append(jnp.where(seg_pos == row_pos, jnp.concatenate([kb] * HG_BLOCK, axis=1),
                              jnp.zeros((), BF16)))
    for i in ns:
        b_scr[i] = b[i]
    ps = [[] for i in ns]
    for s in range(HG_BLOCK):
        for i in ns:
            key_b = jnp.concatenate(
                [jnp.broadcast_to(b_scr[i, blk * HG_BLOCK + s:blk * HG_BLOCK + s + 1, :], (HG_BLOCK, LANES))
                 for blk in range(nb)], axis=0).reshape(nb, HG_BLOCK, LANES)
            d = (b3[i] - key_b) + pair_mask[rev[i]][s]
            ps[i].append((q3[i] * jnp.exp2(d)).reshape(C, LANES).astype(BF16))
    same_blk = (r_i // HG_BLOCK) == (c_i // HG_BLOCK)
    adiag = [_dot_nt(jnp.concatenate(ps[i], axis=1), kcat[i]) for i in ns]

    zrow = jnp.zeros((1, 1, LANES), F32)
    qt = []
    for i in ns:
        if rev[i]:
            rblk3 = jnp.concatenate([b3[i][1:, 0:1, :], zrow], axis=0)
        else:
            rblk3 = jnp.concatenate([zrow, b3[i][:nb - 1, HG_BLOCK - 1:HG_BLOCK, :]], axis=0)
        qt.append((q3[i] * jnp.exp2(b3[i] - rblk3)).reshape(C, LANES))
    rows = [[] for i in ns]
    for blk in range(nb):
        for i in ns:
            if rev[i]:
                edge = (blk + 1) * HG_BLOCK
                if edge == C:
                    rows[i].append(jnp.zeros((HG_BLOCK, C), F32))
                    continue
                kt = kk[i][edge:, :] * jnp.exp2(b[i][edge:edge + 1, :] - b[i][edge:, :])
                kfull = jnp.concatenate([jnp.zeros((edge, LANES), F32), kt], axis=0)
            else:
                edge = blk * HG_BLOCK
                if edge == 0:
                    rows[i].append(jnp.zeros((HG_BLOCK, C), F32))
                    continue
                kt = kk[i][:edge, :] * jnp.exp2(b[i][edge - 1:edge, :] - b[i][:edge, :])
                kfull = jnp.concatenate([kt, jnp.zeros((C - edge, LANES), F32)], axis=0)
            rows[i].append(_dot_nt(qt[i][blk * HG_BLOCK:(blk + 1) * HG_BLOCK, :].astype(BF16),
                                   kfull.astype(BF16)))
    outs = []
    for i in ns:
        att = jnp.where(same_blk, adiag[i], jnp.concatenate(rows[i], axis=0))
        o = _dot(att.astype(BF16), streams[i][3])
        outs.append((o, qe[i], u[i], jnp.exp2(b_end[i])))
    return outs


def _hgrn2_kernel(q_ref, ff_ref, fb_ref, i_ref, g_ref, lbf_ref, lbb_ref, ng_ref, o_ref, of_ref, ob_ref,
                  sf_ref, sb_ref, b_scr):
    L = q_ref.shape[1]
    C = HG_CHUNK
    n = L // C
    gsize = math.gcd(HG_INTRA_CHUNKS, n)
    qscale = HG_DK ** -0.5

    def rows(c):
        return pl.ds(pl.multiple_of(c * C, C), C)

    sf_ref[...] = jnp.zeros_like(sf_ref)
    sb_ref[...] = jnp.zeros_like(sb_ref)
    pair_mask = _hgrn2_pair_masks()

    def group(it, carry):
        streams = []
        dests = []
        for k in range(gsize):
            c = it * gsize + k
            for fz_ref, lb_ref, o_s, s_s, cc, reverse in ((ff_ref, lbf_ref, of_ref, sf_ref, c, False),
                                                          (fb_ref, lbb_ref, ob_ref, sb_ref, n - 1 - c, True)):
                r = rows(cc)
                v = i_ref[0, r, :]
                streams.append((q_ref[0, r, :] * qscale, fz_ref[0, r, :], v, v.astype(BF16), lb_ref[0], reverse))
                dests.append((o_s, s_s, r))
        state = {id(sf_ref): sf_ref[...], id(sb_ref): sb_ref[...]}
        for (o, qe, u, e), (o_s, s_s, r) in zip(_hgrn2_intra_streams(streams, b_scr, pair_mask), dests):
            st = state[id(s_s)]
            o_s[r, :] = o + _dot_nt(qe, st.astype(BF16))
            state[id(s_s)] = st * e + u
        sf_ref[...] = state[id(sf_ref)]
        sb_ref[...] = state[id(sb_ref)]
        return carry

    lax.fori_loop(0, n // gsize, group, 0)
    o = _rms(of_ref[...] + ob_ref[...], ng_ref[...])
    g = g_ref[0]
    o_ref[0] = (o * (g * jax.nn.sigmoid(g))).astype(o_ref.dtype)


def hgrn2_mixer(proj, lb_f, lb_b, norm_g):
    B, L, _ = proj.shape
    H = HG_HEADS

    def col(k):
        return pl.BlockSpec((1, L, LANES), lambda b, h: (b, 0, k * H + h))

    lbspec = pl.BlockSpec((1, 1, LANES), lambda b, h: (h, 0, 0))
    return pl.pallas_call(
        _hgrn2_kernel,
        grid=(B, H),
        in_specs=[col(0), col(1), col(2), col(3), col(4), lbspec, lbspec,
                  pl.BlockSpec((1, LANES), lambda b, h: (0, 0))],
        out_specs=pl.BlockSpec((1, L, LANES), lambda b, h: (b, 0, h)),
        out_shape=jax.ShapeDtypeStruct((B, L, HG_WIDTH), BF16),
        scratch_shapes=[pltpu.VMEM((L, LANES), F32), pltpu.VMEM((L, LANES), F32),
                        pltpu.VMEM((LANES, LANES), F32), pltpu.VMEM((LANES, LANES), F32),
                        pltpu.VMEM((2 * math.gcd(HG_INTRA_CHUNKS, L // HG_CHUNK), HG_CHUNK, LANES), F32)],
        compiler_params=_params("parallel", "arbitrary"),
        name="hgrn2",
    )(proj, proj, proj, proj, proj, lb_f.reshape(H, 1, LANES), lb_b.reshape(H, 1, LANES),
      norm_g.reshape(1, LANES))


def _hyena_filter_kernel(z_ref, t_ref, w1_ref, b1_ref, w2_ref, b2_ref, w3_ref, b3_ref, w4f_ref, w4b_ref,
                         fr_ref, ad_ref, gs_ref, gd_ref, kn_ref):
    def hdot(a, b):
        return jnp.dot(a, b, precision=HIGHEST, preferred_element_type=F32)

    fr = fr_ref[...]
    h = jnp.sin(fr * (hdot(z_ref[...], w1_ref[...]) + b1_ref[...]))
    h = jnp.sin(fr * (hdot(h, w2_ref[...]) + b2_ref[...]))
    h = jnp.sin(fr * (hdot(h, w3_ref[...]) + b3_ref[...]))
    window = jnp.exp(-t_ref[...] * ad_ref[...])
    hf = hdot(h, w4f_ref[...]) * window
    hb = hdot(h, w4b_ref[...]) * window
    row = lax.broadcasted_iota(jnp.int32, hb.shape, 0)
    hb = jnp.where(row == 0, 0.0, hb)
    gs_ref[...] = hf + hb
    gd_ref[...] = hb - hf
    sgn = jnp.where(row % 2 == 0, 1.0, -1.0)
    kn_ref[...] = jnp.sum((hf + hb) * sgn, axis=0, keepdims=True) * (0.5 / hb.shape[0])


def _split2(x):
    hi = x.astype(BF16)
    return hi, (x - hi.astype(F32)).astype(BF16)


def _hyena_spectrum_kernel(ah_ref, al_ref, bh_ref, bl_ref, gs_ref, gd_ref, kre_ref, kim_ref, sh_ref, sl_ref,
                           dh_ref, dl_ref, *, n_fft):
    @pl.when(pl.program_id(1) == 0)
    def _():
        sh_ref[...], sl_ref[...] = _split2(gs_ref[...])
        dh_ref[...], dl_ref[...] = _split2(gd_ref[...])

    tf = ah_ref.shape[0]
    f0 = pl.program_id(1) * tf
    row = lax.broadcasted_iota(jnp.int32, (tf, 1), 0) + f0
    wf = jnp.where(row == 0, 1.0 / n_fft, 2.0 / n_fft)
    ah = ah_ref[...]
    bh = bh_ref[...]
    kre_ref[...] = wf * (_dot(ah, sh_ref[...]) + _dot(ah, sl_ref[...]) + _dot(al_ref[...], sh_ref[...]))
    kim_ref[...] = wf * (_dot(bh, dh_ref[...]) + _dot(bh, dl_ref[...]) + _dot(bl_ref[...], dh_ref[...]))


def hyena_filter_spectrum(L, cos_hi, cos_lo, sin_hi, sin_lo, w1, b1, w2, b2, w3, b3, w4, sin_freq):
    C = HY_WIDTH
    t = jnp.linspace(0.0, 1.0, L, dtype=F32)[:, None]
    w = (2.0 * math.pi / L) * jnp.arange(L, dtype=F32)[:, None]
    f = jnp.linspace(1e-4, HY_BANDS - 1, HY_BANDS, dtype=F32)[None, :]
    z = jnp.concatenate([t, jnp.cos(f * w), -jnp.sin(f * w)], axis=-1)
    zp = jnp.pad(z, ((0, 0), (0, LANES - HY_EMB)))
    w1p = jnp.pad(w1, ((0, LANES - HY_EMB), (0, 0)))
    max_decay = math.log(HY_TARGET) / HY_FAST_PCT
    min_decay = math.log(HY_TARGET) / HY_SLOW_PCT
    absd = jnp.abs(jnp.linspace(min_decay, max_decay, C, dtype=F32))[None, :]
    tc = min(512, C)
    full = lambda shape: pl.BlockSpec(shape, lambda j: (0, 0))
    gs, gd, knyq = pl.pallas_call(
        _hyena_filter_kernel,
        grid=(C // tc,),
        in_specs=[full((L, LANES)), full((L, 1)), full((LANES, HY_FILT)), full((1, HY_FILT)),
                  full((HY_FILT, HY_FILT)), full((1, HY_FILT)), full((HY_FILT, HY_FILT)), full((1, HY_FILT)),
                  pl.BlockSpec((HY_FILT, tc), lambda j: (0, j)),
                  pl.BlockSpec((HY_FILT, tc), lambda j: (0, C // tc + j)),
                  full((1, HY_FILT)),
                  pl.BlockSpec((1, tc), lambda j: (0, j))],
        out_specs=[pl.BlockSpec((L, tc), lambda j: (0, j)), pl.BlockSpec((L, tc), lambda j: (0, j)),
                   pl.BlockSpec((1, tc), lambda j: (0, j))],
        out_shape=[jax.ShapeDtypeStruct((L, C), F32), jax.ShapeDtypeStruct((L, C), F32),
                   jax.ShapeDtypeStruct((1, C), F32)],
        compiler_params=_params("arbitrary"),
        name="hyena_filter",
    )(zp, t, w1p, b1.reshape(1, -1), w2, b2.reshape(1, -1), w3, b3.reshape(1, -1), w4, w4,
      sin_freq.reshape(1, -1), absd)

    tf = min(256, L)
    tbl = pl.BlockSpec((tf, L), lambda j, i: (i, 0))
    kre, kim = pl.pallas_call(
        functools.partial(_hyena_spectrum_kernel, n_fft=2 * L),
        grid=(C // tc, L // tf),
        in_specs=[tbl, tbl, tbl, tbl,
                  pl.BlockSpec((L, tc), lambda j, i: (0, j)),
                  pl.BlockSpec((L, tc), lambda j, i: (0, j))],
        out_specs=[pl.BlockSpec((tf, tc), lambda j, i: (i, j)), pl.BlockSpec((tf, tc), lambda j, i: (i, j))],
        out_shape=[jax.ShapeDtypeStruct((L, C), F32), jax.ShapeDtypeStruct((L, C), F32)],
        scratch_shapes=[pltpu.VMEM((L, tc), BF16)] * 4,
        compiler_params=_params("parallel", "arbitrary"),
        name="hyena_spectrum",
    )(cos_hi, cos_lo, sin_hi, sin_lo, gs, gd)
    return kre, kim, knyq


def _short_conv(u, w_ref, b_ref):
    L = u.shape[0]
    row = lax.broadcasted_iota(jnp.int32, (L, 1), 0)
    prev = jnp.where(row == 0, 0.0, pltpu.roll(u, 1, axis=0))
    nxt = jnp.where(row == L - 1, 0.0, pltpu.roll(u, L - 1, axis=0))
    return prev * w_ref[0:1, :] + u * w_ref[1:2, :] + nxt * w_ref[2:3, :] + b_ref[...]


def _hyena_conv_kernel(x1_ref, x2_ref, v_ref, cw1_ref, cw2_ref, cw3_ref, cb1_ref, cb2_ref, cb3_ref,
                       a_ref, b_ref, kre_ref, kim_ref, knyq_ref, bias_ref, o_ref, r1_ref, r2_ref, *, tf):
    L = v_ref.shape[1]
    x2 = _short_conv(x2_ref[0], cw2_ref, cb2_ref)
    vv = _short_conv(v_ref[0], cw3_ref, cb3_ref)
    vx = vv * x2
    vxb = vx.astype(BF16)
    sgn = jnp.where(lax.broadcasted_iota(jnp.int32, (L, 1), 0) % 2 == 0, 1.0, -1.0)
    def forward(fb):
        fs = slice(fb * tf, (fb + 1) * tf)
        return _dot(a_ref[fs, :], vxb), _dot(b_ref[fs, :], vxb)

    def product(fb, pq):
        p, q = pq
        fs = slice(fb * tf, (fb + 1) * tf)
        kre = kre_ref[fs, :]
        kim = kim_ref[fs, :]
        r1_ref[fs, :] = (p * kre + q * kim).astype(BF16)
        r2_ref[fs, :] = (q * kre - p * kim).astype(BF16)

    _pipeline3(L // tf, forward, product, lambda fb, _: None)

    v_nyq = jnp.sum(vx * sgn, axis=0, keepdims=True)
    y0 = vx * bias_ref[...] + sgn * (v_nyq * knyq_ref[...])
    x1 = _short_conv(x1_ref[0], cw1_ref, cb1_ref)
    y = _dot(a_ref[...], r1_ref[...]) + _dot(b_ref[...], r2_ref[...])
    o_ref[0] = ((y0 + y) * x1).astype(o_ref.dtype)


def hyena_conv(proj, conv_w, conv_b, cos_b, sin_b, kre, kim, knyq, bias, *, tc):
    B, L, _ = proj.shape
    C = HY_WIDTH
    tc = min(tc, C)
    nc = C // tc
    tf = min(1024, L)

    def grp(k):
        return pl.BlockSpec((1, L, tc), lambda c, b: (b, 0, k * nc + c))

    def cw(k):
        return pl.BlockSpec((HY_SHORT, tc), lambda c, b: (0, k * nc + c))

    def cb(k):
        return pl.BlockSpec((1, tc), lambda c, b: (0, k * nc + c))

    tbl = pl.BlockSpec((L, L), lambda c, b: (0, 0), pipeline_mode=pl.Buffered(1))
    chan = pl.BlockSpec((L, tc), lambda c, b: (0, c), pipeline_mode=pl.Buffered(1))
    vec = pl.BlockSpec((1, tc), lambda c, b: (0, c))
    return pl.pallas_call(
        functools.partial(_hyena_conv_kernel, tf=tf),
        grid=(nc, B),
        in_specs=[grp(0), grp(1), grp(2), cw(0), cw(1), cw(2), cb(0), cb(1), cb(2),
                  tbl, tbl, chan, chan, vec, vec],
        out_specs=pl.BlockSpec((1, L, tc), lambda c, b: (b, 0, c)),
        out_shape=jax.ShapeDtypeStruct((B, L, C), BF16),
        scratch_shapes=[pltpu.VMEM((L, tc), BF16), pltpu.VMEM((L, tc), BF16)],
        compiler_params=_params("parallel", "arbitrary"),
        name="hyena_conv",
    )(proj, proj, proj, conv_w, conv_w, conv_w, conv_b.reshape(1, -1), conv_b.reshape(1, -1),
      conv_b.reshape(1, -1), cos_b, sin_b, kre, kim, knyq, bias.reshape(1, -1))


def _dft_tables(L):
    n = 2 * L
    r = math.gcd(DFT_ROW_SPLIT, L)
    j = jnp.arange(L, dtype=jnp.int32)[None, :]

    def cs(f):
        ang = ((f[:, None] * j) % n).astype(F32) * (2.0 * math.pi / n)
        return jnp.cos(ang), jnp.sin(ang)

    c0, s0 = cs(jnp.arange(r, dtype=jnp.int32))
    c1, s1 = cs(jnp.arange(L // r, dtype=jnp.int32) * r)
    cos_m = (c1[:, None, :] * c0[None, :, :] - s1[:, None, :] * s0[None, :, :]).reshape(L, L)
    sin_m = (s1[:, None, :] * c0[None, :, :] + c1[:, None, :] * s0[None, :, :]).reshape(L, L)
    cos_hi = cos_m.astype(BF16)
    sin_hi = sin_m.astype(BF16)
    return (cos_hi, (cos_m - cos_hi.astype(F32)).astype(BF16),
            sin_hi, (sin_m - sin_hi.astype(F32)).astype(BF16))


def _axial_rope_tables(L):
    rows = L // GRID_W
    r, c = jnp.meshgrid(jnp.arange(rows, dtype=F32), jnp.arange(GRID_W, dtype=F32), indexing='ij')
    r = r.reshape(-1)
    c = c.reshape(-1)
    half = GQ_HEAD_DIM // 2
    inv = ROPE_THETA ** (-jnp.arange(0, half, 2, dtype=F32) / half)
    ang_r = r[:, None] * inv[None, :]
    ang_c = c[:, None] * inv[None, :]
    ang = jnp.concatenate([ang_r, ang_r, ang_c, ang_c], axis=-1)
    return jnp.cos(ang), jnp.sin(ang)


def _even_mixer(x, layer, norm_g, w_in, conv_w, conv_b, fw1, fb1, fw2, fb2, fw3, fb3, fw4, sin_freq,
                hy_bias, q_g, k_g, lam_q1, lam_k1, lam_q2, lam_k2, subln_g):
    B, L, D = x.shape
    x2 = x.reshape(B * L, D)
    proj = norm_matmul(x2, norm_g, w_in.astype(BF16), tm=1024, tn=1536).reshape(B, L, -1)
    cos_hi, cos_lo, sin_hi, sin_lo = _dft_tables(L)
    kre, kim, knyq = hyena_filter_spectrum(L, cos_hi, cos_lo, sin_hi, sin_lo, fw1, fb1, fw2, fb2, fw3, fb3, fw4,
                                           sin_freq)
    o_a = hyena_conv(proj, conv_w, conv_b, cos_hi, sin_hi, kre, kim, knyq, hy_bias,
                     tc=256)
    lam_init = 0.8 - 0.6 * math.exp(-0.3 * layer)
    lam = (jnp.exp(jnp.sum(lam_q1.astype(F32) * lam_k1.astype(F32)))
           - jnp.exp(jnp.sum(lam_q2.astype(F32) * lam_k2.astype(F32))) + lam_init)
    slopes = jnp.asarray(np.array([2.0 ** (-8.0 * (h + 1) / DF_HEADS) for h in range(DF_HEADS)],
                                  dtype=np.float32))
    scal = jnp.concatenate([lam.reshape(1), slopes]).astype(F32)
    nb_hy = 3 * HY_WIDTH // LANES
    nb_qk = DF_QK_WIDTH // LANES
    o_b = diff_attention(proj, scal, q_g, k_g, subln_g, q_blk0=nb_hy, k_blk0=nb_hy + nb_qk,
                         v_blk0=nb_hy + 2 * nb_qk, out_scale=1.0 - lam_init, tq=2048)
    return o_a, o_b


def _odd_mixer(x, norm_g, lb_f, lb_b, w_in, hg_norm_g, q_g, k_g):
    B, L, D = x.shape
    x2 = x.reshape(B * L, D)
    proj = norm_matmul(x2, norm_g, w_in.astype(BF16), tm=1024, tn=1664).reshape(B, L, -1)
    o_c = hgrn2_mixer(proj, lb_f, lb_b, hg_norm_g)
    cos, sin = _axial_rope_tables(L)
    gw = GQ_GROUP * GQ_HEAD_DIM
    q0 = 5 * HG_WIDTH
    k0 = q0 + GQ_WIDTH
    v0 = k0 + GQ_KV_HEADS * GQ_HEAD_DIM
    o_d = gqa_attention(proj, cos, sin, q_g, k_g, q_blk0=q0 // gw, k_blk0=k0 // LANES, v_blk0=v0 // LANES,
                        tq=1024)
    return o_c, o_d


def kernel(x, mem, norm_mix_g, norm_mem_q_g, norm_mem_kv_g, norm_ffn_g, ev_w_in, ev_w_out, hy_conv_w, hy_conv_b, hy_fw1, hy_fb1, hy_fw2, hy_fb2, hy_fw3, hy_fb3, hy_fw4, hy_sin_freq, hy_bias, df_q_g, df_k_g, df_lam_q1, df_lam_k1, df_lam_q2, df_lam_k2, df_subln_g, od_w_in, od_w_out, hg_lb_logits, hg_norm_g, gq_q_g, gq_k_g, ca_w_q, ca_w_kv, ca_w_out, ca_q_g, ca_k_g, mlp_w1, mlp_w2):
    B, L, D = x.shape
    M = mem.shape[1]
    depth = norm_mix_g.shape[0]
    lb_soft = jax.nn.softmax(hg_lb_logits.astype(F32), axis=1)
    lbs = jnp.cumsum(lb_soft, axis=1)
    lbs = lbs - lbs[:, :1]
    mem2 = mem.reshape(B * M, D)
    for layer in range(depth):
        j = layer // 2
        if layer % 2 == 0:
            a1, a2 = _even_mixer(x, layer, norm_mix_g[layer], ev_w_in[j], hy_conv_w[j], hy_conv_b[j],
                                 hy_fw1[j], hy_fb1[j], hy_fw2[j], hy_fb2[j], hy_fw3[j], hy_fb3[j], hy_fw4[j],
                                 hy_sin_freq[j], hy_bias[j], df_q_g[j], df_k_g[j], df_lam_q1[j], df_lam_k1[j],
                                 df_lam_q2[j], df_lam_k2[j], df_subln_g[j])
            w_mix = ev_w_out[j]
        else:
            a1, a2 = _odd_mixer(x, norm_mix_g[layer], lbs[0, layer], lbs[1, layer], od_w_in[j],
                                hg_norm_g[j], gq_q_g[j], gq_k_g[j])
            w_mix = od_w_out[j]
        kv = norm_matmul(mem2, norm_mem_kv_g[layer], ca_w_kv[layer].astype(BF16), tm=1024, tn=1024)
        x = mix_proj_cross_attention(a1, a2, w_mix.astype(BF16), x, kv.reshape(B, M, -1), norm_mem_q_g[layer],
                                     ca_w_q[layer].astype(BF16), ca_q_g[layer], ca_k_g[layer],
                                     ca_w_out[layer].astype(BF16), tq=512)
        x = mlp_residual(x.reshape(B * L, D), norm_ffn_g[layer], layer_weight_bf16(mlp_w1, layer),
                         layer_weight_bf16(mlp_w2, layer), tm=512, th=1024).reshape(B, L, D)
    return x
```

```python
import functools
import math

import numpy as np
import jax
import jax.numpy as jnp
from jax import lax
from jax.experimental import pallas as pl
from jax.experimental.pallas import tpu as pltpu

F32 = jnp.float32
BF16 = jnp.bfloat16
HIGHEST = lax.Precision.HIGHEST

D_MODEL = 2048
NORM_EPS = 1e-6
GRID_W = 64
HY_WIDTH = D_MODEL // 2
HY_EMB = 33
HY_BANDS = (HY_EMB - 1) // 2
HY_FILT = 64
HY_SHORT = 3
HY_TARGET = 1e-2
HY_FAST_PCT = 0.3
HY_SLOW_PCT = 1.5
DF_HEADS = 8
DF_HEAD_DIM = 64
DF_V_DIM = 128
DF_QK_WIDTH = DF_HEADS * 2 * DF_HEAD_DIM
HG_WIDTH = D_MODEL // 2
HG_HEADS = 8
HG_DK = HG_WIDTH // HG_HEADS
GQ_WIDTH = D_MODEL - HG_WIDTH
GQ_HEADS = 8
GQ_KV_HEADS = 2
GQ_HEAD_DIM = GQ_WIDTH // GQ_HEADS
GQ_GROUP = GQ_HEADS // GQ_KV_HEADS
ROPE_THETA = 10000.0
CA_HEADS = 4
CA_HEAD_DIM = 128
CA_WIDTH = CA_HEADS * CA_HEAD_DIM

LANES = 128
SUBLANES = 8
VMEM_LIMIT_BYTES = 56 * 1024 * 1024

HG_CHUNK = 64
HG_BLOCK = SUBLANES
ATTN_STRIP = 128
HG_INTRA_CHUNKS = 32
DFT_ROW_SPLIT = 64
LOG2E = 1.4426950408889634


def _params(*sem):
    return pltpu.CompilerParams(dimension_semantics=sem, vmem_limit_bytes=VMEM_LIMIT_BYTES)


def _rms(x, g):
    return x * lax.rsqrt(jnp.mean(x * x, axis=-1, keepdims=True) + NORM_EPS) * g


def _dot(a, b):
    return jnp.dot(a, b, preferred_element_type=F32)


def _dot_nt(a, b):
    return lax.dot_general(a, b, (((1,), (1,)), ((), ())), preferred_element_type=F32)


def _dot_tn(a, b):
    return lax.dot_general(a, b, (((0,), (0,)), ((), ())), preferred_element_type=F32)


def _pipeline3(n, stage_a, stage_b, stage_c):
    a = {}
    b = {}
    for t in range(n + 2):
        if t < n:
            a[t] = stage_a(t)
        if 0 <= t - 1 < n:
            b[t - 1] = stage_b(t - 1, a.pop(t - 1))
        if 0 <= t - 2 < n:
            stage_c(t - 2, b.pop(t - 2))


def _cast_kernel(w_ref, o_ref):
    o_ref[...] = w_ref[0].astype(o_ref.dtype)


def layer_weight_bf16(w, layer, *, block_bytes=8 * 1024 * 1024):
    _, K, N = w.shape
    tr = min(K, block_bytes // (4 * N))
    return pl.pallas_call(
        _cast_kernel,
        grid=(K // tr,),
        in_specs=[pl.BlockSpec((1, tr, N), lambda i: (layer, i, 0))],
        out_specs=pl.BlockSpec((tr, N), lambda i: (i, 0)),
        out_shape=jax.ShapeDtypeStruct((K, N), BF16),
        compiler_params=_params("parallel"),
        name="weight_bf16",
    )(w)


def _norm_matmul_kernel(x_ref, g_ref, w_ref, o_ref, hn_ref):
    @pl.when(pl.program_id(1) == 0)
    def _():
        hn_ref[...] = _rms(x_ref[...], g_ref[...]).astype(BF16)

    o_ref[...] = _dot(hn_ref[...], w_ref[...]).astype(o_ref.dtype)


def norm_matmul(x, g, w, *, tm, tn, out_dtype=F32):
    M, D = x.shape
    N = w.shape[1]
    tm = min(tm, M)
    tn = min(tn, N)
    return pl.pallas_call(
        _norm_matmul_kernel,
        grid=(M // tm, N // tn),
        in_specs=[pl.BlockSpec((tm, D), lambda i, j: (i, 0)),
                  pl.BlockSpec((1, D), lambda i, j: (0, 0)),
                  pl.BlockSpec((D, tn), lambda i, j: (0, j))],
        out_specs=pl.BlockSpec((tm, tn), lambda i, j: (i, j)),
        out_shape=jax.ShapeDtypeStruct((M, N), out_dtype),
        scratch_shapes=[pltpu.VMEM((tm, D), BF16)],
        compiler_params=_params("parallel", "arbitrary"),
        name="norm_matmul",
    )(x, g.reshape(1, D), w)


def _mlp_kernel(x_ref, g_ref, w1_ref, w2_ref, o_ref, hn_ref, acc_ref):
    j = pl.program_id(1)

    last = pl.num_programs(1) - 1

    @pl.when(j == 0)
    def _():
        x = x_ref[...]
        hn_ref[...] = _rms(x, g_ref[...]).astype(BF16)
        acc_ref[...] = x

    def accumulated():
        h1 = jnp.maximum(_dot(hn_ref[...], w1_ref[...]), 0.0)
        return acc_ref[...] + _dot((h1 * h1).astype(BF16), w2_ref[...])

    @pl.when(j != last)
    def _():
        acc_ref[...] = accumulated()

    @pl.when(j == last)
    def _():
        o_ref[...] = accumulated()


def mlp_residual(x, g, w1, w2, *, tm, th):
    M, D = x.shape
    H = w1.shape[1]
    tm = min(tm, M)
    th = min(th, H)
    return pl.pallas_call(
        _mlp_kernel,
        grid=(M // tm, H // th),
        in_specs=[pl.BlockSpec((tm, D), lambda i, j: (i, 0)),
                  pl.BlockSpec((1, D), lambda i, j: (0, 0)),
                  pl.BlockSpec((D, th), lambda i, j: (0, j)),
                  pl.BlockSpec((th, D), lambda i, j: (j, 0))],
        out_specs=pl.BlockSpec((tm, D), lambda i, j: (i, 0)),
        out_shape=jax.ShapeDtypeStruct((M, D), F32),
        scratch_shapes=[pltpu.VMEM((tm, D), BF16), pltpu.VMEM((tm, D), F32)],
        compiler_params=_params("parallel", "arbitrary"),
        name="mlp_residual",
    )(x, g.reshape(1, D), w1, w2)


def _cross_attn_kernel(a1_ref, a2_ref, w1_ref, w2_ref, x_ref, g_ref, wq_ref, kv_ref, qg_ref, kg_ref, wo_ref,
                       o_ref):
    x = x_ref[0] + _dot(a1_ref[0], w1_ref[...]) + _dot(a2_ref[0], w2_ref[...])
    hn = _rms(x, g_ref[...]).astype(BF16)
    q = _dot(hn, wq_ref[...])
    kv = kv_ref[0]
    scale = CA_HEAD_DIM ** -0.5 * LOG2E
    ones = jnp.ones((kv.shape[0], CA_HEAD_DIM), BF16)
    outs = []
    for h in range(CA_HEADS):
        sl = slice(h * CA_HEAD_DIM, (h + 1) * CA_HEAD_DIM)
        qh = (_rms(q[:, sl], qg_ref[...]) * scale).astype(BF16)
        kh = _rms(kv[:, sl], kg_ref[...]).astype(BF16)
        vh = kv[:, CA_WIDTH + h * CA_HEAD_DIM:CA_WIDTH + (h + 1) * CA_HEAD_DIM].astype(BF16)
        s = _dot_nt(qh, kh)
        e = jnp.exp2(s - jnp.max(s, axis=-1, keepdims=True)).astype(BF16)
        res = _dot(e, jnp.concatenate([vh, ones], axis=1))
        outs.append(res[:, :CA_HEAD_DIM] * (1.0 / res[:, CA_HEAD_DIM:CA_HEAD_DIM + 1]))
    o = jnp.concatenate(outs, axis=-1).astype(BF16)
    o_ref[0] = x + _dot(o, wo_ref[...])


def mix_proj_cross_attention(a1, a2, w_mix, x, kv, g, wq, qg, kg, wo, *, tq):
    B, L, D = x.shape
    M = kv.shape[1]
    K1 = a1.shape[2]
    assert a2.shape[2] == K1 and w_mix.shape[0] == 2 * K1
    tq = min(tq, L)
    return pl.pallas_call(
        _cross_attn_kernel,
        grid=(B, L // tq),
        in_specs=[pl.BlockSpec((1, tq, K1), lambda b, i: (b, i, 0)),
                  pl.BlockSpec((1, tq, K1), lambda b, i: (b, i, 0)),
                  pl.BlockSpec((K1, D), lambda b, i: (0, 0), pipeline_mode=pl.Buffered(1)),
                  pl.BlockSpec((K1, D), lambda b, i: (1, 0), pipeline_mode=pl.Buffered(1)),
                  pl.BlockSpec((1, tq, D), lambda b, i: (b, i, 0)),
                  pl.BlockSpec((1, D), lambda b, i: (0, 0)),
                  pl.BlockSpec((D, CA_WIDTH), lambda b, i: (0, 0)),
                  pl.BlockSpec((1, M, 2 * CA_WIDTH), lambda b, i: (b, 0, 0)),
                  pl.BlockSpec((1, CA_HEAD_DIM), lambda b, i: (0, 0)),
                  pl.BlockSpec((1, CA_HEAD_DIM), lambda b, i: (0, 0)),
                  pl.BlockSpec((CA_WIDTH, D), lambda b, i: (0, 0))],
        out_specs=pl.BlockSpec((1, tq, D), lambda b, i: (b, i, 0)),
        out_shape=jax.ShapeDtypeStruct((B, L, D), F32),
        compiler_params=_params("parallel", "arbitrary"),
        name="cross_attention",
    )(a1, a2, w_mix, w_mix, x, g.reshape(1, D), wq, kv, qg.reshape(1, -1), kg.reshape(1, -1), wo)


def _half_rms(x, g2, lo_mask):
    sq = x * x
    ms_lo = jnp.sum(jnp.where(lo_mask, sq, 0.0), axis=-1, keepdims=True)
    ms_hi = jnp.sum(jnp.where(lo_mask, 0.0, sq), axis=-1, keepdims=True)
    inv = jnp.where(lo_mask, lax.rsqrt(ms_lo * (1.0 / DF_HEAD_DIM) + NORM_EPS),
                    lax.rsqrt(ms_hi * (1.0 / DF_HEAD_DIM) + NORM_EPS))
    return x * inv * g2


def _diff_attn_kernel(sc_ref, q_ref, k_ref, v_ref, qg_ref, kg_ref, sg_ref, o_ref, kn_ref, vb_ref, *,
                      out_scale):
    h = pl.program_id(1)
    qi = pl.program_id(2)
    tq = q_ref.shape[1]
    L = k_ref.shape[1]
    lane = lax.broadcasted_iota(jnp.int32, (1, LANES), 1)
    lo_mask = lane < DF_HEAD_DIM

    def prepare_kv():
        kn_ref[...] = _half_rms(k_ref[0], kg_ref[...], lo_mask).astype(BF16)
        vb_ref[:, :LANES] = v_ref[0].astype(BF16)
        vb_ref[:, LANES:] = jnp.ones((L, LANES), BF16)

    if tq == L:
        prepare_kv()
    else:
        pl.when(qi == 0)(prepare_kv)

    lam = sc_ref[0]
    slope = sc_ref[1 + h] * LOG2E
    ts = min(ATTN_STRIP, tq)
    col = lax.broadcasted_iota(jnp.int32, (1, L), 1).astype(F32)

    def scores(i):
        r0 = i * ts
        qn = _half_rms(q_ref[0, r0:r0 + ts, :], qg_ref[...], lo_mask) * (DF_HEAD_DIM ** -0.5 * LOG2E)
        row = (lax.broadcasted_iota(jnp.int32, (ts, 1), 0) + (qi * tq + r0)).astype(F32)
        bias = jnp.abs(row - col) * slope
        kn = kn_ref[...]
        return (_dot_nt(jnp.where(lo_mask, qn, 0.0).astype(BF16), kn) - bias,
                _dot_nt(jnp.where(lo_mask, 0.0, qn).astype(BF16), kn) - bias)

    def exps(i, ss):
        return tuple(jnp.exp2(s - jnp.max(s, axis=-1, keepdims=True)).astype(BF16) for s in ss)

    def values(i, es):
        r1, r2 = (_dot(e, vb_ref[...]) for e in es)
        o = (r1[:, :LANES] * (1.0 / r1[:, LANES:LANES + 1])
             - r2[:, :LANES] * (lam / r2[:, LANES:LANES + 1]))
        o_ref[0, i * ts:(i + 1) * ts, :] = (_rms(o, sg_ref[...]) * out_scale).astype(o_ref.dtype)

    _pipeline3(tq // ts, scores, exps, values)


def diff_attention(proj, scal, q_g, k_g, subln_g, *, q_blk0, k_blk0, v_blk0, out_scale, tq):
    B, L, _ = proj.shape
    tq = min(tq, L)
    qg2 = jnp.concatenate([q_g, q_g]).reshape(1, LANES)
    kg2 = jnp.concatenate([k_g, k_g]).reshape(1, LANES)
    return pl.pallas_call(
        functools.partial(_diff_attn_kernel, out_scale=out_scale),
        grid=(B, DF_HEADS, L // tq),
        in_specs=[pl.BlockSpec(memory_space=pltpu.SMEM),
                  pl.BlockSpec((1, tq, LANES), lambda b, h, i: (b, i, q_blk0 + h)),
                  pl.BlockSpec((1, L, LANES), lambda b, h, i: (b, 0, k_blk0 + h)),
                  pl.BlockSpec((1, L, LANES), lambda b, h, i: (b, 0, v_blk0 + h)),
                  pl.BlockSpec((1, LANES), lambda b, h, i: (0, 0)),
                  pl.BlockSpec((1, LANES), lambda b, h, i: (0, 0)),
                  pl.BlockSpec((1, LANES), lambda b, h, i: (0, 0))],
        out_specs=pl.BlockSpec((1, tq, LANES), lambda b, h, i: (b, i, h)),
        out_shape=jax.ShapeDtypeStruct((B, L, DF_HEADS * DF_V_DIM), BF16),
        scratch_shapes=[pltpu.VMEM((L, LANES), BF16), pltpu.VMEM((L, 2 * LANES), BF16)],
        compiler_params=_params("parallel", "parallel", "arbitrary"),
        name="diff_attention",
    )(scal, proj, proj, proj, qg2, kg2, subln_g.reshape(1, LANES))


def _rope(x, cos, sin, first_mask):
    q = GQ_HEAD_DIM // 4
    rot = jnp.where(first_mask, -pltpu.roll(x, LANES - q, axis=1), pltpu.roll(x, q, axis=1))
    return x * cos + rot * sin


def _gqa_kernel(q_ref, k_ref, v_ref, cq_ref, sq_ref, ck_ref, sk_ref, qg_ref, kg_ref, o_ref, kn_ref, vb_ref):
    qi = pl.program_id(2)
    lane = lax.broadcasted_iota(jnp.int32, (1, LANES), 1)
    first_mask = (lane % (GQ_HEAD_DIM // 2)) < (GQ_HEAD_DIM // 4)

    @pl.when(qi == 0)
    def _():
        kn = _rms(k_ref[0], kg_ref[...])
        kn_ref[...] = _rope(kn, ck_ref[...], sk_ref[...], first_mask).astype(BF16)
        vb_ref[:, :LANES] = v_ref[0].astype(BF16)
        vb_ref[:, LANES:] = jnp.ones((k_ref.shape[1], LANES), BF16)

    scale = GQ_HEAD_DIM ** -0.5 * LOG2E
    tq = q_ref.shape[1]
    ts = min(ATTN_STRIP, tq)
    nr = tq // ts

    def where(i):
        r0 = (i % nr) * ts
        return slice(r0, r0 + ts), slice((i // nr) * GQ_HEAD_DIM, (i // nr + 1) * GQ_HEAD_DIM)

    def scores(i):
        rs, cs = where(i)
        qn = _rope(_rms(q_ref[0, rs, cs], qg_ref[...]), cq_ref[rs, :], sq_ref[rs, :], first_mask) * scale
        return _dot_nt(qn.astype(BF16), kn_ref[...])

    def exps(i, s):
        return jnp.exp2(s - jnp.max(s, axis=-1, keepdims=True)).astype(BF16)

    def values(i, e):
        rs, cs = where(i)
        res = _dot(e, vb_ref[...])
        o_ref[0, rs, cs] = (res[:, :LANES] * (1.0 / res[:, LANES:LANES + 1])).astype(o_ref.dtype)

    _pipeline3(GQ_GROUP * nr, scores, exps, values)


def gqa_attention(proj, cos, sin, q_g, k_g, *, q_blk0, k_blk0, v_blk0, tq):
    B, L, _ = proj.shape
    tq = min(tq, L)
    gw = GQ_GROUP * GQ_HEAD_DIM
    return pl.pallas_call(
        _gqa_kernel,
        grid=(B, GQ_KV_HEADS, L // tq),
        in_specs=[pl.BlockSpec((1, tq, gw), lambda b, g, i: (b, i, q_blk0 + g)),
                  pl.BlockSpec((1, L, LANES), lambda b, g, i: (b, 0, k_blk0 + g)),
                  pl.BlockSpec((1, L, LANES), lambda b, g, i: (b, 0, v_blk0 + g)),
                  pl.BlockSpec((tq, LANES), lambda b, g, i: (i, 0)),
                  pl.BlockSpec((tq, LANES), lambda b, g, i: (i, 0)),
                  pl.BlockSpec((L, LANES), lambda b, g, i: (0, 0)),
                  pl.BlockSpec((L, LANES), lambda b, g, i: (0, 0)),
                  pl.BlockSpec((1, LANES), lambda b, g, i: (0, 0)),
                  pl.BlockSpec((1, LANES), lambda b, g, i: (0, 0))],
        out_specs=pl.BlockSpec((1, tq, gw), lambda b, g, i: (b, i, g)),
        out_shape=jax.ShapeDtypeStruct((B, L, GQ_WIDTH), BF16),
        scratch_shapes=[pltpu.VMEM((L, LANES), BF16), pltpu.VMEM((L, 2 * LANES), BF16)],
        compiler_params=_params("parallel", "parallel", "arbitrary"),
        name="gqa_attention",
    )(proj, proj, proj, cos, sin, cos, sin, q_g.reshape(1, LANES), k_g.reshape(1, LANES))


def _hgrn2_pair_masks():
    pos = lax.broadcasted_iota(jnp.int32, (1, HG_BLOCK, LANES), 1)
    return {rev: [jnp.where((pos <= s) if rev else (pos >= s), 0.0, -jnp.inf).astype(F32)
                  for s in range(HG_BLOCK)] for rev in (False, True)}


def _hgrn2_intra_streams(streams, b_scr, pair_mask):
    C = HG_CHUNK
    nb = C // HG_BLOCK
    ns = range(len(streams))
    rev = [st[5] for st in streams]
    r_i = lax.broadcasted_iota(jnp.int32, (C, C), 0)
    c_i = lax.broadcasted_iota(jnp.int32, (C, C), 1)

    kk, lg = [], []
    for q, fz, v, vb, lb, _ in streams:
        f = lb + (1.0 - lb) * jax.nn.sigmoid(fz)
        kk.append(1.0 - f)
        lg.append(jnp.log(f) * LOG2E)

    tri_f = jnp.where(r_i >= c_i, 1.0, 0.0).astype(BF16)
    tri_b = jnp.where(r_i <= c_i, 1.0, 0.0).astype(BF16)
    b = []
    for i in ns:
        hi = lg[i].astype(BF16)
        r1 = lg[i] - hi.astype(F32)
        mid = r1.astype(BF16)
        lo = (r1 - mid.astype(F32)).astype(BF16)
        tri = tri_b if rev[i] else tri_f
        b.append(_dot(tri, hi) + _dot(tri, mid) + _dot(tri, lo))
    b_end = [b[i][0:1, :] if rev[i] else b[i][C - 1:C, :] for i in ns]
    qe = [(streams[i][0] * jnp.exp2(b[i])).astype(BF16) for i in ns]
    u = [_dot_tn(streams[i][3], (kk[i] * jnp.exp2(b_end[i] - b[i])).astype(BF16)) for i in ns]

    b3 = [b[i].reshape(nb, HG_BLOCK, LANES) for i in ns]
    q3 = [streams[i][0].reshape(nb, HG_BLOCK, LANES) for i in ns]
    seg_pos = lax.broadcasted_iota(jnp.int32, (C, HG_BLOCK * LANES), 1) // LANES
    row_pos = lax.broadcasted_iota(jnp.int32, (C, HG_BLOCK * LANES), 0) % HG_BLOCK
    kcat = []
    for i in ns:
        kb = kk[i].astype(BF16)
        kcat.append(jnp.where(seg_pos == row_pos, jnp.concatenate([kb] * HG_BLOCK, axis=1),
                              jnp.zeros((), BF16)))
    for i in ns:
        b_scr[i] = b[i]
    ps = [[] for i in ns]
    for s in range(HG_BLOCK):
        for i in ns:
            key_b = jnp.concatenate(
                [jnp.broadcast_to(b_scr[i, blk * HG_BLOCK + s:blk * HG_BLOCK + s + 1, :], (HG_BLOCK, LANES))
                 for blk in range(nb)], axis=0).reshape(nb, HG_BLOCK, LANES)
            d = (b3[i] - key_b) + pair_mask[rev[i]][s]
            ps[i].append((q3[i] * jnp.exp2(d)).reshape(C, LANES).astype(BF16))
    same_blk = (r_i // HG_BLOCK) == (c_i // HG_BLOCK)
    adiag = [_dot_nt(jnp.concatenate(ps[i], axis=1), kcat[i]) for i in ns]

    zrow = jnp.zeros((1, 1, LANES), F32)
    qt = []
    for i in ns:
        if rev[i]:
            rblk3 = jnp.concatenate([b3[i][1:, 0:1, :], zrow], axis=0)
        else:
            rblk3 = jnp.concatenate([zrow, b3[i][:nb - 1, HG_BLOCK - 1:HG_BLOCK, :]], axis=0)
        qt.append((q3[i] * jnp.exp2(b3[i] - rblk3)).reshape(C, LANES))
    rows = [[] for i in ns]
    for blk in range(nb):
        for i in ns:
            if rev[i]:
                edge = (blk + 1) * HG_BLOCK
                if edge == C:
                    rows[i].append(jnp.zeros((HG_BLOCK, C), F32))
                    continue
                kt = kk[i][edge:, :] * jnp.exp2(b[i][edge:edge + 1, :] - b[i][edge:, :])
                kfull = jnp.concatenate([jnp.zeros((edge, LANES), F32), kt], axis=0)
            else:
                edge = blk * HG_BLOCK
                if edge == 0:
                    rows[i].append(jnp.zeros((HG_BLOCK, C), F32))
                    continue
                kt = kk[i][:edge, :] * jnp.exp2(b[i][edge - 1:edge, :] - b[i][:edge, :])
                kfull = jnp.concatenate([kt, jnp.zeros((C - edge, LANES), F32)], axis=0)
            rows[i].append(_dot_nt(qt[i][blk * HG_BLOCK:(blk + 1) * HG_BLOCK, :].astype(BF16),
                                   kfull.astype(BF16)))
    outs = []
    for i in ns:
        att = jnp.where(same_blk, adiag[i], jnp.concatenate(rows[i], axis=0))
        o = _dot(att.astype(BF16), streams[i][3])
        outs.append((o, qe[i], u[i], jnp.exp2(b_end[i])))
    return outs


def _hgrn2_kernel(q_ref, ff_ref, fb_ref, i_ref, g_ref, lbf_ref, lbb_ref, ng_ref, o_ref, of_ref, ob_ref,
                  sf_ref, sb_ref, b_scr):
    L = q_ref.shape[1]
    C = HG_CHUNK
    n = L // C
    gsize = math.gcd(HG_INTRA_CHUNKS, n)
    qscale = HG_DK ** -0.5

    def rows(c):
        return pl.ds(pl.multiple_of(c * C, C), C)

    sf_ref[...] = jnp.zeros_like(sf_ref)
    sb_ref[...] = jnp.zeros_like(sb_ref)
    pair_mask = _hgrn2_pair_masks()

    def group(it, carry):
        streams = []
        dests = []
        for k in range(gsize):
            c = it * gsize + k
            for fz_ref, lb_ref, o_s, s_s, cc, reverse in ((ff_ref, lbf_ref, of_ref, sf_ref, c, False),
                                                          (fb_ref, lbb_ref, ob_ref, sb_ref, n - 1 - c, True)):
                r = rows(cc)
                v = i_ref[0, r, :]
                streams.append((q_ref[0, r, :] * qscale, fz_ref[0, r, :], v, v.astype(BF16), lb_ref[0], reverse))
                dests.append((o_s, s_s, r))
        state = {id(sf_ref): sf_ref[...], id(sb_ref): sb_ref[...]}
        for (o, qe, u, e), (o_s, s_s, r) in zip(_hgrn2_intra_streams(streams, b_scr, pair_mask), dests):
            st = state[id(s_s)]
            o_s[r, :] = o + _dot_nt(qe, st.astype(BF16))
            state[id(s_s)] = st * e + u
        sf_ref[...] = state[id(sf_ref)]
        sb_ref[...] = state[id(sb_ref)]
        return carry

    lax.fori_loop(0, n // gsize, group, 0)
    o = _rms(of_ref[...] + ob_ref[...], ng_ref[...])
    g = g_ref[0]
    o_ref[0] = (o * (g * jax.nn.sigmoid(g))).astype(o_ref.dtype)


def hgrn2_mixer(proj, lb_f, lb_b, norm_g):
    B, L, _ = proj.shape
    H = HG_HEADS

    def col(k):
        return pl.BlockSpec((1, L, LANES), lambda b, h: (b, 0, k * H + h))

    lbspec = pl.BlockSpec((1, 1, LANES), lambda b, h: (h, 0, 0))
    return pl.pallas_call(
        _hgrn2_kernel,
        grid=(B, H),
        in_specs=[col(0), col(1), col(2), col(3), col(4), lbspec, lbspec,
                  pl.BlockSpec((1, LANES), lambda b, h: (0, 0))],
        out_specs=pl.BlockSpec((1, L, LANES), lambda b, h: (b, 0, h)),
        out_shape=jax.ShapeDtypeStruct((B, L, HG_WIDTH), BF16),
        scratch_shapes=[pltpu.VMEM((L, LANES), F32), pltpu.VMEM((L, LANES), F32),
                        pltpu.VMEM((LANES, LANES), F32), pltpu.VMEM((LANES, LANES), F32),
                        pltpu.VMEM((2 * math.gcd(HG_INTRA_CHUNKS, L // HG_CHUNK), HG_CHUNK, LANES), F32)],
        compiler_params=_params("parallel", "arbitrary"),
        name="hgrn2",
    )(proj, proj, proj, proj, proj, lb_f.reshape(H, 1, LANES), lb_b.reshape(H, 1, LANES),
      norm_g.reshape(1, LANES))


def _hyena_filter_kernel(z_ref, t_ref, w1_ref, b1_ref, w2_ref, b2_ref, w3_ref, b3_ref, w4f_ref, w4b_ref,
                         fr_ref, ad_ref, gs_ref, gd_ref, kn_ref):
    def hdot(a, b):
        return jnp.dot(a, b, precision=HIGHEST, preferred_element_type=F32)

    fr = fr_ref[...]
    h = jnp.sin(fr * (hdot(z_ref[...], w1_ref[...]) + b1_ref[...]))
    h = jnp.sin(fr * (hdot(h, w2_ref[...]) + b2_ref[...]))
    h = jnp.sin(fr * (hdot(h, w3_ref[...]) + b3_ref[...]))
    window = jnp.exp(-t_ref[...] * ad_ref[...])
    hf = hdot(h, w4f_ref[...]) * window
    hb = hdot(h, w4b_ref[...]) * window
    row = lax.broadcasted_iota(jnp.int32, hb.shape, 0)
    hb = jnp.where(row == 0, 0.0, hb)
    gs_ref[...] = hf + hb
    gd_ref[...] = hb - hf
    sgn = jnp.where(row % 2 == 0, 1.0, -1.0)
    kn_ref[...] = jnp.sum((hf + hb) * sgn, axis=0, keepdims=True) * (0.5 / hb.shape[0])


def _split2(x):
    hi = x.astype(BF16)
    return hi, (x - hi.astype(F32)).astype(BF16)


def _hyena_spectrum_kernel(ah_ref, al_ref, bh_ref, bl_ref, gs_ref, gd_ref, kre_ref, kim_ref, sh_ref, sl_ref,
                           dh_ref, dl_ref, *, n_fft):
    @pl.when(pl.program_id(1) == 0)
    def _():
        sh_ref[...], sl_ref[...] = _split2(gs_ref[...])
        dh_ref[...], dl_ref[...] = _split2(gd_ref[...])

    tf = ah_ref.shape[0]
    f0 = pl.program_id(1) * tf
    row = lax.broadcasted_iota(jnp.int32, (tf, 1), 0) + f0
    wf = jnp.where(row == 0, 1.0 / n_fft, 2.0 / n_fft)
    ah = ah_ref[...]
    bh = bh_ref[...]
    kre_ref[...] = wf * (_dot(ah, sh_ref[...]) + _dot(ah, sl_ref[...]) + _dot(al_ref[...], sh_ref[...]))
    kim_ref[...] = wf * (_dot(bh, dh_ref[...]) + _dot(bh, dl_ref[...]) + _dot(bl_ref[...], dh_ref[...]))


def hyena_filter_spectrum(L, cos_hi, cos_lo, sin_hi, sin_lo, w1, b1, w2, b2, w3, b3, w4, sin_freq):
    C = HY_WIDTH
    t = jnp.linspace(0.0, 1.0, L, dtype=F32)[:, None]
    w = (2.0 * math.pi / L) * jnp.arange(L, dtype=F32)[:, None]
    f = jnp.linspace(1e-4, HY_BANDS - 1, HY_BANDS, dtype=F32)[None, :]
    z = jnp.concatenate([t, jnp.cos(f * w), -jnp.sin(f * w)], axis=-1)
    zp = jnp.pad(z, ((0, 0), (0, LANES - HY_EMB)))
    w1p = jnp.pad(w1, ((0, LANES - HY_EMB), (0, 0)))
    max_decay = math.log(HY_TARGET) / HY_FAST_PCT
    min_decay = math.log(HY_TARGET) / HY_SLOW_PCT
    absd = jnp.abs(jnp.linspace(min_decay, max_decay, C, dtype=F32))[None, :]
    tc = min(512, C)
    full = lambda shape: pl.BlockSpec(shape, lambda j: (0, 0))
    gs, gd, knyq = pl.pallas_call(
        _hyena_filter_kernel,
        grid=(C // tc,),
        in_specs=[full((L, LANES)), full((L, 1)), full((LANES, HY_FILT)), full((1, HY_FILT)),
                  full((HY_FILT, HY_FILT)), full((1, HY_FILT)), full((HY_FILT, HY_FILT)), full((1, HY_FILT)),
                  pl.BlockSpec((HY_FILT, tc), lambda j: (0, j)),
                  pl.BlockSpec((HY_FILT, tc), lambda j: (0, C // tc + j)),
                  full((1, HY_FILT)),
                  pl.BlockSpec((1, tc), lambda j: (0, j))],
        out_specs=[pl.BlockSpec((L, tc), lambda j: (0, j)), pl.BlockSpec((L, tc), lambda j: (0, j)),
                   pl.BlockSpec((1, tc), lambda j: (0, j))],
        out_shape=[jax.ShapeDtypeStruct((L, C), F32), jax.ShapeDtypeStruct((L, C), F32),
                   jax.ShapeDtypeStruct((1, C), F32)],
        compiler_params=_params("arbitrary"),
        name="hyena_filter",
    )(zp, t, w1p, b1.reshape(1, -1), w2, b2.reshape(1, -1), w3, b3.reshape(1, -1), w4, w4,
      sin_freq.reshape(1, -1), absd)

    tf = min(256, L)
    tbl = pl.BlockSpec((tf, L), lambda j, i: (i, 0))
    kre, kim = pl.pallas_call(
        functools.partial(_hyena_spectrum_kernel, n_fft=2 * L),
        grid=(C // tc, L // tf),
        in_specs=[tbl, tbl, tbl, tbl,
                  pl.BlockSpec((L, tc), lambda j, i: (0, j)),
                  pl.BlockSpec((L, tc), lambda j, i: (0, j))],
        out_specs=[pl.BlockSpec((tf, tc), lambda j, i: (i, j)), pl.BlockSpec((tf, tc), lambda j, i: (i, j))],
        out_shape=[jax.ShapeDtypeStruct((L, C), F32), jax.ShapeDtypeStruct((L, C), F32)],
        scratch_shapes=[pltpu.VMEM((L, tc), BF16)] * 4,
        compiler_params=_params("parallel", "arbitrary"),
        name="hyena_spectrum",
    )(cos_hi, cos_lo, sin_hi, sin_lo, gs, gd)
    return kre, kim, knyq


def _short_conv(u, w_ref, b_ref):
    L = u.shape[0]
    row = lax.broadcasted_iota(jnp.int32, (L, 1), 0)
    prev = jnp.where(row == 0, 0.0, pltpu.roll(u, 1, axis=0))
    nxt = jnp.where(row == L - 1, 0.0, pltpu.roll(u, L - 1, axis=0))
    return prev * w_ref[0:1, :] + u * w_ref[1:2, :] + nxt * w_ref[2:3, :] + b_ref[...]


def _hyena_conv_kernel(x1_ref, x2_ref, v_ref, cw1_ref, cw2_ref, cw3_ref, cb1_ref, cb2_ref, cb3_ref,
                       a_ref, b_ref, kre_ref, kim_ref, knyq_ref, bias_ref, o_ref, r1_ref, r2_ref, *, tf):
    L = v_ref.shape[1]
    x2 = _short_conv(x2_ref[0], cw2_ref, cb2_ref)
    vv = _short_conv(v_ref[0], cw3_ref, cb3_ref)
    vx = vv * x2
    vxb = vx.astype(BF16)
    sgn = jnp.where(lax.broadcasted_iota(jnp.int32, (L, 1), 0) % 2 == 0, 1.0, -1.0)
    def forward(fb):
        fs = slice(fb * tf, (fb + 1) * tf)
        return _dot(a_ref[fs, :], vxb), _dot(b_ref[fs, :], vxb)

    def product(fb, pq):
        p, q = pq
        fs = slice(fb * tf, (fb + 1) * tf)
        kre = kre_ref[fs, :]
        kim = kim_ref[fs, :]
        r1_ref[fs, :] = (p * kre + q * kim).astype(BF16)
        r2_ref[fs, :] = (q * kre - p * kim).astype(BF16)

    _pipeline3(L // tf, forward, product, lambda fb, _: None)

    v_nyq = jnp.sum(vx * sgn, axis=0, keepdims=True)
    y0 = vx * bias_ref[...] + sgn * (v_nyq * knyq_ref[...])
    x1 = _short_conv(x1_ref[0], cw1_ref, cb1_ref)
    y = _dot(a_ref[...], r1_ref[...]) + _dot(b_ref[...], r2_ref[...])
    o_ref[0] = ((y0 + y) * x1).astype(o_ref.dtype)


def hyena_conv(proj, conv_w, conv_b, cos_b, sin_b, kre, kim, knyq, bias, *, tc):
    B, L, _ = proj.shape
    C = HY_WIDTH
    tc = min(tc, C)
    nc = C // tc
    tf = min(1024, L)

    def grp(k):
        return pl.BlockSpec((1, L, tc), lambda c, b: (b, 0, k * nc + c))

    def cw(k):
        return pl.BlockSpec((HY_SHORT, tc), lambda c, b: (0, k * nc + c))

    def cb(k):
        return pl.BlockSpec((1, tc), lambda c, b: (0, k * nc + c))

    tbl = pl.BlockSpec((L, L), lambda c, b: (0, 0), pipeline_mode=pl.Buffered(1))
    chan = pl.BlockSpec((L, tc), lambda c, b: (0, c), pipeline_mode=pl.Buffered(1))
    vec = pl.BlockSpec((1, tc), lambda c, b: (0, c))
    return pl.pallas_call(
        functools.partial(_hyena_conv_kernel, tf=tf),
        grid=(nc, B),
        in_specs=[grp(0), grp(1), grp(2), cw(0), cw(1), cw(2), cb(0), cb(1), cb(2),
                  tbl, tbl, chan, chan, vec, vec],
        out_specs=pl.BlockSpec((1, L, tc), lambda c, b: (b, 0, c)),
        out_shape=jax.ShapeDtypeStruct((B, L, C), BF16),
        scratch_shapes=[pltpu.VMEM((L, tc), BF16), pltpu.VMEM((L, tc), BF16)],
        compiler_params=_params("parallel", "arbitrary"),
        name="hyena_conv",
    )(proj, proj, proj, conv_w, conv_w, conv_w, conv_b.reshape(1, -1), conv_b.reshape(1, -1),
      conv_b.reshape(1, -1), cos_b, sin_b, kre, kim, knyq, bias.reshape(1, -1))


def _dft_tables(L):
    n = 2 * L
    r = math.gcd(DFT_ROW_SPLIT, L)
    j = jnp.arange(L, dtype=jnp.int32)[None, :]

    def cs(f):
        ang = ((f[:, None] * j) % n).astype(F32) * (2.0 * math.pi / n)
        return jnp.cos(ang), jnp.sin(ang)

    c0, s0 = cs(jnp.arange(r, dtype=jnp.int32))
    c1, s1 = cs(jnp.arange(L // r, dtype=jnp.int32) * r)
    cos_m = (c1[:, None, :] * c0[None, :, :] - s1[:, None, :] * s0[None, :, :]).reshape(L, L)
    sin_m = (s1[:, None, :] * c0[None, :, :] + c1[:, None, :] * s0[None, :, :]).reshape(L, L)
    cos_hi = cos_m.astype(BF16)
    sin_hi = sin_m.astype(BF16)
    return (cos_hi, (cos_m - cos_hi.astype(F32)).astype(BF16),
            sin_hi, (sin_m - sin_hi.astype(F32)).astype(BF16))


def _axial_rope_tables(L):
    rows = L // GRID_W
    r, c = jnp.meshgrid(jnp.arange(rows, dtype=F32), jnp.arange(GRID_W, dtype=F32), indexing='ij')
    r = r.reshape(-1)
    c = c.reshape(-1)
    half = GQ_HEAD_DIM // 2
    inv = ROPE_THETA ** (-jnp.arange(0, half, 2, dtype=F32) / half)
    ang_r = r[:, None] * inv[None, :]
    ang_c = c[:, None] * inv[None, :]
    ang = jnp.concatenate([ang_r, ang_r, ang_c, ang_c], axis=-1)
    return jnp.cos(ang), jnp.sin(ang)


def _even_mixer(x, layer, norm_g, w_in, conv_w, conv_b, fw1, fb1, fw2, fb2, fw3, fb3, fw4, sin_freq,
                hy_bias, q_g, k_g, lam_q1, lam_k1, lam_q2, lam_k2, subln_g):
    B, L, D = x.shape
    x2 = x.reshape(B * L, D)
    proj = norm_matmul(x2, norm_g, w_in.astype(BF16), tm=1024, tn=1536).reshape(B, L, -1)
    cos_hi, cos_lo, sin_hi, sin_lo = _dft_tables(L)
    kre, kim, knyq = hyena_filter_spectrum(L, cos_hi, cos_lo, sin_hi, sin_lo, fw1, fb1, fw2, fb2, fw3, fb3, fw4,
                                           sin_freq)
    o_a = hyena_conv(proj, conv_w, conv_b, cos_hi, sin_hi, kre, kim, knyq, hy_bias,
                     tc=256)
    lam_init = 0.8 - 0.6 * math.exp(-0.3 * layer)
    lam = (jnp.exp(jnp.sum(lam_q1.astype(F32) * lam_k1.astype(F32)))
           - jnp.exp(jnp.sum(lam_q2.astype(F32) * lam_k2.astype(F32))) + lam_init)
    slopes = jnp.asarray(np.array([2.0 ** (-8.0 * (h + 1) / DF_HEADS) for h in range(DF_HEADS)],
                                  dtype=np.float32))
    scal = jnp.concatenate([lam.reshape(1), slopes]).astype(F32)
    nb_hy = 3 * HY_WIDTH // LANES
    nb_qk = DF_QK_WIDTH // LANES
    o_b = diff_attention(proj, scal, q_g, k_g, subln_g, q_blk0=nb_hy, k_blk0=nb_hy + nb_qk,
                         v_blk0=nb_hy + 2 * nb_qk, out_scale=1.0 - lam_init, tq=2048)
    return o_a, o_b


def _odd_mixer(x, norm_g, lb_f, lb_b, w_in, hg_norm_g, q_g, k_g):
    B, L, D = x.shape
    x2 = x.reshape(B * L, D)
    proj = norm_matmul(x2, norm_g, w_in.astype(BF16), tm=1024, tn=1664).reshape(B, L, -1)
    o_c = hgrn2_mixer(proj, lb_f, lb_b, hg_norm_g)
    cos, sin = _axial_rope_tables(L)
    gw = GQ_GROUP * GQ_HEAD_DIM
    q0 = 5 * HG_WIDTH
    k0 = q0 + GQ_WIDTH
    v0 = k0 + GQ_KV_HEADS * GQ_HEAD_DIM
    o_d = gqa_attention(proj, cos, sin, q_g, k_g, q_blk0=q0 // gw, k_blk0=k0 // LANES, v_blk0=v0 // LANES,
                        tq=1024)
    return o_c, o_d


def kernel(x, mem, norm_mix_g, norm_mem_q_g, norm_mem_kv_g, norm_ffn_g, ev_w_in, ev_w_out, hy_conv_w, hy_conv_b, hy_fw1, hy_fb1, hy_fw2, hy_fb2, hy_fw3, hy_fb3, hy_fw4, hy_sin_freq, hy_bias, df_q_g, df_k_g, df_lam_q1, df_lam_k1, df_lam_q2, df_lam_k2, df_subln_g, od_w_in, od_w_out, hg_lb_logits, hg_norm_g, gq_q_g, gq_k_g, ca_w_q, ca_w_kv, ca_w_out, ca_q_g, ca_k_g, mlp_w1, mlp_w2):
    B, L, D = x.shape
    M = mem.shape[1]
    depth = norm_mix_g.shape[0]
    lb_soft = jax.nn.softmax(hg_lb_logits.astype(F32), axis=1)
    lbs = jnp.cumsum(lb_soft, axis=1)
    lbs = lbs - lbs[:, :1]
    mem2 = mem.reshape(B * M, D)
    for layer in range(depth):
        j = layer // 2
        if layer % 2 == 0:
            a1, a2 = _even_mixer(x, layer, norm_mix_g[layer], ev_w_in[j], hy_conv_w[j], hy_conv_b[j],
                                 hy_fw1[j], hy_fb1[j], hy_fw2[j], hy_fb2[j], hy_fw3[j], hy_fb3[j], hy_fw4[j],
                                 hy_sin_freq[j], hy_bias[j], df_q_g[j], df_k_g[j], df_lam_q1[j], df_lam_k1[j],
                                 df_lam_q2[j], df_lam_k2[j], df_subln_g[j])
            w_mix = ev_w_out[j]
        else:
            a1, a2 = _odd_mixer(x, norm_mix_g[layer], lbs[0, layer], lbs[1, layer], od_w_in[j],
                                hg_norm_g[j], gq_q_g[j], gq_k_g[j])
            w_mix = od_w_out[j]
        kv = norm_matmul(mem2, norm_mem_kv_g[layer], ca_w_kv[layer].astype(BF16), tm=1024, tn=1024)
        x = mix_proj_cross_attention(a1, a2, w_mix.astype(BF16), x, kv.reshape(B, M, -1), norm_mem_q_g[layer],
                                     ca_w_q[layer].astype(BF16), ca_q_g[layer], ca_k_g[layer],
                                     ca_w_out[layer].astype(BF16), tq=512)
        x = mlp_residual(x.reshape(B * L, D), norm_ffn_g[layer], layer_weight_bf16(mlp_w1, layer),
                         layer_weight_bf16(mlp_w2, layer), tm=512, th=1024).reshape(B, L, D)
    return x
```
